```python
import math
import jax
import jax.numpy as jnp
from jax import lax
import numpy as np

D_MODEL = 1024
BATCH = 16
SEQ = 2048
DEPTH = 2

HEAD_DIM = 64
N_HEADS = D_MODEL // HEAD_DIM
RET_HEADS = (3 * N_HEADS) // 8
RWKV_HEADS = (3 * N_HEADS) // 8
DIFF_HEADS = N_HEADS - RET_HEADS - RWKV_HEADS
D_RET = RET_HEADS * HEAD_DIM
D_RWKV = RWKV_HEADS * HEAD_DIM
D_DIFF = DIFF_HEADS * HEAD_DIM
DIFF_QK_DIM = HEAD_DIM // 2
PROJ_SIZES = (D_RET,) * 4 + (D_RWKV,) * 4 + (D_DIFF,) * 3
D_IN = sum(PROJ_SIZES)
RET_CHUNK = 128
ATTN_BLOCK = 128
ROPE_BASE = 10000.0
RWKV_DECAY_LORA = 64
RWKV_AAA_LORA = 64
RWKV_GATE_LORA = 128
RWKV_SHIFT_FEATS = 6
D_FF = -(-8 * D_MODEL // (3 * 256)) * 256
REL_BUCKETS = 32
REL_MAX_DIST = 128
NORM_EPS = 1e-6
RET_GN_EPS = 1e-5
RWKV_GN_EPS = 64e-5

kernel_name = 'hybrid_ret_rwkv7_diffattn_encoder'


def rms_norm(x, g):
    xf = x.astype(jnp.float32)
    y = xf * lax.rsqrt(jnp.mean(xf * xf, axis=-1, keepdims=True) + NORM_EPS)
    return (y * g.astype(jnp.float32)).astype(x.dtype)


def head_group_norm(y, w, b, eps):
    H, N = y.shape[-2:]
    mu = jnp.mean(y, axis=-1, keepdims=True)
    var = jnp.mean(jnp.square(y - mu), axis=-1, keepdims=True)
    yn = (y - mu) * lax.rsqrt(var + eps)
    return yn * w.astype(jnp.float32).reshape(H, N) + b.astype(jnp.float32).reshape(H, N)


def rotary(x, pos):
    half = x.shape[-1] // 2
    freqs = ROPE_BASE ** (-jnp.arange(half, dtype=jnp.float32) / half)
    ang = pos.astype(jnp.float32)[:, None] * freqs[None, :]
    cos = jnp.cos(ang)[None, :, None, :]
    sin = jnp.sin(ang)[None, :, None, :]
    x1, x2 = x[..., :half], x[..., half:]
    return jnp.concatenate([x1 * cos - x2 * sin, x1 * sin + x2 * cos], axis=-1)


def retention_one_direction(q, k, v, log_gamma, include_diag):
    B, T, H, D = q.shape
    C = RET_CHUNK
    nc = T // C
    qc = q.reshape(B, nc, C, H, D)
    kc = k.reshape(B, nc, C, H, D)
    vc = v.reshape(B, nc, C, H, D)
    idx = jnp.arange(C, dtype=jnp.float32)
    dist = idx[:, None] - idx[None, :]
    mask = (dist >= 0) if include_diag else (dist > 0)
    decay_intra = jnp.where(mask[None],
                            jnp.exp(jnp.maximum(dist, 0.0)[None] * log_gamma[:, None, None]), 0.0)
    scores = jnp.einsum('bnihd,bnjhd->bnhij', qc, kc) * decay_intra
    intra = jnp.einsum('bnhij,bnjhe->bnihe', scores, vc)
    zeta = jnp.exp((C - 1 - idx)[None, :] * log_gamma[:, None])
    kv = jnp.einsum('bnjhd,hj,bnjhe->nbhde', kc, zeta, vc)
    chunk_decay = jnp.exp(C * log_gamma)[None, :, None, None]

    def step(state, kv_n):
        return chunk_decay * state + kv_n, state

    _, prev_states = lax.scan(step, jnp.zeros((B, H, D, D), jnp.float32), kv)
    xi = jnp.exp((idx + 1.0)[None, :] * log_gamma[:, None])
    cross = jnp.einsum('bnihd,nbhde,hi->bnihe', qc, prev_states, xi)
    return (intra + cross).reshape(B, T, H, D)


def retention_mixer(q, k, v, g, gn_w, gn_b, pos):
    dtype = q.dtype
    B, T, _ = q.shape
    f32 = jnp.float32
    qh = rotary(q.astype(f32).reshape(B, T, RET_HEADS, HEAD_DIM), pos) * HEAD_DIM ** -0.5
    kh = rotary(k.astype(f32).reshape(B, T, RET_HEADS, HEAD_DIM), pos)
    vh = v.astype(f32).reshape(B, T, RET_HEADS, HEAD_DIM)
    log_gamma = jnp.log1p(-jnp.exp2(-5.0 - jnp.arange(RET_HEADS, dtype=f32)))
    y_fwd = retention_one_direction(qh, kh, vh, log_gamma, True)
    y_bwd = jnp.flip(retention_one_direction(jnp.flip(qh, 1), jnp.flip(kh, 1), jnp.flip(vh, 1),
                                             log_gamma, False), 1)
    y = head_group_norm(y_fwd + y_bwd, gn_w, gn_b, RET_GN_EPS).reshape(B, T, D_RET)
    return (jax.nn.silu(g.astype(f32)) * y).astype(dtype)


def bidir_token_shift(f, mu):
    prev = jnp.pad(f[:, :-1], ((0, 0), (1, 0), (0, 0)))
    nxt = jnp.pad(f[:, 1:], ((0, 0), (0, 1), (0, 0)))
    return f + mu[0] * (prev - f) + mu[1] * (nxt - f)


def rwkv7_mixer(r, k, v, z, mu, w0, w1, w2, a0, a1, a2, g1, g2, k_k, k_a, r_k, ln_w, ln_b):
    dtype = r.dtype
    B, T, _ = r.shape
    H, N = RWKV_HEADS, HEAD_DIM
    f32 = jnp.float32
    mu = mu.astype(f32)
    xr = bidir_token_shift(r.astype(f32), mu[0])
    xk = bidir_token_shift(k.astype(f32), mu[1])
    xv = bidir_token_shift(v.astype(f32), mu[2])
    zf = z.astype(f32)
    xw = bidir_token_shift(zf, mu[3])
    xa = bidir_token_shift(zf, mu[4])
    xg = bidir_token_shift(zf, mu[5])
    w_lo = jnp.einsum('dbtr,drc->dbtc', jnp.tanh(jnp.einsum('btc,dcr->dbtr', xw, w1.astype(f32))), w2.astype(f32))
    w_raw = w0.astype(f32)[:, None, None, :] + w_lo
    decay = jnp.exp(-jnp.exp(-jax.nn.softplus(-w_raw) - 0.5))
    a = jax.nn.sigmoid(a0.astype(f32)[:, None, None, :]
                       + jnp.einsum('dbtr,drc->dbtc', jnp.einsum('btc,dcr->dbtr', xa, a1.astype(f32)), a2.astype(f32)))
    gate = jnp.einsum('btr,rc->btc', jax.nn.sigmoid(jnp.einsum('btc,cr->btr', xg, g1.astype(f32))), g2.astype(f32))
    rh = xr.reshape(B, T, H, N)
    kh = xk.reshape(B, T, H, N)
    vh = xv.reshape(B, T, H, N)
    kk = kh * k_k.astype(f32).reshape(H, N)
    kk = kk / jnp.maximum(jnp.linalg.norm(kk, axis=-1, keepdims=True), 1e-12)
    a = a.reshape(2, B, T, H, N)
    decay = decay.reshape(2, B, T, H, N)
    k_dir = kh[None] * (1.0 + (a - 1.0) * k_a.astype(f32).reshape(H, N))

    def orient_pair(t):
        return jnp.stack([t[0], jnp.flip(t[1], 1)])

    def orient_shared(t):
        return jnp.stack([t, jnp.flip(t, 1)])

    kk_o = orient_shared(kk)
    seqs = (orient_pair(decay), orient_pair(k_dir), orient_shared(vh), orient_shared(rh),
            -kk_o, kk_o * orient_pair(a))
    seqs = tuple(jnp.moveaxis(s, 2, 0) for s in seqs)

    def step(S, inp):
        w_t, k_t, v_t, r_t, a_t, b_t = inp
        S = (S * w_t[..., None, :]
             + jnp.einsum('dbhvk,dbhk->dbhv', S, a_t)[..., None] * b_t[..., None, :]
             + v_t[..., :, None] * k_t[..., None, :])
        return S, jnp.einsum('dbhvk,dbhk->dbhv', S, r_t)

    _, ys = lax.scan(step, jnp.zeros((2, B, H, N, N), f32), seqs)
    ys = jnp.moveaxis(ys, 0, 2)
    y = ys[0] + jnp.flip(ys[1], 1)
    y = head_group_norm(y, ln_w, ln_b, RWKV_GN_EPS)
    bonus = jnp.sum(rh * k_dir.sum(0) * r_k.astype(f32), axis=-1, keepdims=True) * vh
    return ((y + bonus).reshape(B, T, D_RWKV) * gate).astype(dtype)


def t5_bucket(rel):
    nb = REL_BUCKETS // 2
    max_exact = nb // 2
    n = jnp.abs(rel)
    nf = jnp.maximum(n, 1).astype(jnp.float32)
    large = max_exact + (jnp.log(nf / max_exact) / math.log(REL_MAX_DIST / max_exact)
                         * (nb - max_exact)).astype(jnp.int32)
    large = jnp.minimum(large, nb - 1)
    return jnp.where(rel > 0, nb, 0) + jnp.where(n < max_exact, n, large)


def diff_attention(q, k, v, lam_vecs, subln_w, rel_bias, lambda_init):
    dtype = q.dtype
    B, T, _ = q.shape
    H, d = DIFF_HEADS, DIFF_QK_DIM
    f32 = jnp.float32
    qh = q.astype(f32).reshape(B, T, H, 2, d) * d ** -0.5
    kh = k.astype(f32).reshape(B, T, H, 2, d)
    vh = v.astype(f32).reshape(B, T, H, 2 * d)
    lv = lam_vecs.astype(f32)
    lam = jnp.exp(jnp.sum(lv[0] * lv[1])) - jnp.exp(jnp.sum(lv[2] * lv[3])) + lambda_init
    nblk = T // ATTN_BLOCK
    q_blocks = jnp.moveaxis(qh.reshape(B, nblk, ATTN_BLOCK, H, 2, d), 1, 0)
    starts = jnp.arange(nblk, dtype=jnp.int32) * ATTN_BLOCK
    k_pos = jnp.arange(T, dtype=jnp.int32)
    table = rel_bias.astype(f32)

    def one_block(args):
        qb, start = args
        q_pos = start + jnp.arange(ATTN_BLOCK, dtype=jnp.int32)
        bias = jnp.moveaxis(table[t5_bucket(k_pos[None, :] - q_pos[:, None])], -1, 0)
        s = jnp.einsum('bqhcd,bkhcd->bchqk', qb, kh) + bias[None, None]
        p = jax.nn.softmax(s, axis=-1)
        attn = p[:, 0] - lam * p[:, 1]
        return jnp.einsum('bhqk,bkhe->bqhe', attn, vh)

    out = lax.map(one_block, (q_blocks, starts))
    out = jnp.moveaxis(out, 0, 1).reshape(B, T, H, 2 * d)
    out = out * lax.rsqrt(jnp.mean(out * out, axis=-1, keepdims=True) + NORM_EPS) * subln_w.astype(f32)
    return (out * (1.0 - lambda_init)).reshape(B, T, D_DIFF).astype(dtype)


def setup_inputs(seed: int = 0) -> dict:
    key = jax.random.key(seed)
    ks = jax.random.split(key, 32)
    f32 = jnp.float32

    def nrm(k, shape, scale):
        return jax.random.normal(k, shape, f32) * scale

    def gain(k, shape):
        return 1.0 + 0.05 * jax.random.normal(k, shape, f32)

    return {
        'x': nrm(ks[0], (BATCH, SEQ, D_MODEL), 1.0),
        'mix_norm_g': gain(ks[1], (DEPTH, D_MODEL)),
        'w_in': nrm(ks[2], (DEPTH, D_MODEL, D_IN), D_MODEL ** -0.5),
        'w_out': nrm(ks[3], (DEPTH, D_RET + D_RWKV + D_DIFF, D_MODEL), D_MODEL ** -0.5),
        'ret_gn_w': gain(ks[4], (DEPTH, D_RET)),
        'ret_gn_b': nrm(ks[5], (DEPTH, D_RET), 0.02),
        'rwkv_mu': jax.random.uniform(ks[6], (DEPTH, RWKV_SHIFT_FEATS, 2, D_RWKV), f32, 0.0, 0.5),
        'rwkv_w0': jax.random.uniform(ks[7], (DEPTH, 2, D_RWKV), f32, -5.0, 0.5),
        'rwkv_w1': nrm(ks[8], (DEPTH, 2, D_RWKV, RWKV_DECAY_LORA), D_RWKV ** -0.5),
        'rwkv_w2': nrm(ks[9], (DEPTH, 2, RWKV_DECAY_LORA, D_RWKV), 0.3 * RWKV_DECAY_LORA ** -0.5),
        'rwkv_a0': nrm(ks[10], (DEPTH, 2, D_RWKV), 0.5),
        'rwkv_a1': nrm(ks[11], (DEPTH, 2, D_RWKV, RWKV_AAA_LORA), D_RWKV ** -0.5),
        'rwkv_a2': nrm(ks[12], (DEPTH, 2, RWKV_AAA_LORA, D_RWKV), 0.3 * RWKV_AAA_LORA ** -0.5),
        'rwkv_g1': nrm(ks[13], (DEPTH, D_RWKV, RWKV_GATE_LORA), D_RWKV ** -0.5),
        'rwkv_g2': nrm(ks[14], (DEPTH, RWKV_GATE_LORA, D_RWKV), RWKV_GATE_LORA ** -0.5),
        'rwkv_k_k': 0.85 + 0.05 * jax.random.normal(ks[15], (DEPTH, D_RWKV), f32),
        'rwkv_k_a': gain(ks[16], (DEPTH, D_RWKV)),
        'rwkv_r_k': nrm(ks[17], (DEPTH, RWKV_HEADS, HEAD_DIM), 0.1),
        'rwkv_ln_w': gain(ks[18], (DEPTH, D_RWKV)),
        'rwkv_ln_b': nrm(ks[19], (DEPTH, D_RWKV), 0.02),
        'diff_lambda': nrm(ks[20], (DEPTH, 4, DIFF_QK_DIM), 0.1),
        'diff_subln_w': gain(ks[21], (DEPTH, 2 * DIFF_QK_DIM)),
        'rel_bias': nrm(ks[22], (REL_BUCKETS, DIFF_HEADS), 0.5),
        'ffn_norm_g': gain(ks[23], (DEPTH, D_MODEL)),
        'w_gate': nrm(ks[24], (DEPTH, D_MODEL, D_FF), D_MODEL ** -0.5),
        'w_up': nrm(ks[25], (DEPTH, D_MODEL, D_FF), D_MODEL ** -0.5),
        'w_down': nrm(ks[26], (DEPTH, D_FF, D_MODEL), D_FF ** -0.5),
        'final_norm_g': gain(ks[27], (D_MODEL,)),
    }


def reference(x, mix_norm_g, w_in, w_out, ret_gn_w, ret_gn_b, rwkv_mu, rwkv_w0, rwkv_w1, rwkv_w2,
              rwkv_a0, rwkv_a1, rwkv_a2, rwkv_g1, rwkv_g2, rwkv_k_k, rwkv_k_a, rwkv_r_k, rwkv_ln_w,
              rwkv_ln_b, diff_lambda, diff_subln_w, rel_bias, ffn_norm_g, w_gate, w_up, w_down,
              final_norm_g):
    T = x.shape[1]
    pos = jnp.arange(T, dtype=jnp.int32)
    split_at = np.cumsum(PROJ_SIZES)[:-1].tolist()
    for l in range(DEPTH):
        h = rms_norm(x, mix_norm_g[l])
        proj = jnp.einsum('btd,de->bte', h, w_in[l])
        (rq, rk, rv, rg, wr, wk, wv, wz, dq, dk, dv) = jnp.split(proj, split_at, axis=-1)
        y_ret = retention_mixer(rq, rk, rv, rg, ret_gn_w[l], ret_gn_b[l], pos)
        y_rwkv = rwkv7_mixer(wr, wk, wv, wz, rwkv_mu[l], rwkv_w0[l], rwkv_w1[l], rwkv_w2[l],
                             rwkv_a0[l], rwkv_a1[l], rwkv_a2[l], rwkv_g1[l], rwkv_g2[l],
                             rwkv_k_k[l], rwkv_k_a[l], rwkv_r_k[l], rwkv_ln_w[l], rwkv_ln_b[l])
        lambda_init = 0.8 - 0.6 * math.exp(-0.3 * l)
        y_diff = diff_attention(dq, dk, dv, diff_lambda[l], diff_subln_w[l], rel_bias, lambda_init)
        mixed = jnp.concatenate([y_ret, y_rwkv, y_diff], axis=-1)
        x = x + jnp.einsum('bte,ed->btd', mixed, w_out[l])
        h = rms_norm(x, ffn_norm_g[l])
        hidden = jax.nn.silu(jnp.einsum('btd,df->btf', h, w_gate[l])) * jnp.einsum('btd,df->btf', h, w_up[l])
        x = x + jnp.einsum('btf,fd->btd', hidden, w_down[l])
    return rms_norm(x, final_norm_g)
```

```python
import functools
import math

import numpy as np
import jax
import jax.numpy as jnp
from jax import lax
from jax.experimental import pallas as pl
from jax.experimental.pallas import tpu as pltpu

F32 = jnp.float32
BF16 = jnp.bfloat16
HI = lax.Precision.HIGHEST

D_MODEL = 1024
HEAD_DIM = 64
LANES = 128
N_HEADS = D_MODEL // HEAD_DIM
RET_HEADS = (3 * N_HEADS) // 8
RWKV_HEADS = (3 * N_HEADS) // 8
DIFF_HEADS = N_HEADS - RET_HEADS - RWKV_HEADS
D_RET = RET_HEADS * HEAD_DIM
D_RWKV = RWKV_HEADS * HEAD_DIM
D_DIFF = DIFF_HEADS * HEAD_DIM
DIFF_QK_DIM = HEAD_DIM // 2
D_IN = 4 * D_RET + 4 * D_RWKV + 3 * D_DIFF
ROPE_BASE = 10000.0
D_FF = -(-8 * D_MODEL // (3 * 256)) * 256
REL_BUCKETS = 32
REL_MAX_DIST = 128
NORM_EPS = 1e-6
RET_GN_EPS = 1e-5
RWKV_GN_EPS = 64e-5

RET_CHUNK = 256
RWKV_CHUNK = 64
ATTN_TQ = 256
VMEM_LIMIT = 56 * 1024 * 1024


def _cparams(sem):
    return pltpu.CompilerParams(dimension_semantics=sem, vmem_limit_bytes=VMEM_LIMIT)


def _dot(a, b, precision=None):
    return jnp.dot(a, b, preferred_element_type=F32, precision=precision)


def _dot_nt(a, b, precision=None):
    return lax.dot_general(a, b, (((1,), (1,)), ((), ())), preferred_element_type=F32,
                           precision=precision)


def _dot_tn(a, b, precision=None):
    return lax.dot_general(a, b, (((0,), (0,)), ((), ())), preferred_element_type=F32,
                           precision=precision)


def _head_avg_matrix(n):
    r = lax.broadcasted_iota(jnp.int32, (n, n), 0) // HEAD_DIM
    c = lax.broadcasted_iota(jnp.int32, (n, n), 1) // HEAD_DIM
    return jnp.where(r == c, 1.0 / HEAD_DIM, 0.0).astype(F32)


def _group_norm(y, avg, w, b, eps):
    mu = _dot(y, avg, HI)
    yc = y - mu
    var = _dot(yc * yc, avg, HI)
    return yc * lax.rsqrt(var + eps) * w + b


def _norm_matmul_kernel(x_ref, g_ref, w_ref, o_ref):
    x = x_ref[...]
    ms = jnp.mean(x * x, axis=-1, keepdims=True)
    h = x * lax.rsqrt(ms + NORM_EPS) * g_ref[...]
    o_ref[...] = _dot(h.astype(BF16), w_ref[...])


def _norm_matmul(x2, g, w_bf16, tm=256):
    m, d = x2.shape
    n = w_bf16.shape[1]
    return pl.pallas_call(
        _norm_matmul_kernel,
        grid=(m // tm,),
        in_specs=[pl.BlockSpec((tm, d), lambda i: (i, 0)),
                  pl.BlockSpec((1, d), lambda i: (0, 0)),
                  pl.BlockSpec((d, n), lambda i: (0, 0))],
        out_specs=pl.BlockSpec((tm, n), lambda i: (i, 0)),
        out_shape=jax.ShapeDtypeStruct((m, n), F32),
        compiler_params=_cparams(("parallel",)),
        name="norm_inproj",
    )(x2, g.reshape(1, d), w_bf16)


def _retention_kernel(q_ref, k_ref, v_ref, g_ref, cos_ref, sin_ref, gnw_ref, gnb_ref, o_ref,
                      qs_ref, ks_ref, acc_ref, *, seq, chunk, log_gamma):
    nc = seq // chunk
    lane = lax.broadcasted_iota(jnp.int32, (1, LANES), 1)
    first_half = (lane & (HEAD_DIM - 1)) < (HEAD_DIM // 2)
    head_masks = (lane < HEAD_DIM, lane >= HEAD_DIM)
    ri = lax.broadcasted_iota(jnp.int32, (LANES, LANES), 0)
    ci = lax.broadcasted_iota(jnp.int32, (LANES, LANES), 1)
    block_diag = (ri // HEAD_DIM) == (ci // HEAD_DIM)
    avg = _head_avg_matrix(LANES)
    ti = lax.broadcasted_iota(jnp.int32, (chunk, chunk), 0)
    tj = lax.broadcasted_iota(jnp.int32, (chunk, chunk), 1)
    dist = jnp.abs(ti - tj).astype(F32)
    pos = lax.broadcasted_iota(jnp.int32, (chunk, 1), 0).astype(F32)

    def rope(x, cos, sin):
        partner = jnp.where(first_half, pltpu.roll(x, LANES - HEAD_DIM // 2, 1),
                            pltpu.roll(x, HEAD_DIM // 2, 1))
        return x * cos + partner * sin

    for p in range(RET_HEADS // 2):
        cols = slice(p * LANES, (p + 1) * LANES)
        lg = jnp.where(head_masks[0], log_gamma[2 * p], log_gamma[2 * p + 1]).astype(F32)
        decay_mats = [jnp.exp(dist * log_gamma[2 * p + hh]) for hh in range(2)]
        xi_f = jnp.exp((pos + 1.0) * lg)
        xi_b = jnp.exp((chunk - pos) * lg)
        zeta_f = jnp.exp((chunk - 1.0 - pos) * lg)
        zeta_b = jnp.exp(pos * lg)
        dec_c = jnp.exp(chunk * lg)

        def fwd_step(n, state):
            rows = pl.ds(pl.multiple_of(n * chunk, chunk), chunk)
            cos = cos_ref[rows, :]
            sin = sin_ref[rows, :]
            q = rope(q_ref[rows, cols], cos, sin) * (HEAD_DIM ** -0.5)
            k = rope(k_ref[rows, cols], cos, sin)
            v = v_ref[rows, cols]
            qs_ref[rows, cols] = q
            ks_ref[rows, cols] = k
            kb = k.astype(BF16)
            vb = v.astype(BF16)
            o = _dot((q * xi_f).astype(BF16), state.astype(BF16))
            for hh in range(2):
                qm = jnp.where(head_masks[hh], q, 0.0).astype(BF16)
                s = _dot_nt(qm, kb) * decay_mats[hh]
                oh = _dot(s.astype(BF16), vb)
                o = o + jnp.where(head_masks[hh], oh, 0.0)
            acc_ref[rows, cols] = o
            kv = _dot_tn((k * zeta_f).astype(BF16), vb)
            return state * dec_c + jnp.where(block_diag, kv, 0.0)

        lax.fori_loop(0, nc, fwd_step, jnp.zeros((LANES, LANES), F32))

        def bwd_step(i, state):
            n = nc - 1 - i
            rows = pl.ds(pl.multiple_of(n * chunk, chunk), chunk)
            q = qs_ref[rows, cols]
            k = ks_ref[rows, cols]
            v = v_ref[rows, cols]
            o = acc_ref[rows, cols] + _dot((q * xi_b).astype(BF16), state.astype(BF16))
            y = _group_norm(o, avg, gnw_ref[:, cols], gnb_ref[:, cols], RET_GN_EPS)
            g = g_ref[rows, cols]
            o_ref[rows, cols] = g * jax.nn.sigmoid(g) * y
            kv = _dot_tn((k * zeta_b).astype(BF16), v.astype(BF16))
            return state * dec_c + jnp.where(block_diag, kv, 0.0)

        lax.fori_loop(0, nc, bwd_step, jnp.zeros((LANES, LANES), F32))


def _retention(proj, cos_tab, sin_tab, gn_w, gn_b, batch, seq):
    chunk = min(RET_CHUNK, seq)
    log_gamma = tuple(float(np.log1p(-np.exp2(-5.0 - h))) for h in range(RET_HEADS))
    kern = functools.partial(_retention_kernel, seq=seq, chunk=chunk, log_gamma=log_gamma)
    col = lambda j: pl.BlockSpec((seq, D_RET), lambda b, j=j: (b, j))
    full = lambda shape: pl.BlockSpec(shape, lambda b: (0,) * len(shape))
    return pl.pallas_call(
        kern,
        grid=(batch,),
        in_specs=[col(0), col(1), col(2), col(3), full((seq, LANES)), full((seq, LANES)),
                  full((1, D_RET)), full((1, D_RET))],
        out_specs=pl.BlockSpec((seq, D_RET), lambda b: (b, 0)),
        out_shape=jax.ShapeDtypeStruct((batch * seq, D_RET), F32),
        scratch_shapes=[pltpu.VMEM((seq, D_RET), F32)] * 3,
        compiler_params=_cparams(("parallel",)),
        name="retention",
    )(proj, proj, proj, proj, cos_tab, sin_tab, gn_w.reshape(1, D_RET), gn_b.reshape(1, D_RET))


def _rwkv_prep_kernel(r_ref, k_ref, v_ref, z_ref, rp_ref, kp_ref, vp_ref, zp_ref,
                      rn_ref, kn_ref, vn_ref, zn_ref, mu_ref, w0_ref, w1_ref, w2_ref,
                      a0_ref, a1_ref, a2_ref, g1_ref, g2_ref, kk_ref, ka_ref, rk_ref,
                      lw0_o, lw1_o, kd0_o, kd1_o, bb0_o, bb1_o, v_o, r_o, kk_o, gate_o, bonus_o,
                      *, tb):
    i = pl.program_id(1)
    has_prev = (i > 0).astype(F32)
    has_next = (i < pl.num_programs(1) - 1).astype(F32)
    row = lax.broadcasted_iota(jnp.int32, (tb, 1), 0)
    avg = _head_avg_matrix(D_RWKV)

    def shifted(f_ref, p_ref, n_ref):
        f = f_ref[...]
        prev = jnp.where(row == 0, p_ref[7:8, :] * has_prev, pltpu.roll(f, 1, 0))
        nxt = jnp.where(row == tb - 1, n_ref[0:1, :] * has_next, pltpu.roll(f, tb - 1, 0))
        return f, prev - f, nxt - f

    def mix(parts, idx):
        f, dp, dn = parts
        return f + mu_ref[idx, 0:1, :] * dp + mu_ref[idx, 1:2, :] * dn

    xr = mix(shifted(r_ref, rp_ref, rn_ref), 0)
    xk = mix(shifted(k_ref, kp_ref, kn_ref), 1)
    xv = mix(shifted(v_ref, vp_ref, vn_ref), 2)
    zparts = shifted(z_ref, zp_ref, zn_ref)
    xw = mix(zparts, 3)
    xa = mix(zparts, 4)
    xg = mix(zparts, 5)

    gate = _dot(jax.nn.sigmoid(_dot(xg, g1_ref[...], HI)), g2_ref[...], HI)
    kk = xk * kk_ref[...]
    norm = jnp.sqrt(_dot(kk * kk, avg, HI) * HEAD_DIM)
    kk = kk / jnp.maximum(norm, 1e-12)

    ksum = jnp.zeros_like(xk)
    outs = ((lw0_o, kd0_o, bb0_o), (lw1_o, kd1_o, bb1_o))
    for d in range(2):
        w_lo = _dot(jnp.tanh(_dot(xw, w1_ref[d], HI)), w2_ref[d], HI)
        w_raw = w0_ref[d:d + 1, :] + w_lo
        log_w = -math.exp(-0.5) * jax.nn.sigmoid(w_raw)
        a = jax.nn.sigmoid(a0_ref[d:d + 1, :] + _dot(_dot(xa, a1_ref[d], HI), a2_ref[d], HI))
        k_dir = xk * (1.0 + (a - 1.0) * ka_ref[...])
        ksum = ksum + k_dir
        lw_o, kd_o, bb_o = outs[d]
        lw_o[...] = log_w
        kd_o[...] = k_dir
        bb_o[...] = kk * a
    bonus = _dot(xr * ksum * rk_ref[...], avg, HI) * HEAD_DIM
    v_o[...] = xv
    r_o[...] = xr
    kk_o[...] = kk
    gate_o[...] = gate
    bonus_o[...] = bonus * xv


def _rwkv_prep(proj, mu, w0, w1, w2, a0, a1, a2, g1, g2, k_k, k_a, r_k, batch, seq):
    tb = min(256, seq)
    nt = seq // tb
    d = D_RWKV
    col0 = (4 * D_RET) // d
    main = lambda j: pl.BlockSpec((tb, d), lambda b, i, j=j: (b * nt + i, col0 + j))
    prev = lambda j: pl.BlockSpec(
        (8, d), lambda b, i, j=j: (b * (seq // 8) + jnp.maximum(i * (tb // 8) - 1, 0), col0 + j))
    nxt = lambda j: pl.BlockSpec(
        (8, d), lambda b, i, j=j: (b * (seq // 8) + jnp.minimum((i + 1) * (tb // 8), seq // 8 - 1),
                                   col0 + j))
    full = lambda shape: pl.BlockSpec(shape, lambda b, i: (0,) * len(shape))
    out_spec = pl.BlockSpec((tb, d), lambda b, i: (b * nt + i, 0))
    out_sd = jax.ShapeDtypeStruct((batch * seq, d), F32)
    row = lambda a: a.reshape(1, d)
    return pl.pallas_call(
        functools.partial(_rwkv_prep_kernel, tb=tb),
        grid=(batch, nt),
        in_specs=[main(0), main(1), main(2), main(3), prev(0), prev(1), prev(2), prev(3),
                  nxt(0), nxt(1), nxt(2), nxt(3),
                  full(mu.shape), full(w0.shape), full(w1.shape), full(w2.shape),
                  full(a0.shape), full(a1.shape), full(a2.shape), full(g1.shape), full(g2.shape),
                  full((1, d)), full((1, d)), full((1, d))],
        out_specs=[out_spec] * 11,
        out_shape=[out_sd] * 11,
        compiler_params=_cparams(("parallel", "parallel")),
        name="rwkv_features",
    )(*([proj] * 12), mu, w0, w1, w2, a0, a1, a2, g1, g2, row(k_k), row(k_a), row(r_k))


def _rwkv_scan_kernel(lw0_ref, lw1_ref, kd0_ref, kd1_ref, bb0_ref, bb1_ref, v_ref, r_ref, kk_ref,
                      gate_ref, bonus_ref, lnw_ref, lnb_ref, o_ref, acc_ref, *, seq, chunk):
    nc = seq // chunk
    c2 = 2 * chunk
    lane = lax.broadcasted_iota(jnp.int32, (1, LANES), 1)
    head_masks = (lane < HEAD_DIM, lane >= HEAD_DIM)
    avg = _head_avg_matrix(LANES)
    ri = lax.broadcasted_iota(jnp.int32, (c2, c2), 0)
    ci = lax.broadcasted_iota(jnp.int32, (c2, c2), 1)
    same_head = (ri // chunk) == (ci // chunk)
    rt = ri % chunk
    ct = ci % chunk
    eye = (ri == ci).astype(F32)
    ti = lax.broadcasted_iota(jnp.int32, (chunk, chunk), 0)
    tj = lax.broadcasted_iota(jnp.int32, (chunk, chunk), 1)

    def stack(x):
        return jnp.concatenate([jnp.where(head_masks[0], x, 0.0), jnp.where(head_masks[1], x, 0.0)],
                               axis=0)

    dirs = ((lw0_ref, kd0_ref, bb0_ref), (lw1_ref, kd1_ref, bb1_ref))
    for d in range(2):
        lw_ref, kd_ref, bb_ref = dirs[d]
        if d == 0:
            strict = same_head & (ct < rt)
            incl = same_head & (ct <= rt)
            cum_mat = (tj <= ti).astype(F32)
        else:
            strict = same_head & (ct > rt)
            incl = same_head & (ct >= rt)
            cum_mat = (tj >= ti).astype(F32)

        def step(i, state, d=d, lw_ref=lw_ref, kd_ref=kd_ref, bb_ref=bb_ref, strict=strict,
                 incl=incl, cum_mat=cum_mat):
            n = i if d == 0 else nc - 1 - i
            rows = pl.ds(pl.multiple_of(n * chunk, chunk), chunk)
            lw = lw_ref[rows, :]
            cum_in = _dot(cum_mat, lw, HI)
            cum_ex = cum_in - lw
            total = jnp.sum(lw, axis=0, keepdims=True)
            kk = kk_ref[rows, :]
            kd = kd_ref[rows, :]
            bb = bb_ref[rows, :]
            inv_p = jnp.exp(-cum_in)
            to_end = jnp.exp(total - cum_in)
            a_s = stack(-kk * jnp.exp(cum_ex))
            r_s = stack(r_ref[rows, :] * jnp.exp(cum_in))
            b_s = stack(bb * inv_p)
            k_s = stack(kd * inv_p)
            v_s = stack(v_ref[rows, :])
            bend_s = stack(bb * to_end)
            kend_s = stack(kd * to_end)

            ar = jnp.concatenate([a_s, r_s], axis=0)
            bk = jnp.concatenate([b_s, k_s], axis=0)
            gram = _dot_nt(ar, bk, HI)
            l_ab = jnp.where(strict, gram[:c2, :c2], 0.0)
            l_ak = jnp.where(strict, gram[:c2, c2:], 0.0)
            m_rb = jnp.where(incl, gram[c2:, :c2], 0.0)
            m_rk = jnp.where(incl, gram[c2:, c2:], 0.0)

            power = l_ab
            inv = eye + l_ab
            for _ in range(int(math.log2(chunk)) - 1):
                power = _dot(power, power, HI)
                inv = inv + _dot(inv, power, HI)

            ar_state = _dot_nt(ar, state, HI)
            rhs = ar_state[:c2] + _dot(l_ak, v_s, HI)
            u = _dot(inv, rhs, HI)
            y2 = ar_state[c2:] + _dot(m_rb, u, HI) + _dot(m_rk, v_s, HI)
            y = y2[:chunk] + y2[chunk:]
            if d == 0:
                acc_ref[rows, :] = y
            else:
                acc_ref[rows, :] = acc_ref[rows, :] + y
            return state * jnp.exp(total) + _dot_tn(u, bend_s, HI) + _dot_tn(v_s, kend_s, HI)

        lax.fori_loop(0, nc, step, jnp.zeros((LANES, LANES), F32))

    def finish(n, carry):
        rows = pl.ds(pl.multiple_of(n * chunk, chunk), chunk)
        y = _group_norm(acc_ref[rows, :], avg, lnw_ref[...], lnb_ref[...], RWKV_GN_EPS)
        o_ref[rows, :] = (y + bonus_ref[rows, :]) * gate_ref[rows, :]
        return carry

    lax.fori_loop(0, nc, finish, 0)


def _rwkv_scan(feats, ln_w, ln_b, batch, seq):
    chunk = min(RWKV_CHUNK, seq)
    npair = RWKV_HEADS // 2
    blk = pl.BlockSpec((seq, LANES), lambda b, p: (b, p))
    vec = pl.BlockSpec((1, LANES), lambda b, p: (0, p))
    return pl.pallas_call(
        functools.partial(_rwkv_scan_kernel, seq=seq, chunk=chunk),
        grid=(batch, npair),
        in_specs=[blk] * 11 + [vec, vec],
        out_specs=blk,
        out_shape=jax.ShapeDtypeStruct((batch * seq, D_RWKV), F32),
        scratch_shapes=[pltpu.VMEM((seq, LANES), F32)],
        compiler_params=_cparams(("parallel", "parallel")),
        name="rwkv_scan",
    )(*feats, ln_w.reshape(1, D_RWKV), ln_b.reshape(1, D_RWKV))


def _diff_attn_kernel(q_ref, k_ref, v_ref, band_ref, lam_ref, subw_ref, o_ref, *, seq, tq,
                      lambda_init):
    i = pl.program_id(2)
    lane = lax.broadcasted_iota(jnp.int32, (1, LANES), 1)
    lv = lam_ref[...]
    lam = (jnp.exp(jnp.sum(lv[0:1] * lv[1:2], axis=1, keepdims=True))
           - jnp.exp(jnp.sum(lv[2:3] * lv[3:4], axis=1, keepdims=True)) + lambda_init)
    q = q_ref[...] * (DIFF_QK_DIM ** -0.5)
    kb = k_ref[...].astype(BF16)
    vb = v_ref[...].astype(BF16)
    off = pl.multiple_of((seq - tq) - i * tq, LANES)
    out = jnp.zeros((tq, LANES), F32)
    for hh in range(2):
        bias = band_ref[hh, :, pl.ds(off, seq)]
        parts = []
        for c in range(2):
            lo = hh * HEAD_DIM + c * DIFF_QK_DIM
            qm = jnp.where((lane >= lo) & (lane < lo + DIFF_QK_DIM), q, 0.0).astype(BF16)
            s = _dot_nt(qm, kb) + bias
            e = jnp.exp(s - jnp.max(s, axis=-1, keepdims=True))
            denom = jnp.sum(e, axis=-1, keepdims=True)
            parts.append(_dot(e.astype(BF16), vb) / denom)
        head = parts[0] - lam * parts[1]
        in_head = (lane >= hh * HEAD_DIM) & (lane < (hh + 1) * HEAD_DIM)
        head = jnp.where(in_head, head, 0.0)
        ms = jnp.sum(head * head, axis=-1, keepdims=True) * (1.0 / HEAD_DIM)
        out = out + head * lax.rsqrt(ms + NORM_EPS)
    o_ref[...] = out * subw_ref[...] * (1.0 - lambda_init)


def _diff_attention(proj, band, lam_vecs, subln_w, lambda_init, batch, seq):
    tq = min(ATTN_TQ, seq)
    nq = seq // tq
    npair = DIFF_HEADS // 2
    col0 = (4 * D_RET + 4 * D_RWKV) // LANES
    width = band.shape[-1]
    return pl.pallas_call(
        functools.partial(_diff_attn_kernel, seq=seq, tq=tq, lambda_init=lambda_init),
        grid=(batch, npair, nq),
        in_specs=[pl.BlockSpec((tq, LANES), lambda b, p, i: (b * nq + i, col0 + p)),
                  pl.BlockSpec((seq, LANES), lambda b, p, i: (b, col0 + npair + p)),
                  pl.BlockSpec((seq, LANES), lambda b, p, i: (b, col0 + 2 * npair + p)),
                  pl.BlockSpec((2, tq, width), lambda b, p, i: (p, 0, 0)),
                  pl.BlockSpec((4, DIFF_QK_DIM), lambda b, p, i: (0, 0)),
                  pl.BlockSpec((1, LANES), lambda b, p, i: (0, 0))],
        out_specs=pl.BlockSpec((tq, LANES), lambda b, p, i: (b * nq + i, p)),
        out_shape=jax.ShapeDtypeStruct((batch * seq, D_DIFF), F32),
        compiler_params=_cparams(("parallel", "parallel", "arbitrary")),
        name="diff_attention",
    )(proj, proj, proj, band, lam_vecs, jnp.tile(subln_w, 2).reshape(1, LANES))


def _t5_bucket(rel):
    nb = REL_BUCKETS // 2
    max_exact = nb // 2
    n = jnp.abs(rel)
    nf = jnp.maximum(n, 1).astype(jnp.float32)
    large = max_exact + (jnp.log(nf / max_exact) / math.log(REL_MAX_DIST / max_exact)
                         * (nb - max_exact)).astype(jnp.int32)
    large = jnp.minimum(large, nb - 1)
    return jnp.where(rel > 0, nb, 0) + jnp.where(n < max_exact, n, large)


def _bias_band(rel_bias, seq, tq):
    width = 2 * seq - tq
    rel = (jnp.arange(width, dtype=jnp.int32)[None, :] - (seq - tq)
           - jnp.arange(tq, dtype=jnp.int32)[:, None])
    return jnp.moveaxis(rel_bias.astype(F32)[_t5_bucket(rel)], -1, 0)


def _outproj_kernel(x_ref, yr_ref, yw_ref, yd_ref, w_ref, o_ref):
    acc = x_ref[...]
    acc = acc + _dot(yr_ref[...].astype(BF16), w_ref[0:D_RET, :])
    acc = acc + _dot(yw_ref[...].astype(BF16), w_ref[D_RET:D_RET + D_RWKV, :])
    acc = acc + _dot(yd_ref[...].astype(BF16), w_ref[D_RET + D_RWKV:, :])
    o_ref[...] = acc


def _outproj(x2, y_ret, y_rwkv, y_diff, w_bf16, tm=512):
    m, d = x2.shape
    tm = min(tm, m)
    row = lambda n: pl.BlockSpec((tm, n), lambda i: (i, 0))
    return pl.pallas_call(
        _outproj_kernel,
        grid=(m // tm,),
        in_specs=[row(d), row(D_RET), row(D_RWKV), row(D_DIFF),
                  pl.BlockSpec(w_bf16.shape, lambda i: (0, 0))],
        out_specs=row(d),
        out_shape=jax.ShapeDtypeStruct((m, d), F32),
        compiler_params=_cparams(("parallel",)),
        name="outproj",
    )(x2, y_ret, y_rwkv, y_diff, w_bf16)


def _ffn_up_kernel(x_ref, g_ref, wg_ref, wu_ref, o_ref, h_ref):
    @pl.when(pl.program_id(1) == 0)
    def _():
        x = x_ref[...]
        ms = jnp.mean(x * x, axis=-1, keepdims=True)
        h_ref[...] = (x * lax.rsqrt(ms + NORM_EPS) * g_ref[...]).astype(BF16)

    h = h_ref[...]
    gate = _dot(h, wg_ref[...])
    up = _dot(h, wu_ref[...])
    o_ref[...] = (gate * jax.nn.sigmoid(gate) * up).astype(BF16)


def _ffn_up(x2, g, wg_bf16, wu_bf16, tm=512, tf=1408):
    m, d = x2.shape
    tm = min(tm, m)
    f = wg_bf16.shape[1]
    return pl.pallas_call(
        _ffn_up_kernel,
        grid=(m // tm, f // tf),
        in_specs=[pl.BlockSpec((tm, d), lambda i, j: (i, 0)),
                  pl.BlockSpec((1, d), lambda i, j: (0, 0)),
                  pl.BlockSpec((d, tf), lambda i, j: (0, j)),
                  pl.BlockSpec((d, tf), lambda i, j: (0, j))],
        out_specs=pl.BlockSpec((tm, tf), lambda i, j: (i, j)),
        out_shape=jax.ShapeDtypeStruct((m, f), BF16),
        scratch_shapes=[pltpu.VMEM((tm, d), BF16)],
        compiler_params=_cparams(("parallel", "arbitrary")),
        name="ffn_up",
    )(x2, g.reshape(1, d), wg_bf16, wu_bf16)


def _ffn_down_kernel(x_ref, h_ref, w_ref, g_ref, o_ref, *, final_norm):
    y = x_ref[...] + _dot(h_ref[...], w_ref[...])
    if final_norm:
        ms = jnp.mean(y * y, axis=-1, keepdims=True)
        y = y * lax.rsqrt(ms + NORM_EPS) * g_ref[...]
    o_ref[...] = y


def _ffn_down(x2, hidden, w_bf16, g, final_norm, tm=512):
    m, d = x2.shape
    tm = min(tm, m)
    f = hidden.shape[1]
    return pl.pallas_call(
        functools.partial(_ffn_down_kernel, final_norm=final_norm),
        grid=(m // tm,),
        in_specs=[pl.BlockSpec((tm, d), lambda i: (i, 0)),
                  pl.BlockSpec((tm, f), lambda i: (i, 0)),
                  pl.BlockSpec((f, d), lambda i: (0, 0)),
                  pl.BlockSpec((1, d), lambda i: (0, 0))],
        out_specs=pl.BlockSpec((tm, d), lambda i: (i, 0)),
        out_shape=jax.ShapeDtypeStruct((m, d), F32),
        compiler_params=_cparams(("parallel",)),
        name="ffn_down",
    )(x2, hidden, w_bf16, g.reshape(1, d))


def _rope_tables(seq):
    half = HEAD_DIM // 2
    freqs = ROPE_BASE ** (-jnp.arange(half, dtype=F32) / half)
    ang = jnp.arange(seq, dtype=jnp.int32).astype(F32)[:, None] * freqs[None, :]
    cos = jnp.cos(ang)
    sin = jnp.sin(ang)
    cos_tab = jnp.tile(cos, (1, LANES // half))
    sin_tab = jnp.tile(jnp.concatenate([-sin, sin], axis=1), (1, LANES // HEAD_DIM))
    return cos_tab, sin_tab


def kernel(x, mix_norm_g, w_in, w_out, ret_gn_w, ret_gn_b, rwkv_mu, rwkv_w0, rwkv_w1, rwkv_w2, rwkv_a0, rwkv_a1, rwkv_a2, rwkv_g1, rwkv_g2, rwkv_k_k, rwkv_k_a, rwkv_r_k, rwkv_ln_w, rwkv_ln_b, diff_lambda, diff_subln_w, rel_bias, ffn_norm_g, w_gate, w_up, w_down, final_norm_g):
    batch, seq, d = x.shape
    depth = w_in.shape[0]
    cos_tab, sin_tab = _rope_tables(seq)
    band = _bias_band(rel_bias, seq, min(ATTN_TQ, seq))
    x2 = x.reshape(batch * seq, d)
    for l in range(depth):
        proj = _norm_matmul(x2, mix_norm_g[l], w_in[l].astype(BF16))
        y_ret = _retention(proj, cos_tab, sin_tab, ret_gn_w[l], ret_gn_b[l], batch, seq)
        feats = _rwkv_prep(proj, rwkv_mu[l], rwkv_w0[l], rwkv_w1[l], rwkv_w2[l], rwkv_a0[l],
                           rwkv_a1[l], rwkv_a2[l], rwkv_g1[l], rwkv_g2[l], rwkv_k_k[l],
                           rwkv_k_a[l], rwkv_r_k[l].reshape(-1), batch, seq)
        y_rwkv = _rwkv_scan(feats, rwkv_ln_w[l], rwkv_ln_b[l], batch, seq)
        lambda_init = 0.8 - 0.6 * math.exp(-0.3 * l)
        y_diff = _diff_attention(proj, band, diff_lambda[l], diff_subln_w[l], lambda_init,
                                 batch, seq)
        x2 = _outproj(x2, y_ret, y_rwkv, y_diff, w_out[l].astype(BF16))
        hidden = _ffn_up(x2, ffn_norm_g[l], w_gate[l].astype(BF16), w_up[l].astype(BF16))
        x2 = _ffn_down(x2, hidden, w_down[l].astype(BF16), final_norm_g, l == depth - 1)
    return x2.reshape(batch, seq, d)
```

```python
import functools
import math

import numpy as np
import jax
import jax.numpy as jnp
from jax import lax
from jax.experimental import pallas as pl
from jax.experimental.pallas import tpu as pltpu

F32 = jnp.float32
BF16 = jnp.bfloat16
HI = lax.Precision.HIGHEST

D_MODEL = 1024
HEAD_DIM = 64
LANES = 128
N_HEADS = D_MODEL // HEAD_DIM
RET_HEADS = (3 * N_HEADS) // 8
RWKV_HEADS = (3 * N_HEADS) // 8
DIFF_HEADS = N_HEADS - RET_HEADS - RWKV_HEADS
D_RET = RET_HEADS * HEAD_DIM
D_RWKV = RWKV_HEADS * HEAD_DIM
D_DIFF = DIFF_HEADS * HEAD_DIM
DIFF_QK_DIM = HEAD_DIM // 2
D_IN = 4 * D_RET + 4 * D_RWKV + 3 * D_DIFF
ROPE_BASE = 10000.0
D_FF = -(-8 * D_MODEL // (3 * 256)) * 256
REL_BUCKETS = 32
REL_MAX_DIST = 128
NORM_EPS = 1e-6
RET_GN_EPS = 1e-5
RWKV_GN_EPS = 64e-5

RET_CHUNK = 256
RWKV_CHUNK = 64
ATTN_TQ = 256
VMEM_LIMIT = 56 * 1024 * 1024


def _cparams(sem):
    return pltpu.CompilerParams(dimension_semantics=sem, vmem_limit_bytes=VMEM_LIMIT)


def _dot(a, b, precision=None):
    return jnp.dot(a, b, preferred_element_type=F32, precision=precision)


def _dot_nt(a, b, precision=None):
    return lax.dot_general(a, b, (((1,), (1,)), ((), ())), preferred_element_type=F32,
                           precision=precision)


def _dot_tn(a, b, precision=None):
    return lax.dot_general(a, b, (((0,), (0,)), ((), ())), preferred_element_type=F32,
                           precision=precision)


def _head_avg_matrix(n):
    r = lax.broadcasted_iota(jnp.int32, (n, n), 0) // HEAD_DIM
    c = lax.broadcasted_iota(jnp.int32, (n, n), 1) // HEAD_DIM
    return jnp.where(r == c, 1.0 / HEAD_DIM, 0.0).astype(F32)


def _group_norm(y, avg, w, b, eps):
    mu = _dot(y, avg, HI)
    yc = y - mu
    var = _dot(yc * yc, avg, HI)
    return yc * lax.rsqrt(var + eps) * w + b


def _norm_matmul_kernel(x_ref, g_ref, w_ref, o_ref):
    x = x_ref[...]
    ms = jnp.mean(x * x, axis=-1, keepdims=True)
    h = x * lax.rsqrt(ms + NORM_EPS) * g_ref[...]
    o_ref[...] = _dot(h.astype(BF16), w_ref[...])


def _norm_matmul(x2, g, w_bf16, tm=256):
    m, d = x2.shape
    n = w_bf16.shape[1]
    return pl.pallas_call(
        _norm_matmul_kernel,
        grid=(m // tm,),
        in_specs=[pl.BlockSpec((tm, d), lambda i: (i, 0)),
                  pl.BlockSpec((1, d), lambda i: (0, 0)),
                  pl.BlockSpec((d, n), lambda i: (0, 0))],
        out_specs=pl.BlockSpec((tm, n), lambda i: (i, 0)),
        out_shape=jax.ShapeDtypeStruct((m, n), F32),
        compiler_params=_cparams(("parallel",)),
        name="norm_inproj",
    )(x2, g.reshape(1, d), w_bf16)


def _retention_kernel(q_ref, k_ref, v_ref, g_ref, cos_ref, sin_ref, gnw_ref, gnb_ref, o_ref,
                      qs_ref, ks_ref, acc_ref, *, seq, chunk, log_gamma):
    nc = seq // chunk
    lane = lax.broadcasted_iota(jnp.int32, (1, LANES), 1)
    first_half = (lane & (HEAD_DIM - 1)) < (HEAD_DIM // 2)
    head_masks = (lane < HEAD_DIM, lane >= HEAD_DIM)
    ri = lax.broadcasted_iota(jnp.int32, (LANES, LANES), 0)
    ci = lax.broadcasted_iota(jnp.int32, (LANES, LANES), 1)
    block_diag = (ri // HEAD_DIM) == (ci // HEAD_DIM)
    avg = _head_avg_matrix(LANES)
    ti = lax.broadcasted_iota(jnp.int32, (chunk, chunk), 0)
    tj = lax.broadcasted_iota(jnp.int32, (chunk, chunk), 1)
    dist = jnp.abs(ti - tj).astype(F32)
    pos = lax.broadcasted_iota(jnp.int32, (chunk, 1), 0).astype(F32)

    def rope(x, cos, sin):
        partner = jnp.where(first_half, pltpu.roll(x, LANES - HEAD_DIM // 2, 1),
                            pltpu.roll(x, HEAD_DIM // 2, 1))
        return x * cos + partner * sin

    for p in range(RET_HEADS // 2):
        cols = slice(p * LANES, (p + 1) * LANES)
        lg = jnp.where(head_masks[0], log_gamma[2 * p], log_gamma[2 * p + 1]).astype(F32)
        decay_mats = [jnp.exp(dist * log_gamma[2 * p + hh]) for hh in range(2)]
        xi_f = jnp.exp((pos + 1.0) * lg)
        xi_b = jnp.exp((chunk - pos) * lg)
        zeta_f = jnp.exp((chunk - 1.0 - pos) * lg)
        zeta_b = jnp.exp(pos * lg)
        dec_c = jnp.exp(chunk * lg)

        def fwd_step(n, state):
            rows = pl.ds(pl.multiple_of(n * chunk, chunk), chunk)
            cos = cos_ref[rows, :]
            sin = sin_ref[rows, :]
            q = rope(q_ref[rows, cols], cos, sin) * (HEAD_DIM ** -0.5)
            k = rope(k_ref[rows, cols], cos, sin)
            v = v_ref[rows, cols]
            qs_ref[rows, cols] = q
            ks_ref[rows, cols] = k
            kb = k.astype(BF16)
            vb = v.astype(BF16)
            o = _dot((q * xi_f).astype(BF16), state.astype(BF16))
            for hh in range(2):
                qm = jnp.where(head_masks[hh], q, 0.0).astype(BF16)
                s = _dot_nt(qm, kb) * decay_mats[hh]
                oh = _dot(s.astype(BF16), vb)
                o = o + jnp.where(head_masks[hh], oh, 0.0)
            acc_ref[rows, cols] = o
            kv = _dot_tn((k * zeta_f).astype(BF16), vb)
            return state * dec_c + jnp.where(block_diag, kv, 0.0)

        lax.fori_loop(0, nc, fwd_step, jnp.zeros((LANES, LANES), F32))

        def bwd_step(i, state):
            n = nc - 1 - i
            rows = pl.ds(pl.multiple_of(n * chunk, chunk), chunk)
            q = qs_ref[rows, cols]
            k = ks_ref[rows, cols]
            v = v_ref[rows, cols]
            o = acc_ref[rows, cols] + _dot((q * xi_b).astype(BF16), state.astype(BF16))
            y = _group_norm(o, avg, gnw_ref[:, cols], gnb_ref[:, cols], RET_GN_EPS)
            g = g_ref[rows, cols]
            o_ref[rows, cols] = g * jax.nn.sigmoid(g) * y
            kv = _dot_tn((k * zeta_b).astype(BF16), v.astype(BF16))
            return state * dec_c + jnp.where(block_diag, kv, 0.0)

        lax.fori_loop(0, nc, bwd_step, jnp.zeros((LANES, LANES), F32))


def _retention(proj, cos_tab, sin_tab, gn_w, gn_b, batch, seq):
    chunk = min(RET_CHUNK, seq)
    log_gamma = tuple(float(np.log1p(-np.exp2(-5.0 - h))) for h in range(RET_HEADS))
    kern = functools.partial(_retention_kernel, seq=seq, chunk=chunk, log_gamma=log_gamma)
    col = lambda j: pl.BlockSpec((seq, D_RET), lambda b, j=j: (b, j))
    full = lambda shape: pl.BlockSpec(shape, lambda b: (0,) * len(shape))
    return pl.pallas_call(
        kern,
        grid=(batch,),
        in_specs=[col(0), col(1), col(2), col(3), full((seq, LANES)), full((seq, LANES)),
                  full((1, D_RET)), full((1, D_RET))],
        out_specs=pl.BlockSpec((seq, D_RET), lambda b: (b, 0)),
        out_shape=jax.ShapeDtypeStruct((batch * seq, D_RET), F32),
        scratch_shapes=[pltpu.VMEM((seq, D_RET), F32)] * 3,
        compiler_params=_cparams(("parallel",)),
        name="retention",
    )(proj, proj, proj, proj, cos_tab, sin_tab, gn_w.reshape(1, D_RET), gn_b.reshape(1, D_RET))


def _rwkv_prep_kernel(r_ref, k_ref, v_ref, z_ref, rp_ref, kp_ref, vp_ref, zp_ref,
                      rn_ref, kn_ref, vn_ref, zn_ref, mu_ref, w0_ref, w1_ref, w2_ref,
                      a0_ref, a1_ref, a2_ref, g1_ref, g2_ref, kk_ref, ka_ref, rk_ref,
                      lw0_o, lw1_o, kd0_o, kd1_o, bb0_o, bb1_o, v_o, r_o, kk_o, gate_o, bonus_o,
                      *, tb):
    i = pl.program_id(1)
    has_prev = (i > 0).astype(F32)
    has_next = (i < pl.num_programs(1) - 1).astype(F32)
    row = lax.broadcasted_iota(jnp.int32, (tb, 1), 0)
    avg = _head_avg_matrix(D_RWKV)

    def shifted(f_ref, p_ref, n_ref):
        f = f_ref[...]
        prev = jnp.where(row == 0, p_ref[7:8, :] * has_prev, pltpu.roll(f, 1, 0))
        nxt = jnp.where(row == tb - 1, n_ref[0:1, :] * has_next, pltpu.roll(f, tb - 1, 0))
        return f, prev - f, nxt - f

    def mix(parts, idx):
        f, dp, dn = parts
        return f + mu_ref[idx, 0:1, :] * dp + mu_ref[idx, 1:2, :] * dn

    xr = mix(shifted(r_ref, rp_ref, rn_ref), 0)
    xk = mix(shifted(k_ref, kp_ref, kn_ref), 1)
    xv = mix(shifted(v_ref, vp_ref, vn_ref), 2)
    zparts = shifted(z_ref, zp_ref, zn_ref)
    xw = mix(zparts, 3)
    xa = mix(zparts, 4)
    xg = mix(zparts, 5)

    gate = _dot(jax.nn.sigmoid(_dot(xg, g1_ref[...], HI)), g2_ref[...], HI)
    kk = xk * kk_ref[...]
    norm = jnp.sqrt(_dot(kk * kk, avg, HI) * HEAD_DIM)
    kk = kk / jnp.maximum(norm, 1e-12)

    ksum = jnp.zeros_like(xk)
    outs = ((lw0_o, kd0_o, bb0_o), (lw1_o, kd1_o, bb1_o))
    for d in range(2):
        w_lo = _dot(jnp.tanh(_dot(xw, w1_ref[d], HI)), w2_ref[d], HI)
        w_raw = w0_ref[d:d + 1, :] + w_lo
        log_w = -math.exp(-0.5) * jax.nn.sigmoid(w_raw)
        a = jax.nn.sigmoid(a0_ref[d:d + 1, :] + _dot(_dot(xa, a1_ref[d], HI), a2_ref[d], HI))
        k_dir = xk * (1.0 + (a - 1.0) * ka_ref[...])
        ksum = ksum + k_dir
        lw_o, kd_o, bb_o = outs[d]
        lw_o[...] = log_w
        kd_o[...] = k_dir
        bb_o[...] = kk * a
    bonus = _dot(xr * ksum * rk_ref[...], avg, HI) * HEAD_DIM
    v_o[...] = xv
    r_o[...] = xr
    kk_o[...] = kk
    gate_o[...] = gate
    bonus_o[...] = bonus * xv


def _rwkv_prep(proj, mu, w0, w1, w2, a0, a1, a2, g1, g2, k_k, k_a, r_k, batch, seq):
    tb = min(256, seq)
    nt = seq // tb
    d = D_RWKV
    col0 = (4 * D_RET) // d
    main = lambda j: pl.BlockSpec((tb, d), lambda b, i, j=j: (b * nt + i, col0 + j))
    prev = lambda j: pl.BlockSpec(
        (8, d), lambda b, i, j=j: (b * (seq // 8) + jnp.maximum(i * (tb // 8) - 1, 0), col0 + j))
    nxt = lambda j: pl.BlockSpec(
        (8, d), lambda b, i, j=j: (b * (seq // 8) + jnp.minimum((i + 1) * (tb // 8), seq // 8 - 1),
                                   col0 + j))
    full = lambda shape: pl.BlockSpec(shape, lambda b, i: (0,) * len(shape))
    out_spec = pl.BlockSpec((tb, d), lambda b, i: (b * nt + i, 0))
    out_sd = jax.ShapeDtypeStruct((batch * seq, d), F32)
    row = lambda a: a.reshape(1, d)
    return pl.pallas_call(
        functools.partial(_rwkv_prep_kernel, tb=tb),
        grid=(batch, nt),
        in_specs=[main(0), main(1), main(2), main(3), prev(0), prev(1), prev(2), prev(3),
                  nxt(0), nxt(1), nxt(2), nxt(3),
                  full(mu.shape), full(w0.shape), full(w1.shape), full(w2.shape),
                  full(a0.shape), full(a1.shape), full(a2.shape), full(g1.shape), full(g2.shape),
                  full((1, d)), full((1, d)), full((1, d))],
        out_specs=[out_spec] * 11,
        out_shape=[out_sd] * 11,
        compiler_params=_cparams(("parallel", "parallel")),
        name="rwkv_features",
    )(*([proj] * 12), mu, w0, w1, w2, a0, a1, a2, g1, g2, row(k_k), row(k_a), row(r_k))


def _split3(x):
    hi = x.astype(BF16)
    rest = x - hi.astype(F32)
    mid = rest.astype(BF16)
    lo = (rest - mid.astype(F32)).astype(BF16)
    return hi, mid, lo


def _bdot(a, b):
    return _dot(a.astype(BF16), b.astype(BF16))


def _rwkv_scan_kernel(lw0_ref, lw1_ref, kd0_ref, kd1_ref, bb0_ref, bb1_ref, v_ref, r_ref, kk_ref,
                      gate_ref, bonus_ref, lnw_ref, lnb_ref, o_ref, acc0_ref, acc1_ref, *, seq, chunk):
    nc = seq // chunk
    c2 = 2 * chunk
    lane = lax.broadcasted_iota(jnp.int32, (1, LANES), 1)
    head_masks = (lane < HEAD_DIM, lane >= HEAD_DIM)
    avg = _head_avg_matrix(LANES)
    ri = lax.broadcasted_iota(jnp.int32, (c2, c2), 0)
    ci = lax.broadcasted_iota(jnp.int32, (c2, c2), 1)
    same_head = (ri // chunk) == (ci // chunk)
    rt = ri % chunk
    ct = ci % chunk
    eye = (ri == ci).astype(F32)
    ti = lax.broadcasted_iota(jnp.int32, (chunk, chunk), 0)
    tj = lax.broadcasted_iota(jnp.int32, (chunk, chunk), 1)
    strict = (same_head & (ct < rt), same_head & (ct > rt))
    incl = (same_head & (ct <= rt), same_head & (ct >= rt))
    cum_mat = ((tj <= ti).astype(BF16), (tj >= ti).astype(BF16))
    dirs = ((lw0_ref, kd0_ref, bb0_ref), (lw1_ref, kd1_ref, bb1_ref))

    def stack(x):
        return jnp.concatenate([jnp.where(head_masks[0], x, 0.0), jnp.where(head_masks[1], x, 0.0)],
                               axis=0).astype(BF16)

    def chunk_step(d, rows, state):
        lw_ref, kd_ref, bb_ref = dirs[d]
        lw = lw_ref[rows, :]
        cum_in = sum(_dot(cum_mat[d], part) for part in _split3(lw))
        cum_ex = cum_in - lw
        total = jnp.sum(lw, axis=0, keepdims=True)
        kk = kk_ref[rows, :]
        kd = kd_ref[rows, :]
        bb = bb_ref[rows, :]
        inv_p = jnp.exp(-cum_in)
        to_end = jnp.exp(total - cum_in)
        a_s = stack(-kk * jnp.exp(cum_ex))
        r_s = stack(r_ref[rows, :] * jnp.exp(cum_in))
        b_s = stack(bb * inv_p)
        k_s = stack(kd * inv_p)
        v_s = stack(v_ref[rows, :])
        bend_s = stack(bb * to_end)
        kend_s = stack(kd * to_end)

        ar = jnp.concatenate([a_s, r_s], axis=0)
        bk = jnp.concatenate([b_s, k_s], axis=0)
        gram = _dot_nt(ar, bk)
        l_ab = jnp.where(strict[d], gram[:c2, :c2], 0.0)
        l_ak = jnp.where(strict[d], gram[:c2, c2:], 0.0)
        m_rb = jnp.where(incl[d], gram[c2:, :c2], 0.0)
        m_rk = jnp.where(incl[d], gram[c2:, c2:], 0.0)

        power = l_ab
        inv = eye + l_ab
        for _ in range(int(math.log2(chunk)) - 1):
            power = _bdot(power, power)
            inv = inv + _bdot(inv, power)

        ar_state = _dot_nt(ar, state.astype(BF16))
        rhs = ar_state[:c2] + _bdot(l_ak, v_s)
        u = _bdot(inv, rhs)
        uv = jnp.concatenate([u.astype(BF16), v_s], axis=0)
        y2 = ar_state[c2:] + _bdot(jnp.concatenate([m_rb, m_rk], axis=1), uv)
        y = y2[:chunk] + y2[chunk:]
        new_state = state * jnp.exp(total) + _dot_tn(uv, jnp.concatenate([bend_s, kend_s], axis=0))
        return y, new_state

    def step(i, states):
        rows0 = pl.ds(pl.multiple_of(i * chunk, chunk), chunk)
        rows1 = pl.ds(pl.multiple_of((nc - 1 - i) * chunk, chunk), chunk)
        y0, s0 = chunk_step(0, rows0, states[0])
        y1, s1 = chunk_step(1, rows1, states[1])
        acc0_ref[rows0, :] = y0
        acc1_ref[rows1, :] = y1
        return s0, s1

    zero = jnp.zeros((LANES, LANES), F32)
    lax.fori_loop(0, nc, step, (zero, zero))

    def finish(n, carry):
        rows = pl.ds(pl.multiple_of(n * chunk, chunk), chunk)
        y = _group_norm(acc0_ref[rows, :] + acc1_ref[rows, :], avg, lnw_ref[...], lnb_ref[...],
                        RWKV_GN_EPS)
        o_ref[rows, :] = (y + bonus_ref[rows, :]) * gate_ref[rows, :]
        return carry

    lax.fori_loop(0, nc, finish, 0)


def _rwkv_scan(feats, ln_w, ln_b, batch, seq):
    chunk = min(RWKV_CHUNK, seq)
    npair = RWKV_HEADS // 2
    blk = pl.BlockSpec((seq, LANES), lambda b, p: (b, p))
    vec = pl.BlockSpec((1, LANES), lambda b, p: (0, p))
    return pl.pallas_call(
        functools.partial(_rwkv_scan_kernel, seq=seq, chunk=chunk),
        grid=(batch, npair),
        in_specs=[blk] * 11 + [vec, vec],
        out_specs=blk,
        out_shape=jax.ShapeDtypeStruct((batch * seq, D_RWKV), F32),
        scratch_shapes=[pltpu.VMEM((seq, LANES), F32)] * 2,
        compiler_params=_cparams(("parallel", "parallel")),
        name="rwkv_scan",
    )(*feats, ln_w.reshape(1, D_RWKV), ln_b.reshape(1, D_RWKV))


def _diff_attn_kernel(q_ref, k_ref, v_ref, band_ref, lam_ref, subw_ref, o_ref, *, seq, tq,
                      lambda_init):
    i = pl.program_id(2)
    lane = lax.broadcasted_iota(jnp.int32, (1, LANES), 1)
    lv = lam_ref[...]
    lam = (jnp.exp(jnp.sum(lv[0:1] * lv[1:2], axis=1, keepdims=True))
           - jnp.exp(jnp.sum(lv[2:3] * lv[3:4], axis=1, keepdims=True)) + lambda_init)
    q = q_ref[...] * (DIFF_QK_DIM ** -0.5)
    kb = k_ref[...].astype(BF16)
    vb = v_ref[...].astype(BF16)
    off = pl.multiple_of((seq - tq) - i * tq, LANES)
    out = jnp.zeros((tq, LANES), F32)
    for hh in range(2):
        bias = band_ref[hh, :, pl.ds(off, seq)]
        parts = []
        for c in range(2):
            lo = hh * HEAD_DIM + c * DIFF_QK_DIM
            qm = jnp.where((lane >= lo) & (lane < lo + DIFF_QK_DIM), q, 0.0).astype(BF16)
            s = _dot_nt(qm, kb) + bias
            e = jnp.exp(s - jnp.max(s, axis=-1, keepdims=True))
            denom = jnp.sum(e, axis=-1, keepdims=True)
            parts.append(_dot(e.astype(BF16), vb) / denom)
        head = parts[0] - lam * parts[1]
        in_head = (lane >= hh * HEAD_DIM) & (lane < (hh + 1) * HEAD_DIM)
        head = jnp.where(in_head, head, 0.0)
        ms = jnp.sum(head * head, axis=-1, keepdims=True) * (1.0 / HEAD_DIM)
        out = out + head * lax.rsqrt(ms + NORM_EPS)
    o_ref[...] = out * subw_ref[...] * (1.0 - lambda_init)


def _diff_attention(proj, band, lam_vecs, subln_w, lambda_init, batch, seq):
    tq = min(ATTN_TQ, seq)
    nq = seq // tq
    npair = DIFF_HEADS // 2
    col0 = (4 * D_RET + 4 * D_RWKV) // LANES
    width = band.shape[-1]
    return pl.pallas_call(
        functools.partial(_diff_attn_kernel, seq=seq, tq=tq, lambda_init=lambda_init),
        grid=(batch, npair, nq),
        in_specs=[pl.BlockSpec((tq, LANES), lambda b, p, i: (b * nq + i, col0 + p)),
                  pl.BlockSpec((seq, LANES), lambda b, p, i: (b, col0 + npair + p)),
                  pl.BlockSpec((seq, LANES), lambda b, p, i: (b, col0 + 2 * npair + p)),
                  pl.BlockSpec((2, tq, width), lambda b, p, i: (p, 0, 0)),
                  pl.BlockSpec((4, DIFF_QK_DIM), lambda b, p, i: (0, 0)),
                  pl.BlockSpec((1, LANES), lambda b, p, i: (0, 0))],
        out_specs=pl.BlockSpec((tq, LANES), lambda b, p, i: (b * nq + i, p)),
        out_shape=jax.ShapeDtypeStruct((batch * seq, D_DIFF), F32),
        compiler_params=_cparams(("parallel", "parallel", "arbitrary")),
        name="diff_attention",
    )(proj, proj, proj, band, lam_vecs, jnp.tile(subln_w, 2).reshape(1, LANES))


def _t5_bucket(rel):
    nb = REL_BUCKETS // 2
    max_exact = nb // 2
    n = jnp.abs(rel)
    nf = jnp.maximum(n, 1).astype(jnp.float32)
    large = max_exact + (jnp.log(nf / max_exact) / math.log(REL_MAX_DIST / max_exact)
                         * (nb - max_exact)).astype(jnp.int32)
    large = jnp.minimum(large, nb - 1)
    return jnp.where(rel > 0, nb, 0) + jnp.where(n < max_exact, n, large)


def _bias_band(rel_bias, seq, tq):
    width = 2 * seq - tq
    period = 2 * seq - 1
    m = jnp.arange(period, dtype=jnp.int32)
    rel = jnp.where(m < width, m, m - period) - (seq - tq)
    vec = rel_bias.astype(F32)[_t5_bucket(rel)].T
    rows = jnp.tile(vec, (1, tq))[:, :tq * (period - 1)].reshape(-1, tq, period - 1)
    return rows[:, :, :width]


def _outproj_kernel(x_ref, yr_ref, yw_ref, yd_ref, w_ref, o_ref):
    acc = x_ref[...]
    acc = acc + _dot(yr_ref[...].astype(BF16), w_ref[0:D_RET, :])
    acc = acc + _dot(yw_ref[...].astype(BF16), w_ref[D_RET:D_RET + D_RWKV, :])
    acc = acc + _dot(yd_ref[...].astype(BF16), w_ref[D_RET + D_RWKV:, :])
    o_ref[...] = acc


def _outproj(x2, y_ret, y_rwkv, y_diff, w_bf16, tm=512):
    m, d = x2.shape
    tm = min(tm, m)
    row = lambda n: pl.BlockSpec((tm, n), lambda i: (i, 0))
    return pl.pallas_call(
        _outproj_kernel,
        grid=(m // tm,),
        in_specs=[row(d), row(D_RET), row(D_RWKV), row(D_DIFF),
                  pl.BlockSpec(w_bf16.shape, lambda i: (0, 0))],
        out_specs=row(d),
        out_shape=jax.ShapeDtypeStruct((m, d), F32),
        compiler_params=_cparams(("parallel",)),
        name="outproj",
    )(x2, y_ret, y_rwkv, y_diff, w_bf16)


def _ffn_up_kernel(x_ref, g_ref, wg_ref, wu_ref, o_ref, h_ref):
    @pl.when(pl.program_id(1) == 0)
    def _():
        x = x_ref[...]
        ms = jnp.mean(x * x, axis=-1, keepdims=True)
        h_ref[...] = (x * lax.rsqrt(ms + NORM_EPS) * g_ref[...]).astype(BF16)

    h = h_ref[...]
    gate = _dot(h, wg_ref[...])
    up = _dot(h, wu_ref[...])
    o_ref[...] = (gate * jax.nn.sigmoid(gate) * up).astype(BF16)


def _ffn_up(x2, g, wg_bf16, wu_bf16, tm=512, tf=1408):
    m, d = x2.shape
    tm = min(tm, m)
    f = wg_bf16.shape[1]
    return pl.pallas_call(
        _ffn_up_kernel,
        grid=(m // tm, f // tf),
        in_specs=[pl.BlockSpec((tm, d), lambda i, j: (i, 0)),
                  pl.BlockSpec((1, d), lambda i, j: (0, 0)),
                  pl.BlockSpec((d, tf), lambda i, j: (0, j)),
                  pl.BlockSpec((d, tf), lambda i, j: (0, j))],
        out_specs=pl.BlockSpec((tm, tf), lambda i, j: (i, j)),
        out_shape=jax.ShapeDtypeStruct((m, f), BF16),
        scratch_shapes=[pltpu.VMEM((tm, d), BF16)],
        compiler_params=_cparams(("parallel", "arbitrary")),
        name="ffn_up",
    )(x2, g.reshape(1, d), wg_bf16, wu_bf16)


def _ffn_down_kernel(x_ref, h_ref, w_ref, g_ref, o_ref, *, final_norm):
    y = x_ref[...] + _dot(h_ref[...], w_ref[...])
    if final_norm:
        ms = jnp.mean(y * y, axis=-1, keepdims=True)
        y = y * lax.rsqrt(ms + NORM_EPS) * g_ref[...]
    o_ref[...] = y


def _ffn_down(x2, hidden, w_bf16, g, final_norm, tm=512):
    m, d = x2.shape
    tm = min(tm, m)
    f = hidden.shape[1]
    return pl.pallas_call(
        functools.partial(_ffn_down_kernel, final_norm=final_norm),
        grid=(m // tm,),
        in_specs=[pl.BlockSpec((tm, d), lambda i: (i, 0)),
                  pl.BlockSpec((tm, f), lambda i: (i, 0)),
                  pl.BlockSpec((f, d), lambda i: (0, 0)),
                  pl.BlockSpec((1, d), lambda i: (0, 0))],
        out_specs=pl.BlockSpec((tm, d), lambda i: (i, 0)),
        out_shape=jax.ShapeDtypeStruct((m, d), F32),
        compiler_params=_cparams(("parallel",)),
        name="ffn_down",
    )(x2, hidden, w_bf16, g.reshape(1, d))


def _rope_tables(seq):
    half = HEAD_DIM // 2
    freqs = ROPE_BASE ** (-jnp.arange(half, dtype=F32) / half)
    ang = jnp.arange(seq, dtype=jnp.int32).astype(F32)[:, None] * freqs[None, :]
    cos = jnp.cos(ang)
    sin = jnp.sin(ang)
    cos_tab = jnp.tile(cos, (1, LANES // half))
    sin_tab = jnp.tile(jnp.concatenate([-sin, sin], axis=1), (1, LANES // HEAD_DIM))
    return cos_tab, sin_tab


def kernel(x, mix_norm_g, w_in, w_out, ret_gn_w, ret_gn_b, rwkv_mu, rwkv_w0, rwkv_w1, rwkv_w2, rwkv_a0, rwkv_a1, rwkv_a2, rwkv_g1, rwkv_g2, rwkv_k_k, rwkv_k_a, rwkv_r_k, rwkv_ln_w, rwkv_ln_b, diff_lambda, diff_subln_w, rel_bias, ffn_norm_g, w_gate, w_up, w_down, final_norm_g):
    batch, seq, d = x.shape
    depth = w_in.shape[0]
    cos_tab, sin_tab = _rope_tables(seq)
    band = _bias_band(rel_bias, seq, min(ATTN_TQ, seq))
    x2 = x.reshape(batch * seq, d)
    for l in range(depth):
        proj = _norm_matmul(x2, mix_norm_g[l], w_in[l].astype(BF16))
        y_ret = _retention(proj, cos_tab, sin_tab, ret_gn_w[l], ret_gn_b[l], batch, seq)
        feats = _rwkv_prep(proj, rwkv_mu[l], rwkv_w0[l], rwkv_w1[l], rwkv_w2[l], rwkv_a0[l],
                           rwkv_a1[l], rwkv_a2[l], rwkv_g1[l], rwkv_g2[l], rwkv_k_k[l],
                           rwkv_k_a[l], rwkv_r_k[l].reshape(-1), batch, seq)
        y_rwkv = _rwkv_scan(feats, rwkv_ln_w[l], rwkv_ln_b[l], batch, seq)
        lambda_init = 0.8 - 0.6 * math.exp(-0.3 * l)
        y_diff = _diff_attention(proj, band, diff_lambda[l], diff_subln_w[l], lambda_init,
                                 batch, seq)
        x2 = _outproj(x2, y_ret, y_rwkv, y_diff, w_out[l].astype(BF16))
        hidden = _ffn_up(x2, ffn_norm_g[l], w_gate[l].astype(BF16), w_up[l].astype(BF16))
        x2 = _ffn_down(x2, hidden, w_down[l].astype(BF16), final_norm_g, l == depth - 1)
    return x2.reshape(batch, seq, d)
```

```python
import functools
import math

import numpy as np
import jax
import jax.numpy as jnp
from jax import lax
from jax.experimental import pallas as pl
from jax.experimental.pallas import tpu as pltpu

F32 = jnp.float32
BF16 = jnp.bfloat16

D_MODEL = 1024
HEAD_DIM = 64
LANES = 128
N_HEADS = D_MODEL // HEAD_DIM
RET_HEADS = (3 * N_HEADS) // 8
RWKV_HEADS = (3 * N_HEADS) // 8
DIFF_HEADS = N_HEADS - RET_HEADS - RWKV_HEADS
D_RET = RET_HEADS * HEAD_DIM
D_RWKV = RWKV_HEADS * HEAD_DIM
D_DIFF = DIFF_HEADS * HEAD_DIM
DIFF_QK_DIM = HEAD_DIM // 2
D_IN = 4 * D_RET + 4 * D_RWKV + 3 * D_DIFF
ROPE_BASE = 10000.0
D_FF = -(-8 * D_MODEL // (3 * 256)) * 256
REL_BUCKETS = 32
REL_MAX_DIST = 128
NORM_EPS = 1e-6
RET_GN_EPS = 1e-5
RWKV_GN_EPS = 64e-5

RET_CHUNK = 256
RWKV_CHUNK = 64
RWKV_GROUP = 4
ATTN_TQ = 256
VMEM_LIMIT = 56 * 1024 * 1024


def _cparams(sem):
    return pltpu.CompilerParams(dimension_semantics=sem, vmem_limit_bytes=VMEM_LIMIT)


def _dot(a, b):
    return jnp.dot(a, b, preferred_element_type=F32)


def _bdot(a, b):
    return _dot(a.astype(BF16), b.astype(BF16))


def _dot_nt(a, b):
    return lax.dot_general(a, b, (((1,), (1,)), ((), ())), preferred_element_type=F32)


def _dot_tn(a, b):
    return lax.dot_general(a, b, (((0,), (0,)), ((), ())), preferred_element_type=F32)


def _head_avg_matrix():
    r = lax.broadcasted_iota(jnp.int32, (LANES, LANES), 0) // HEAD_DIM
    c = lax.broadcasted_iota(jnp.int32, (LANES, LANES), 1) // HEAD_DIM
    return jnp.where(r == c, 1.0 / HEAD_DIM, 0.0).astype(BF16)


def _head_mean(x, avg):
    hi = x.astype(BF16)
    lo = (x - hi.astype(F32)).astype(BF16)
    return _dot(hi, avg) + _dot(lo, avg)


def _group_norm(y, avg, w, b, eps):
    yc = y - _head_mean(y, avg)
    var = _head_mean(yc * yc, avg)
    return yc * lax.rsqrt(var + eps) * w + b


def _norm_matmul_kernel(x_ref, g_ref, w_ref, o_ref):
    x = x_ref[...]
    ms = jnp.mean(x * x, axis=-1, keepdims=True)
    h = x * lax.rsqrt(ms + NORM_EPS) * g_ref[...]
    o_ref[...] = _dot(h.astype(BF16), w_ref[...])


def _norm_matmul(x2, g, w_bf16, tm=256):
    m, d = x2.shape
    n = w_bf16.shape[1]
    return pl.pallas_call(
        _norm_matmul_kernel,
        grid=(m // tm,),
        in_specs=[pl.BlockSpec((tm, d), lambda i: (i, 0)),
                  pl.BlockSpec((1, d), lambda i: (0, 0)),
                  pl.BlockSpec((d, n), lambda i: (0, 0))],
        out_specs=pl.BlockSpec((tm, n), lambda i: (i, 0)),
        out_shape=jax.ShapeDtypeStruct((m, n), F32),
        compiler_params=_cparams(("parallel",)),
        name="norm_inproj",
    )(x2, g.reshape(1, d), w_bf16)


def _retention_kernel(q_ref, k_ref, v_ref, g_ref, cos_ref, sin_ref, gnw_ref, gnb_ref, o_ref,
                      qs_ref, ks_ref, acc_ref, *, seq, chunk, log_gamma):
    nc = seq // chunk
    lane = lax.broadcasted_iota(jnp.int32, (1, LANES), 1)
    first_half = (lane & (HEAD_DIM - 1)) < (HEAD_DIM // 2)
    head_masks = (lane < HEAD_DIM, lane >= HEAD_DIM)
    ri = lax.broadcasted_iota(jnp.int32, (LANES, LANES), 0)
    ci = lax.broadcasted_iota(jnp.int32, (LANES, LANES), 1)
    block_diag = (ri // HEAD_DIM) == (ci // HEAD_DIM)
    avg = _head_avg_matrix()
    ti = lax.broadcasted_iota(jnp.int32, (chunk, chunk), 0)
    tj = lax.broadcasted_iota(jnp.int32, (chunk, chunk), 1)
    dist = jnp.abs(ti - tj).astype(F32)
    pos = lax.broadcasted_iota(jnp.int32, (chunk, 1), 0).astype(F32)

    def rope(x, cos, sin):
        partner = jnp.where(first_half, pltpu.roll(x, LANES - HEAD_DIM // 2, 1),
                            pltpu.roll(x, HEAD_DIM // 2, 1))
        return x * cos + partner * sin

    for p in range(RET_HEADS // 2):
        cols = slice(p * LANES, (p + 1) * LANES)
        lg = jnp.where(head_masks[0], log_gamma[2 * p], log_gamma[2 * p + 1]).astype(F32)
        decay_mats = [jnp.exp(dist * log_gamma[2 * p + hh]) for hh in range(2)]
        xi_f = jnp.exp((pos + 1.0) * lg)
        xi_b = jnp.exp((chunk - pos) * lg)
        zeta_f = jnp.exp((chunk - 1.0 - pos) * lg)
        zeta_b = jnp.exp(pos * lg)
        dec_c = jnp.exp(chunk * lg)

        def fwd_step(n, state):
            rows = pl.ds(pl.multiple_of(n * chunk, chunk), chunk)
            cos = cos_ref[rows, :]
            sin = sin_ref[rows, :]
            q = rope(q_ref[rows, cols], cos, sin) * (HEAD_DIM ** -0.5)
            k = rope(k_ref[rows, cols], cos, sin)
            v = v_ref[rows, cols]
            qs_ref[rows, cols] = q
            ks_ref[rows, cols] = k
            kb = k.astype(BF16)
            vb = v.astype(BF16)
            o = _dot((q * xi_f).astype(BF16), state.astype(BF16))
            for hh in range(2):
                qm = jnp.where(head_masks[hh], q, 0.0).astype(BF16)
                s = _dot_nt(qm, kb) * decay_mats[hh]
                oh = _dot(s.astype(BF16), vb)
                o = o + jnp.where(head_masks[hh], oh, 0.0)
            acc_ref[rows, cols] = o
            kv = _dot_tn((k * zeta_f).astype(BF16), vb)
            return state * dec_c + jnp.where(block_diag, kv, 0.0)

        lax.fori_loop(0, nc, fwd_step, jnp.zeros((LANES, LANES), F32))

        def bwd_step(i, state):
            n = nc - 1 - i
            rows = pl.ds(pl.multiple_of(n * chunk, chunk), chunk)
            q = qs_ref[rows, cols]
            k = ks_ref[rows, cols]
            v = v_ref[rows, cols]
            o = acc_ref[rows, cols] + _dot((q * xi_b).astype(BF16), state.astype(BF16))
            y = _group_norm(o, avg, gnw_ref[:, cols], gnb_ref[:, cols], RET_GN_EPS)
            g = g_ref[rows, cols]
            o_ref[rows, cols] = g * jax.nn.sigmoid(g) * y
            kv = _dot_tn((k * zeta_b).astype(BF16), v.astype(BF16))
            return state * dec_c + jnp.where(block_diag, kv, 0.0)

        lax.fori_loop(0, nc, bwd_step, jnp.zeros((LANES, LANES), F32))


def _retention(proj, cos_tab, sin_tab, gn_w, gn_b, batch, seq):
    chunk = min(RET_CHUNK, seq)
    log_gamma = tuple(float(np.log1p(-np.exp2(-5.0 - h))) for h in range(RET_HEADS))
    kern = functools.partial(_retention_kernel, seq=seq, chunk=chunk, log_gamma=log_gamma)
    col = lambda j: pl.BlockSpec((seq, D_RET), lambda b, j=j: (b, j))
    full = lambda shape: pl.BlockSpec(shape, lambda b: (0,) * len(shape))
    return pl.pallas_call(
        kern,
        grid=(batch,),
        in_specs=[col(0), col(1), col(2), col(3), full((seq, LANES)), full((seq, LANES)),
                  full((1, D_RET)), full((1, D_RET))],
        out_specs=pl.BlockSpec((seq, D_RET), lambda b: (b, 0)),
        out_shape=jax.ShapeDtypeStruct((batch * seq, D_RET), F32),
        scratch_shapes=[pltpu.VMEM((seq, D_RET), F32)] * 3,
        compiler_params=_cparams(("parallel",)),
        name="retention",
    )(proj, proj, proj, proj, cos_tab, sin_tab, gn_w.reshape(1, D_RET), gn_b.reshape(1, D_RET))


def _rwkv_prep_kernel(r_ref, k_ref, v_ref, z_ref, rp_ref, kp_ref, vp_ref, zp_ref,
                      rn_ref, kn_ref, vn_ref, zn_ref, mu_ref, w0_ref, w1_ref, w2_ref,
                      a0_ref, a1_ref, a2_ref, g1_ref, g2_ref, kk_ref, ka_ref, rk_ref,
                      lw0_o, lw1_o, kd0_o, kd1_o, bb0_o, bb1_o, v_o, r_o, kk_o, gate_o, bonus_o,
                      *, tb):
    i = pl.program_id(1)
    has_prev = (i > 0).astype(F32)
    has_next = (i < pl.num_programs(1) - 1).astype(F32)
    row = lax.broadcasted_iota(jnp.int32, (tb, 1), 0)
    avg = _head_avg_matrix()

    def head_sum(x):
        return jnp.concatenate([_head_mean(x[:, c * LANES:(c + 1) * LANES], avg)
                                for c in range(D_RWKV // LANES)], axis=1) * HEAD_DIM

    def shifted(f_ref, p_ref, n_ref):
        f = f_ref[...]
        prev = jnp.where(row == 0, p_ref[7:8, :] * has_prev, pltpu.roll(f, 1, 0))
        nxt = jnp.where(row == tb - 1, n_ref[0:1, :] * has_next, pltpu.roll(f, tb - 1, 0))
        return f, prev - f, nxt - f

    def mix(parts, idx):
        f, dp, dn = parts
        return f + mu_ref[idx, 0:1, :] * dp + mu_ref[idx, 1:2, :] * dn

    xr = mix(shifted(r_ref, rp_ref, rn_ref), 0)
    xk = mix(shifted(k_ref, kp_ref, kn_ref), 1)
    xv = mix(shifted(v_ref, vp_ref, vn_ref), 2)
    zparts = shifted(z_ref, zp_ref, zn_ref)
    xw = mix(zparts, 3)
    xa = mix(zparts, 4)
    xg = mix(zparts, 5)

    gate = _bdot(jax.nn.sigmoid(_bdot(xg, g1_ref[...])), g2_ref[...])
    w_lo = _bdot(jnp.tanh(_bdot(xw, w1_ref[...])), w2_ref[...])
    a_lo = _bdot(_bdot(xa, a1_ref[...]), a2_ref[...])
    kk = xk * kk_ref[...]
    kk = kk / jnp.maximum(jnp.sqrt(head_sum(kk * kk)), 1e-12)

    ksum = jnp.zeros_like(xk)
    outs = ((lw0_o, kd0_o, bb0_o), (lw1_o, kd1_o, bb1_o))
    for d in range(2):
        cols = slice(d * D_RWKV, (d + 1) * D_RWKV)
        w_raw = w0_ref[d:d + 1, :] + w_lo[:, cols]
        log_w = -math.exp(-0.5) * jax.nn.sigmoid(w_raw)
        a = jax.nn.sigmoid(a0_ref[d:d + 1, :] + a_lo[:, cols])
        k_dir = xk * (1.0 + (a - 1.0) * ka_ref[...])
        ksum = ksum + k_dir
        lw_o, kd_o, bb_o = outs[d]
        lw_o[...] = log_w
        kd_o[...] = k_dir
        bb_o[...] = kk * a
    bonus = head_sum(xr * ksum * rk_ref[...])
    v_o[...] = xv
    r_o[...] = xr
    kk_o[...] = kk
    gate_o[...] = gate
    bonus_o[...] = bonus * xv


def _rwkv_prep(proj, mu, w0, w1, w2, a0, a1, a2, g1, g2, k_k, k_a, r_k, batch, seq):
    tb = min(256, seq)
    nt = seq // tb
    d = D_RWKV
    col0 = (4 * D_RET) // d
    main = lambda j: pl.BlockSpec((tb, d), lambda b, i, j=j: (b * nt + i, col0 + j))
    prev = lambda j: pl.BlockSpec(
        (8, d), lambda b, i, j=j: (b * (seq // 8) + jnp.maximum(i * (tb // 8) - 1, 0), col0 + j))
    nxt = lambda j: pl.BlockSpec(
        (8, d), lambda b, i, j=j: (b * (seq // 8) + jnp.minimum((i + 1) * (tb // 8), seq // 8 - 1),
                                   col0 + j))
    full = lambda shape: pl.BlockSpec(shape, lambda b, i: (0,) * len(shape))
    out_spec = pl.BlockSpec((tb, d), lambda b, i: (b * nt + i, 0))
    out_sd = jax.ShapeDtypeStruct((batch * seq, d), F32)
    row = lambda a: a.reshape(1, d)

    def both_dirs(first, second):
        r = first.shape[-1]
        blk = jnp.zeros((2 * r, 2 * d), F32)
        blk = blk.at[:r, :d].set(second[0]).at[r:, d:].set(second[1])
        return jnp.concatenate([first[0], first[1]], axis=1).astype(BF16), blk.astype(BF16)

    w1, w2 = both_dirs(w1, w2)
    a1, a2 = both_dirs(a1, a2)
    g1, g2 = g1.astype(BF16), g2.astype(BF16)
    return pl.pallas_call(
        functools.partial(_rwkv_prep_kernel, tb=tb),
        grid=(batch, nt),
        in_specs=[main(0), main(1), main(2), main(3), prev(0), prev(1), prev(2), prev(3),
                  nxt(0), nxt(1), nxt(2), nxt(3),
                  full(mu.shape), full(w0.shape), full(w1.shape), full(w2.shape),
                  full(a0.shape), full(a1.shape), full(a2.shape), full(g1.shape), full(g2.shape),
                  full((1, d)), full((1, d)), full((1, d))],
        out_specs=[out_spec] * 11,
        out_shape=[out_sd] * 11,
        compiler_params=_cparams(("parallel", "parallel")),
        name="rwkv_features",
    )(*([proj] * 12), mu, w0, w1, w2, a0, a1, a2, g1, g2, row(k_k), row(k_a), row(r_k))


def _split3(x):
    hi = x.astype(BF16)
    rest = x - hi.astype(F32)
    mid = rest.astype(BF16)
    lo = (rest - mid.astype(F32)).astype(BF16)
    return hi, mid, lo


def _rwkv_scan_kernel(lw0_ref, kd0_ref, bb0_ref, v0_ref, r0_ref, kk0_ref,
                      lw1_ref, kd1_ref, bb1_ref, v1_ref, r1_ref, kk1_ref,
                      y0_ref, y1_ref, s_ref, *, chunk, group):
    npair = RWKV_HEADS // 2
    c2 = 2 * chunk

    @pl.when(pl.program_id(1) == 0)
    def _():
        s_ref[...] = jnp.zeros_like(s_ref)

    lane = lax.broadcasted_iota(jnp.int32, (1, LANES), 1)
    head_masks = (lane < HEAD_DIM, lane >= HEAD_DIM)
    ri = lax.broadcasted_iota(jnp.int32, (c2, c2), 0)
    ci = lax.broadcasted_iota(jnp.int32, (c2, c2), 1)
    same_head = (ri // chunk) == (ci // chunk)
    rt = ri % chunk
    ct = ci % chunk
    eye = (ri == ci).astype(F32)
    ti = lax.broadcasted_iota(jnp.int32, (chunk, chunk), 0)
    tj = lax.broadcasted_iota(jnp.int32, (chunk, chunk), 1)
    strict = (same_head & (ct < rt), same_head & (ct > rt))
    incl = (same_head & (ct <= rt), same_head & (ct >= rt))
    cum_mat = ((tj <= ti).astype(BF16), (tj >= ti).astype(BF16))
    ins = ((lw0_ref, kd0_ref, bb0_ref, v0_ref, r0_ref, kk0_ref),
           (lw1_ref, kd1_ref, bb1_ref, v1_ref, r1_ref, kk1_ref))
    outs = (y0_ref, y1_ref)
    chains = [(d, p) for p in range(npair) for d in range(2)]

    def stack(x):
        return jnp.concatenate([jnp.where(head_masks[0], x, 0.0), jnp.where(head_masks[1], x, 0.0)],
                               axis=0).astype(BF16)

    def body(c, carry):
        rows = (pl.ds(pl.multiple_of(c * chunk, chunk), chunk),
                pl.ds(pl.multiple_of((group - 1 - c) * chunk, chunk), chunk))
        feats = []
        for d in range(2):
            lw_ref, kd_ref, bb_ref, v_ref, r_ref, kk_ref = ins[d]
            lw = lw_ref[rows[d], :]
            cum_in = sum(_dot(cum_mat[d], part) for part in _split3(lw))
            total = jnp.sum(lw, axis=0, keepdims=True)
            kd = kd_ref[rows[d], :]
            bb = bb_ref[rows[d], :]
            inv_p = jnp.exp(-cum_in)
            to_end = jnp.exp(total - cum_in)
            feats.append(dict(a=-kk_ref[rows[d], :] * jnp.exp(cum_in - lw),
                              r=r_ref[rows[d], :] * jnp.exp(cum_in),
                              b=bb * inv_p, k=kd * inv_p, v=v_ref[rows[d], :],
                              bend=bb * to_end, kend=kd * to_end, dec=jnp.exp(total)))

        def per_chain(name):
            return [stack(feats[d][name][:, p * LANES:(p + 1) * LANES]) for d, p in chains]

        v_s = per_chain("v")
        ar = [jnp.concatenate(pair, axis=0) for pair in zip(per_chain("a"), per_chain("r"))]
        bk = [jnp.concatenate(pair, axis=0) for pair in zip(per_chain("b"), per_chain("k"))]
        ends = [jnp.concatenate(pair, axis=0) for pair in zip(per_chain("bend"), per_chain("kend"))]
        gram = [_dot_nt(x, y) for x, y in zip(ar, bk)]
        l_ab = [jnp.where(strict[d], g[:c2, :c2], 0.0) for (d, _), g in zip(chains, gram)]
        l_ak = [jnp.where(strict[d], g[:c2, c2:], 0.0).astype(BF16) for (d, _), g in zip(chains, gram)]
        m_r = [jnp.where(jnp.concatenate([incl[d], incl[d]], axis=1), g[c2:, :], 0.0).astype(BF16)
               for (d, _), g in zip(chains, gram)]
        lv = [_dot(x, y) for x, y in zip(l_ak, v_s)]

        power = [_bdot(x, x) for x in l_ab]
        inv = [eye + x for x in l_ab]
        for j in range(int(math.log2(chunk)) - 1):
            last = j == int(math.log2(chunk)) - 2
            pb = [x.astype(BF16) for x in power]
            if last:
                inv = [t + _dot(t.astype(BF16), x) for t, x in zip(inv, pb)]
            else:
                both = [_dot(jnp.concatenate([x, t.astype(BF16)], axis=0), x) for t, x in zip(inv, pb)]
                power = [x[:c2] for x in both]
                inv = [t + x[c2:] for t, x in zip(inv, both)]

        states = [s_ref[j] for j in range(len(chains))]
        ar_state = [_dot_nt(x, s.astype(BF16)) for x, s in zip(ar, states)]
        u = [_bdot(t, x[:c2] + y) for t, x, y in zip(inv, ar_state, lv)]
        uv = [jnp.concatenate([x.astype(BF16), y], axis=0) for x, y in zip(u, v_s)]
        y2 = [x[c2:] + _dot(m, w) for x, m, w in zip(ar_state, m_r, uv)]
        upd = [_dot_tn(w, e) for w, e in zip(uv, ends)]
        for j, (d, p) in enumerate(chains):
            cols = slice(p * LANES, (p + 1) * LANES)
            outs[d][rows[d], cols] = y2[j][:chunk] + y2[j][chunk:]
            s_ref[j] = states[j] * feats[d]["dec"][:, cols] + upd[j]
        return carry

    lax.fori_loop(0, group, body, 0)


def _rwkv_scan(feats, batch, seq):
    lw0, lw1, kd0, kd1, bb0, bb1, xv, xr, kk = feats
    chunk = min(RWKV_CHUNK, seq)
    group = min(RWKV_GROUP, seq // chunk)
    tb = chunk * group
    nt = seq // tb
    fwd = pl.BlockSpec((tb, D_RWKV), lambda b, i: (b * nt + i, 0))
    bwd = pl.BlockSpec((tb, D_RWKV), lambda b, i: (b * nt + nt - 1 - i, 0))
    out_sd = jax.ShapeDtypeStruct((batch * seq, D_RWKV), F32)
    return pl.pallas_call(
        functools.partial(_rwkv_scan_kernel, chunk=chunk, group=group),
        grid=(batch, nt),
        in_specs=[fwd] * 6 + [bwd] * 6,
        out_specs=[fwd, bwd],
        out_shape=[out_sd, out_sd],
        scratch_shapes=[pltpu.VMEM((RWKV_HEADS, LANES, LANES), F32)],
        compiler_params=_cparams(("parallel", "arbitrary")),
        name="rwkv_scan",
    )(lw0, kd0, bb0, xv, xr, kk, lw1, kd1, bb1, xv, xr, kk)


def _diff_attn_kernel(q_ref, k_ref, v_ref, band_ref, lam_ref, subw_ref, o_ref, *, seq, tq,
                      lambda_init):
    i = pl.program_id(2)
    lane = lax.broadcasted_iota(jnp.int32, (1, LANES), 1)
    lv = lam_ref[...]
    lam = (jnp.exp(jnp.sum(lv[0:1] * lv[1:2], axis=1, keepdims=True))
           - jnp.exp(jnp.sum(lv[2:3] * lv[3:4], axis=1, keepdims=True)) + lambda_init)
    q = q_ref[...] * (DIFF_QK_DIM ** -0.5)
    kb = k_ref[...].astype(BF16)
    vb = v_ref[...].astype(BF16)
    off = pl.multiple_of((seq - tq) - i * tq, LANES)
    out = jnp.zeros((tq, LANES), F32)
    for hh in range(2):
        bias = band_ref[hh, :, pl.ds(off, seq)]
        parts = []
        for c in range(2):
            lo = hh * HEAD_DIM + c * DIFF_QK_DIM
            qm = jnp.where((lane >= lo) & (lane < lo + DIFF_QK_DIM), q, 0.0).astype(BF16)
            s = _dot_nt(qm, kb) + bias
            e = jnp.exp(s - jnp.max(s, axis=-1, keepdims=True))
            denom = jnp.sum(e, axis=-1, keepdims=True)
            parts.append(_dot(e.astype(BF16), vb) / denom)
        head = parts[0] - lam * parts[1]
        in_head = (lane >= hh * HEAD_DIM) & (lane < (hh + 1) * HEAD_DIM)
        head = jnp.where(in_head, head, 0.0)
        ms = jnp.sum(head * head, axis=-1, keepdims=True) * (1.0 / HEAD_DIM)
        out = out + head * lax.rsqrt(ms + NORM_EPS)
    o_ref[...] = out * subw_ref[...] * (1.0 - lambda_init)


def _diff_attention(proj, band, lam_vecs, subln_w, lambda_init, batch, seq):
    tq = min(ATTN_TQ, seq)
    nq = seq // tq
    npair = DIFF_HEADS // 2
    col0 = (4 * D_RET + 4 * D_RWKV) // LANES
    width = band.shape[-1]
    return pl.pallas_call(
        functools.partial(_diff_attn_kernel, seq=seq, tq=tq, lambda_init=lambda_init),
        grid=(batch, npair, nq),
        in_specs=[pl.BlockSpec((tq, LANES), lambda b, p, i: (b * nq + i, col0 + p)),
                  pl.BlockSpec((seq, LANES), lambda b, p, i: (b, col0 + npair + p)),
                  pl.BlockSpec((seq, LANES), lambda b, p, i: (b, col0 + 2 * npair + p)),
                  pl.BlockSpec((2, tq, width), lambda b, p, i: (p, 0, 0)),
                  pl.BlockSpec((4, DIFF_QK_DIM), lambda b, p, i: (0, 0)),
                  pl.BlockSpec((1, LANES), lambda b, p, i: (0, 0))],
        out_specs=pl.BlockSpec((tq, LANES), lambda b, p, i: (b * nq + i, p)),
        out_shape=jax.ShapeDtypeStruct((batch * seq, D_DIFF), F32),
        compiler_params=_cparams(("parallel", "parallel", "arbitrary")),
        name="diff_attention",
    )(proj, proj, proj, band, lam_vecs, jnp.tile(subln_w, 2).reshape(1, LANES))


def _t5_bucket(rel):
    nb = REL_BUCKETS // 2
    max_exact = nb // 2
    n = jnp.abs(rel)
    nf = jnp.maximum(n, 1).astype(jnp.float32)
    large = max_exact + (jnp.log(nf / max_exact) / math.log(REL_MAX_DIST / max_exact)
                         * (nb - max_exact)).astype(jnp.int32)
    large = jnp.minimum(large, nb - 1)
    return jnp.where(rel > 0, nb, 0) + jnp.where(n < max_exact, n, large)


def _bias_band(rel_bias, seq, tq):
    width = 2 * seq - tq
    period = 2 * seq - 1
    m = jnp.arange(period, dtype=jnp.int32)
    rel = jnp.where(m < width, m, m - period) - (seq - tq)
    vec = rel_bias.astype(F32)[_t5_bucket(rel)].T
    rows = jnp.tile(vec, (1, tq))[:, :tq * (period - 1)].reshape(-1, tq, period - 1)
    return rows[:, :, :width]


def _outproj_kernel(x_ref, yr_ref, y0_ref, y1_ref, bonus_ref, gate_ref, yd_ref, lnw_ref, lnb_ref,
                    w_ref, o_ref):
    avg = _head_avg_matrix()
    acc = x_ref[...]
    acc = acc + _dot(yr_ref[...].astype(BF16), w_ref[0:D_RET, :])
    for p in range(RWKV_HEADS // 2):
        cols = slice(p * LANES, (p + 1) * LANES)
        y = _group_norm(y0_ref[:, cols] + y1_ref[:, cols], avg, lnw_ref[:, cols], lnb_ref[:, cols],
                        RWKV_GN_EPS)
        y = (y + bonus_ref[:, cols]) * gate_ref[:, cols]
        acc = acc + _dot(y.astype(BF16), w_ref[D_RET + p * LANES:D_RET + (p + 1) * LANES, :])
    acc = acc + _dot(yd_ref[...].astype(BF16), w_ref[D_RET + D_RWKV:, :])
    o_ref[...] = acc


def _outproj(x2, y_ret, y0, y1, bonus, gate, y_diff, ln_w, ln_b, w_bf16, tm=512):
    m, d = x2.shape
    tm = min(tm, m)
    row = lambda n: pl.BlockSpec((tm, n), lambda i: (i, 0))
    vec = pl.BlockSpec((1, D_RWKV), lambda i: (0, 0))
    return pl.pallas_call(
        _outproj_kernel,
        grid=(m // tm,),
        in_specs=[row(d), row(D_RET), row(D_RWKV), row(D_RWKV), row(D_RWKV), row(D_RWKV), row(D_DIFF),
                  vec, vec, pl.BlockSpec(w_bf16.shape, lambda i: (0, 0))],
        out_specs=row(d),
        out_shape=jax.ShapeDtypeStruct((m, d), F32),
        compiler_params=_cparams(("parallel",)),
        name="outproj",
    )(x2, y_ret, y0, y1, bonus, gate, y_diff, ln_w.reshape(1, D_RWKV), ln_b.reshape(1, D_RWKV), w_bf16)


def _ffn_up_kernel(x_ref, g_ref, wg_ref, wu_ref, o_ref, h_ref):
    @pl.when(pl.program_id(1) == 0)
    def _():
        x = x_ref[...]
        ms = jnp.mean(x * x, axis=-1, keepdims=True)
        h_ref[...] = (x * lax.rsqrt(ms + NORM_EPS) * g_ref[...]).astype(BF16)

    h = h_ref[...]
    gate = _dot(h, wg_ref[...])
    up = _dot(h, wu_ref[...])
    o_ref[...] = (gate * jax.nn.sigmoid(gate) * up).astype(BF16)


def _ffn_up(x2, g, wg_bf16, wu_bf16, tm=512, tf=1408):
    m, d = x2.shape
    tm = min(tm, m)
    f = wg_bf16.shape[1]
    return pl.pallas_call(
        _ffn_up_kernel,
        grid=(m // tm, f // tf),
        in_specs=[pl.BlockSpec((tm, d), lambda i, j: (i, 0)),
                  pl.BlockSpec((1, d), lambda i, j: (0, 0)),
                  pl.BlockSpec((d, tf), lambda i, j: (0, j)),
                  pl.BlockSpec((d, tf), lambda i, j: (0, j))],
        out_specs=pl.BlockSpec((tm, tf), lambda i, j: (i, j)),
        out_shape=jax.ShapeDtypeStruct((m, f), BF16),
        scratch_shapes=[pltpu.VMEM((tm, d), BF16)],
        compiler_params=_cparams(("parallel", "arbitrary")),
        name="ffn_up",
    )(x2, g.reshape(1, d), wg_bf16, wu_bf16)


def _ffn_down_kernel(x_ref, h_ref, w_ref, g_ref, o_ref, *, final_norm):
    y = x_ref[...] + _dot(h_ref[...], w_ref[...])
    if final_norm:
        ms = jnp.mean(y * y, axis=-1, keepdims=True)
        y = y * lax.rsqrt(ms + NORM_EPS) * g_ref[...]
    o_ref[...] = y


def _ffn_down(x2, hidden, w_bf16, g, final_norm, tm=512):
    m, d = x2.shape
    tm = min(tm, m)
    f = hidden.shape[1]
    return pl.pallas_call(
        functools.partial(_ffn_down_kernel, final_norm=final_norm),
        grid=(m // tm,),
        in_specs=[pl.BlockSpec((tm, d), lambda i: (i, 0)),
                  pl.BlockSpec((tm, f), lambda i: (i, 0)),
                  pl.BlockSpec((f, d), lambda i: (0, 0)),
                  pl.BlockSpec((1, d), lambda i: (0, 0))],
        out_specs=pl.BlockSpec((tm, d), lambda i: (i, 0)),
        out_shape=jax.ShapeDtypeStruct((m, d), F32),
        compiler_params=_cparams(("parallel",)),
        name="ffn_down",
    )(x2, hidden, w_bf16, g.reshape(1, d))


def _rope_tables(seq):
    half = HEAD_DIM // 2
    freqs = ROPE_BASE ** (-jnp.arange(half, dtype=F32) / half)
    ang = jnp.arange(seq, dtype=jnp.int32).astype(F32)[:, None] * freqs[None, :]
    cos = jnp.cos(ang)
    sin = jnp.sin(ang)
    cos_tab = jnp.tile(cos, (1, LANES // half))
    sin_tab = jnp.tile(jnp.concatenate([-sin, sin], axis=1), (1, LANES // HEAD_DIM))
    return cos_tab, sin_tab


def kernel(x, mix_norm_g, w_in, w_out, ret_gn_w, ret_gn_b, rwkv_mu, rwkv_w0, rwkv_w1, rwkv_w2, rwkv_a0, rwkv_a1, rwkv_a2, rwkv_g1, rwkv_g2, rwkv_k_k, rwkv_k_a, rwkv_r_k, rwkv_ln_w, rwkv_ln_b, diff_lambda, diff_subln_w, rel_bias, ffn_norm_g, w_gate, w_up, w_down, final_norm_g):
    batch, seq, d = x.shape
    depth = w_in.shape[0]
    cos_tab, sin_tab = _rope_tables(seq)
    band = _bias_band(rel_bias, seq, min(ATTN_TQ, seq))
    x2 = x.reshape(batch * seq, d)
    for l in range(depth):
        proj = _norm_matmul(x2, mix_norm_g[l], w_in[l].astype(BF16))
        y_ret = _retention(proj, cos_tab, sin_tab, ret_gn_w[l], ret_gn_b[l], batch, seq)
        feats = _rwkv_prep(proj, rwkv_mu[l], rwkv_w0[l], rwkv_w1[l], rwkv_w2[l], rwkv_a0[l],
                           rwkv_a1[l], rwkv_a2[l], rwkv_g1[l], rwkv_g2[l], rwkv_k_k[l],
                           rwkv_k_a[l], rwkv_r_k[l].reshape(-1), batch, seq)
        y0, y1 = _rwkv_scan(feats[:9], batch, seq)
        lambda_init = 0.8 - 0.6 * math.exp(-0.3 * l)
        y_diff = _diff_attention(proj, band, diff_lambda[l], diff_subln_w[l], lambda_init,
                                 batch, seq)
        x2 = _outproj(x2, y_ret, y0, y1, feats[10], feats[9], y_diff, rwkv_ln_w[l], rwkv_ln_b[l],
                      w_out[l].astype(BF16))
        hidden = _ffn_up(x2, ffn_norm_g[l], w_gate[l].astype(BF16), w_up[l].astype(BF16))
        x2 = _ffn_down(x2, hidden, w_down[l].astype(BF16), final_norm_g, l == depth - 1)
    return x2.reshape(batch, seq, d)
```

```python
import functools
import math

import numpy as np
import jax
import jax.numpy as jnp
from jax import lax
from jax.experimental import pallas as pl
from jax.experimental.pallas import tpu as pltpu

F32 = jnp.float32
BF16 = jnp.bfloat16

D_MODEL = 1024
HEAD_DIM = 64
LANES = 128
BF16_ROWS = 16
LOG2E = math.log2(math.e)
N_HEADS = D_MODEL // HEAD_DIM
RET_HEADS = (3 * N_HEADS) // 8
RWKV_HEADS = (3 * N_HEADS) // 8
DIFF_HEADS = N_HEADS - RET_HEADS - RWKV_HEADS
D_RET = RET_HEADS * HEAD_DIM
D_RWKV = RWKV_HEADS * HEAD_DIM
D_DIFF = DIFF_HEADS * HEAD_DIM
DIFF_QK_DIM = HEAD_DIM // 2
D_IN = 4 * D_RET + 4 * D_RWKV + 3 * D_DIFF
ROPE_BASE = 10000.0
D_FF = -(-8 * D_MODEL // (3 * 256)) * 256
REL_BUCKETS = 32
REL_MAX_DIST = 128
NORM_EPS = 1e-6
RET_GN_EPS = 1e-5
RWKV_GN_EPS = 64e-5

RET_CHUNK = 256
RWKV_CHUNK = 64
RWKV_GROUP = 4
ATTN_TQ = 256
VMEM_LIMIT = 56 * 1024 * 1024


def _cparams(sem):
    return pltpu.CompilerParams(dimension_semantics=sem, vmem_limit_bytes=VMEM_LIMIT)


def _dot(a, b):
    return jnp.dot(a, b, preferred_element_type=F32)


def _bdot(a, b):
    return _dot(a.astype(BF16), b.astype(BF16))


def _dot_nt(a, b):
    return lax.dot_general(a, b, (((1,), (1,)), ((), ())), preferred_element_type=F32)


def _dot_tn(a, b):
    return lax.dot_general(a, b, (((0,), (0,)), ((), ())), preferred_element_type=F32)


def _head_avg_matrix():
    r = lax.broadcasted_iota(jnp.int32, (LANES, LANES), 0) // HEAD_DIM
    c = lax.broadcasted_iota(jnp.int32, (LANES, LANES), 1) // HEAD_DIM
    return jnp.where(r == c, 1.0 / HEAD_DIM, 0.0).astype(BF16)


def _head_mean(x, avg):
    hi = x.astype(BF16)
    lo = (x - hi.astype(F32)).astype(BF16)
    return _dot(hi, avg) + _dot(lo, avg)


def _group_norm(y, avg, w, b, eps):
    yc = y - _head_mean(y, avg)
    var = _head_mean(yc * yc, avg)
    return yc * lax.rsqrt(var + eps) * w + b


def _norm_matmul_kernel(x_ref, g_ref, w_ref, o_ref):
    x = x_ref[...]
    ms = jnp.mean(x * x, axis=-1, keepdims=True)
    h = x * lax.rsqrt(ms + NORM_EPS) * g_ref[...]
    o_ref[...] = _dot(h.astype(BF16), w_ref[...])


def _norm_matmul(x2, g, w_bf16, tm=256):
    m, d = x2.shape
    n = w_bf16.shape[1]
    return pl.pallas_call(
        _norm_matmul_kernel,
        grid=(m // tm,),
        in_specs=[pl.BlockSpec((tm, d), lambda i: (i, 0)),
                  pl.BlockSpec((1, d), lambda i: (0, 0)),
                  pl.BlockSpec((d, n), lambda i: (0, 0))],
        out_specs=pl.BlockSpec((tm, n), lambda i: (i, 0)),
        out_shape=jax.ShapeDtypeStruct((m, n), F32),
        compiler_params=_cparams(("parallel",)),
        name="norm_inproj",
    )(x2, g.reshape(1, d), w_bf16)


def _retention_kernel(q_ref, k_ref, v_ref, g_ref, cos_ref, sin_ref, gnw_ref, gnb_ref, o_ref,
                      qs_ref, ks_ref, acc_ref, *, seq, chunk, log_gamma):
    nc = seq // chunk
    lane = lax.broadcasted_iota(jnp.int32, (1, LANES), 1)
    first_half = (lane & (HEAD_DIM - 1)) < (HEAD_DIM // 2)
    head_masks = (lane < HEAD_DIM, lane >= HEAD_DIM)
    ri = lax.broadcasted_iota(jnp.int32, (LANES, LANES), 0)
    ci = lax.broadcasted_iota(jnp.int32, (LANES, LANES), 1)
    block_diag = (ri // HEAD_DIM) == (ci // HEAD_DIM)
    avg = _head_avg_matrix()
    ti = lax.broadcasted_iota(jnp.int32, (chunk, chunk), 0)
    tj = lax.broadcasted_iota(jnp.int32, (chunk, chunk), 1)
    dist = jnp.abs(ti - tj).astype(F32)
    pos = lax.broadcasted_iota(jnp.int32, (chunk, 1), 0).astype(F32)

    def rope(x, cos, sin):
        partner = jnp.where(first_half, pltpu.roll(x, LANES - HEAD_DIM // 2, 1),
                            pltpu.roll(x, HEAD_DIM // 2, 1))
        return x * cos + partner * sin

    for p in range(RET_HEADS // 2):
        cols = slice(p * LANES, (p + 1) * LANES)
        lg = jnp.where(head_masks[0], log_gamma[2 * p], log_gamma[2 * p + 1]).astype(F32)
        decay_mats = [jnp.exp(dist * log_gamma[2 * p + hh]) for hh in range(2)]
        xi_f = jnp.exp((pos + 1.0) * lg)
        xi_b = jnp.exp((chunk - pos) * lg)
        zeta_f = jnp.exp((chunk - 1.0 - pos) * lg)
        zeta_b = jnp.exp(pos * lg)
        dec_c = jnp.exp(chunk * lg)

        def fwd_step(n, state):
            rows = pl.ds(pl.multiple_of(n * chunk, chunk), chunk)
            cos = cos_ref[rows, :]
            sin = sin_ref[rows, :]
            q = rope(q_ref[rows, cols], cos, sin) * (HEAD_DIM ** -0.5)
            k = rope(k_ref[rows, cols], cos, sin)
            v = v_ref[rows, cols]
            qs_ref[rows, cols] = q
            ks_ref[rows, cols] = k
            kb = k.astype(BF16)
            vb = v.astype(BF16)
            o = _dot((q * xi_f).astype(BF16), state.astype(BF16))
            for hh in range(2):
                qm = jnp.where(head_masks[hh], q, 0.0).astype(BF16)
                s = _dot_nt(qm, kb) * decay_mats[hh]
                oh = _dot(s.astype(BF16), vb)
                o = o + jnp.where(head_masks[hh], oh, 0.0)
            acc_ref[rows, cols] = o
            kv = _dot_tn((k * zeta_f).astype(BF16), vb)
            return state * dec_c + jnp.where(block_diag, kv, 0.0)

        lax.fori_loop(0, nc, fwd_step, jnp.zeros((LANES, LANES), F32))

        def bwd_step(i, state):
            n = nc - 1 - i
            rows = pl.ds(pl.multiple_of(n * chunk, chunk), chunk)
            q = qs_ref[rows, cols]
            k = ks_ref[rows, cols]
            v = v_ref[rows, cols]
            o = acc_ref[rows, cols] + _dot((q * xi_b).astype(BF16), state.astype(BF16))
            y = _group_norm(o, avg, gnw_ref[:, cols], gnb_ref[:, cols], RET_GN_EPS)
            g = g_ref[rows, cols]
            o_ref[rows, cols] = g * jax.nn.sigmoid(g) * y
            kv = _dot_tn((k * zeta_b).astype(BF16), v.astype(BF16))
            return state * dec_c + jnp.where(block_diag, kv, 0.0)

        lax.fori_loop(0, nc, bwd_step, jnp.zeros((LANES, LANES), F32))


def _retention(proj, cos_tab, sin_tab, gn_w, gn_b, batch, seq):
    chunk = min(RET_CHUNK, seq)
    log_gamma = tuple(float(np.log1p(-np.exp2(-5.0 - h))) for h in range(RET_HEADS))
    kern = functools.partial(_retention_kernel, seq=seq, chunk=chunk, log_gamma=log_gamma)
    col = lambda j: pl.BlockSpec((seq, D_RET), lambda b, j=j: (b, j))
    full = lambda shape: pl.BlockSpec(shape, lambda b: (0,) * len(shape))
    return pl.pallas_call(
        kern,
        grid=(batch,),
        in_specs=[col(0), col(1), col(2), col(3), full((seq, LANES)), full((seq, LANES)),
                  full((1, D_RET)), full((1, D_RET))],
        out_specs=pl.BlockSpec((seq, D_RET), lambda b: (b, 0)),
        out_shape=jax.ShapeDtypeStruct((batch * seq, D_RET), F32),
        scratch_shapes=[pltpu.VMEM((seq, D_RET), F32)] * 3,
        compiler_params=_cparams(("parallel",)),
        name="retention",
    )(proj, proj, proj, proj, cos_tab, sin_tab, gn_w.reshape(1, D_RET), gn_b.reshape(1, D_RET))


def _rwkv_prep_kernel(r_ref, k_ref, v_ref, z_ref, rp_ref, kp_ref, vp_ref, zp_ref,
                      rn_ref, kn_ref, vn_ref, zn_ref, mu_ref, w0_ref, w1_ref, w2_ref,
                      a0_ref, a1_ref, a2_ref, g1_ref, g2_ref, kk_ref, ka_ref, rk_ref,
                      lw0_o, lw1_o, kd0_o, kd1_o, bb0_o, bb1_o, v_o, r_o, kk_o, gate_o, bonus_o,
                      *, tb):
    i = pl.program_id(1)
    has_prev = (i > 0).astype(F32)
    has_next = (i < pl.num_programs(1) - 1).astype(F32)
    row = lax.broadcasted_iota(jnp.int32, (tb, 1), 0)
    avg = _head_avg_matrix()

    def head_sum(x):
        return jnp.concatenate([_head_mean(x[:, c * LANES:(c + 1) * LANES], avg)
                                for c in range(D_RWKV // LANES)], axis=1) * HEAD_DIM

    def shifted(f_ref, p_ref, n_ref):
        f = f_ref[...]
        prev = jnp.where(row == 0, p_ref[7:8, :] * has_prev, pltpu.roll(f, 1, 0))
        nxt = jnp.where(row == tb - 1, n_ref[0:1, :] * has_next, pltpu.roll(f, tb - 1, 0))
        return f, prev - f, nxt - f

    def mix(parts, idx):
        f, dp, dn = parts
        return f + mu_ref[idx, 0:1, :] * dp + mu_ref[idx, 1:2, :] * dn

    xr = mix(shifted(r_ref, rp_ref, rn_ref), 0)
    xk = mix(shifted(k_ref, kp_ref, kn_ref), 1)
    xv = mix(shifted(v_ref, vp_ref, vn_ref), 2)
    zparts = shifted(z_ref, zp_ref, zn_ref)
    xw = mix(zparts, 3)
    xa = mix(zparts, 4)
    xg = mix(zparts, 5)

    gate = _bdot(jax.nn.sigmoid(_bdot(xg, g1_ref[...])), g2_ref[...])
    w_lo = _bdot(jnp.tanh(_bdot(xw, w1_ref[...])), w2_ref[...])
    a_lo = _bdot(_bdot(xa, a1_ref[...]), a2_ref[...])
    kk = xk * kk_ref[...]
    kk = kk / jnp.maximum(jnp.sqrt(head_sum(kk * kk)), 1e-12)

    ksum = jnp.zeros_like(xk)
    outs = ((lw0_o, kd0_o, bb0_o), (lw1_o, kd1_o, bb1_o))
    for d in range(2):
        cols = slice(d * D_RWKV, (d + 1) * D_RWKV)
        w_raw = w0_ref[d:d + 1, :] + w_lo[:, cols]
        log_w = -math.exp(-0.5) * jax.nn.sigmoid(w_raw)
        a = jax.nn.sigmoid(a0_ref[d:d + 1, :] + a_lo[:, cols])
        k_dir = xk * (1.0 + (a - 1.0) * ka_ref[...])
        ksum = ksum + k_dir
        lw_o, kd_o, bb_o = outs[d]
        lw_o[...] = log_w
        kd_o[...] = k_dir
        bb_o[...] = kk * a
    bonus = head_sum(xr * ksum * rk_ref[...])
    v_o[...] = xv
    r_o[...] = xr
    kk_o[...] = kk
    gate_o[...] = gate
    bonus_o[...] = bonus * xv


def _rwkv_prep(proj, mu, w0, w1, w2, a0, a1, a2, g1, g2, k_k, k_a, r_k, batch, seq):
    tb = min(256, seq)
    nt = seq // tb
    d = D_RWKV
    col0 = (4 * D_RET) // d
    main = lambda j: pl.BlockSpec((tb, d), lambda b, i, j=j: (b * nt + i, col0 + j))
    prev = lambda j: pl.BlockSpec(
        (8, d), lambda b, i, j=j: (b * (seq // 8) + jnp.maximum(i * (tb // 8) - 1, 0), col0 + j))
    nxt = lambda j: pl.BlockSpec(
        (8, d), lambda b, i, j=j: (b * (seq // 8) + jnp.minimum((i + 1) * (tb // 8), seq // 8 - 1),
                                   col0 + j))
    full = lambda shape: pl.BlockSpec(shape, lambda b, i: (0,) * len(shape))
    out_spec = pl.BlockSpec((tb, d), lambda b, i: (b * nt + i, 0))
    out_sd = jax.ShapeDtypeStruct((batch * seq, d), F32)
    row = lambda a: a.reshape(1, d)

    def both_dirs(first, second):
        r = first.shape[-1]
        blk = jnp.zeros((2 * r, 2 * d), F32)
        blk = blk.at[:r, :d].set(second[0]).at[r:, d:].set(second[1])
        return jnp.concatenate([first[0], first[1]], axis=1).astype(BF16), blk.astype(BF16)

    w1, w2 = both_dirs(w1, w2)
    a1, a2 = both_dirs(a1, a2)
    g1, g2 = g1.astype(BF16), g2.astype(BF16)
    return pl.pallas_call(
        functools.partial(_rwkv_prep_kernel, tb=tb),
        grid=(batch, nt),
        in_specs=[main(0), main(1), main(2), main(3), prev(0), prev(1), prev(2), prev(3),
                  nxt(0), nxt(1), nxt(2), nxt(3),
                  full(mu.shape), full(w0.shape), full(w1.shape), full(w2.shape),
                  full(a0.shape), full(a1.shape), full(a2.shape), full(g1.shape), full(g2.shape),
                  full((1, d)), full((1, d)), full((1, d))],
        out_specs=[out_spec] * 11,
        out_shape=[out_sd] * 11,
        compiler_params=_cparams(("parallel", "parallel")),
        name="rwkv_features",
    )(*([proj] * 12), mu, w0, w1, w2, a0, a1, a2, g1, g2, row(k_k), row(k_a), row(r_k))


def _split3(x):
    hi = x.astype(BF16)
    rest = x - hi.astype(F32)
    mid = rest.astype(BF16)
    lo = (rest - mid.astype(F32)).astype(BF16)
    return hi, mid, lo


def _rwkv_scan_kernel(lw0_ref, kd0_ref, bb0_ref, v0_ref, r0_ref, kk0_ref,
                      lw1_ref, kd1_ref, bb1_ref, v1_ref, r1_ref, kk1_ref,
                      y0_ref, y1_ref, s_ref, gr_ref, u0_ref, mr_ref, vbd_ref, ev_ref, dec_ref,
                      *, chunk, group):
    npair = RWKV_HEADS // 2
    assert 2 * chunk == LANES

    @pl.when(pl.program_id(1) == 0)
    def _():
        s_ref[...] = jnp.zeros_like(s_ref)

    lane = lax.broadcasted_iota(jnp.int32, (1, LANES), 1)
    halves = (lane < chunk, lane >= chunk)
    ri = lax.broadcasted_iota(jnp.int32, (LANES, LANES), 0)
    ci = lax.broadcasted_iota(jnp.int32, (LANES, LANES), 1)
    same_head = (ri // HEAD_DIM) == (ci // HEAD_DIM)
    ti = lax.broadcasted_iota(jnp.int32, (chunk, LANES), 0)
    tj = lax.broadcasted_iota(jnp.int32, (chunk, LANES), 1) % chunk
    eye = (ti == tj).astype(F32)
    strict = (tj < ti, tj > ti)
    incl = (tj <= ti, tj >= ti)
    si = lax.broadcasted_iota(jnp.int32, (chunk, chunk), 0)
    sj = lax.broadcasted_iota(jnp.int32, (chunk, chunk), 1)
    cum_mat = ((sj <= si).astype(BF16), (sj >= si).astype(BF16))
    ins = ((lw0_ref, kd0_ref, bb0_ref, v0_ref, r0_ref, kk0_ref),
           (lw1_ref, kd1_ref, bb1_ref, v1_ref, r1_ref, kk1_ref))
    outs = (y0_ref, y1_ref)
    items = [(c, d, p) for c in range(group) for p in range(npair) for d in range(2)]
    rows = lambda c, d: slice((c if d == 0 else group - 1 - c) * chunk,
                              ((c if d == 0 else group - 1 - c) + 1) * chunk)

    def blockdiag(x):
        return jnp.concatenate([jnp.where(halves[0], x, 0.0), jnp.where(halves[1], x, 0.0)],
                               axis=0).astype(BF16)

    feats = {}
    for c in range(group):
        for d in range(2):
            lw_ref, kd_ref, bb_ref, v_ref, r_ref, kk_ref = ins[d]
            rs = rows(c, d)
            lw = lw_ref[rs, :]
            cum_in = sum(_dot(cum_mat[d], part) for part in _split3(lw))
            total = jnp.sum(lw, axis=0, keepdims=True)
            kd = kd_ref[rs, :]
            bb = bb_ref[rs, :]
            inv_p = jnp.exp(-cum_in)
            to_end = jnp.exp(total - cum_in)
            feats[c, d] = dict(a=-kk_ref[rs, :] * jnp.exp(cum_in - lw), r=r_ref[rs, :] * jnp.exp(cum_in),
                               b=bb * inv_p, k=kd * inv_p, v=v_ref[rs, :],
                               bend=bb * to_end, kend=kd * to_end, dec=jnp.exp(total))

    def per_item(name):
        return [feats[c, d][name][:, p * LANES:(p + 1) * LANES] for c, d, p in items]

    a = per_item("a")
    v = per_item("v")
    v_bd = [blockdiag(x) for x in v]
    ar = [jnp.concatenate(pair, axis=0).astype(BF16) for pair in zip(a, per_item("r"))]
    bk = [jnp.concatenate([blockdiag(x), blockdiag(y)], axis=0)
          for x, y in zip(per_item("b"), per_item("k"))]
    gram = [_dot_nt(x, y) for x, y in zip(ar, bk)]
    l_ab = [jnp.where(strict[d], g[:chunk, :LANES], 0.0) for (_, d, _), g in zip(items, gram)]
    l_ak = [jnp.where(strict[d], g[:chunk, LANES:], 0.0).astype(BF16) for (_, d, _), g in zip(items, gram)]
    for n, ((_, d, _), g) in enumerate(zip(items, gram)):
        mr_ref[n] = jnp.where(jnp.concatenate([incl[d], incl[d]], axis=1), g[chunk:, :], 0.0).astype(BF16)
    lv = [_dot(x, y) for x, y in zip(l_ak, v_bd)]

    power = [_dot(x.astype(BF16), blockdiag(x)) for x in l_ab]
    inv = [eye + x for x in l_ab]
    rounds = int(math.log2(chunk)) - 1
    for j in range(rounds):
        pd = [blockdiag(x) for x in power]
        if j == rounds - 1:
            inv = [t + _dot(t.astype(BF16), x) for t, x in zip(inv, pd)]
        else:
            both = [_dot(jnp.concatenate([x, t], axis=0).astype(BF16), y)
                    for x, t, y in zip(power, inv, pd)]
            power = [x[:chunk] for x in both]
            inv = [t + x[chunk:] for t, x in zip(inv, both)]
    ta = [_dot(t.astype(BF16), jnp.concatenate([blockdiag(x), blockdiag(y)], axis=1))
          for t, x, y in zip(inv, a, lv)]
    for n, (c, d, p) in enumerate(items):
        cols = slice(p * LANES, (p + 1) * LANES)
        gr_ref[n] = jnp.concatenate([ta[n][:, :LANES].astype(BF16), ar[n][chunk:]], axis=0)
        u0_ref[n] = ta[n][:, LANES:]
        vbd_ref[n] = v_bd[n]
        ev_ref[n] = jnp.concatenate([feats[c, d]["bend"][:, cols], feats[c, d]["kend"][:, cols],
                                     v[n]], axis=0).astype(BF16)
        dec_ref[n] = jnp.broadcast_to(feats[c, d]["dec"][:, cols], (8, LANES))

    per_chunk = 2 * npair
    for c in range(group):
        ns = range(c * per_chunk, (c + 1) * per_chunk)
        states = [s_ref[j] for j in range(per_chunk)]
        prod = [_dot_nt(gr_ref[n], s.astype(BF16)) for n, s in zip(ns, states)]
        u = [x[:chunk] + u0_ref[n] for n, x in zip(ns, prod)]
        y = [x[chunk:] + _dot(mr_ref[n], jnp.concatenate([blockdiag(w), vbd_ref[n]], axis=0))
             for n, x, w in zip(ns, prod, u)]
        upd = [_dot_tn(jnp.concatenate([w.astype(BF16), ev_ref[n, 2 * chunk:, :]], axis=0),
                       ev_ref[n, :2 * chunk, :]) for n, w in zip(ns, u)]
        for j, n in enumerate(ns):
            _, d, p = items[n]
            outs[d][rows(c, d), p * LANES:(p + 1) * LANES] = y[j]
            s_ref[j] = states[j] * dec_ref[n, 0:1, :] + jnp.where(same_head, upd[j], 0.0)


def _rwkv_scan(feats, batch, seq):
    lw0, lw1, kd0, kd1, bb0, bb1, xv, xr, kk = feats
    chunk = min(RWKV_CHUNK, seq)
    group = min(RWKV_GROUP, seq // chunk)
    tb = chunk * group
    nt = seq // tb
    n_items = group * RWKV_HEADS
    fwd = pl.BlockSpec((tb, D_RWKV), lambda b, i: (b * nt + i, 0))
    bwd = pl.BlockSpec((tb, D_RWKV), lambda b, i: (b * nt + nt - 1 - i, 0))
    out_sd = jax.ShapeDtypeStruct((batch * seq, D_RWKV), F32)
    return pl.pallas_call(
        functools.partial(_rwkv_scan_kernel, chunk=chunk, group=group),
        grid=(batch, nt),
        in_specs=[fwd] * 6 + [bwd] * 6,
        out_specs=[fwd, bwd],
        out_shape=[out_sd, out_sd],
        scratch_shapes=[pltpu.VMEM((RWKV_HEADS, LANES, LANES), F32),
                        pltpu.VMEM((n_items, 2 * chunk, LANES), BF16),
                        pltpu.VMEM((n_items, chunk, LANES), F32),
                        pltpu.VMEM((n_items, chunk, 4 * chunk), BF16),
                        pltpu.VMEM((n_items, 2 * chunk, LANES), BF16),
                        pltpu.VMEM((n_items, 3 * chunk, LANES), BF16),
                        pltpu.VMEM((n_items, 8, LANES), F32)],
        compiler_params=_cparams(("parallel", "arbitrary")),
        name="rwkv_scan",
    )(lw0, kd0, bb0, xv, xr, kk, lw1, kd1, bb1, xv, xr, kk)


def _diff_attn_kernel(q_ref, k_ref, v_ref, band_ref, lam_ref, subw_ref, o_ref, kb_ref, vt_ref,
                      acc_ref, s_ref, *, seq, tq, tk, lambda_init):
    i = pl.program_id(2)

    @pl.when(i == 0)
    def _():
        kb_ref[...] = k_ref[...].astype(BF16)
        vt_ref[...] = jnp.transpose(v_ref[...]).astype(BF16)

    lane = lax.broadcasted_iota(jnp.int32, (1, LANES), 1)
    lv = lam_ref[...]
    lam = (jnp.exp(jnp.sum(lv[0:1] * lv[1:2], axis=1, keepdims=True))
           - jnp.exp(jnp.sum(lv[2:3] * lv[3:4], axis=1, keepdims=True)) + lambda_init)
    q = q_ref[...] * (DIFF_QK_DIM ** -0.5 * LOG2E)
    off = (seq - tq) - i * tq
    ones_rows = jnp.ones((BF16_ROWS, tk), BF16)
    n_maps = 4
    qms = [jnp.where((lane >= m * DIFF_QK_DIM) & (lane < (m + 1) * DIFF_QK_DIM), q, 0.0).astype(BF16)
           for m in range(n_maps)]
    n_tiles = seq // tk
    acc_ref[...] = jnp.zeros_like(acc_ref)

    def tile_start(j):
        t = i + j
        return pl.multiple_of(jnp.where(t >= n_tiles, t - n_tiles, t) * tk, tk)

    def stage_scores(j, m):
        s_ref[j % 2, m] = _dot_nt(kb_ref[pl.ds(tile_start(j), tk), :], qms[m])

    m_run = [jnp.full((1, tq), -1e30, F32)] * n_maps
    for m in range(n_maps):
        stage_scores(0, m)
    for j in range(n_tiles):
        slot = j % 2
        start = tile_start(j)
        far = 2 <= j <= n_tiles - 2
        rows = BF16_ROWS // 2 if far else tk
        bias = [band_ref[hh, pl.ds(pl.multiple_of(off + start, rows), rows), :] for hh in range(2)]
        vt = [jnp.concatenate([vt_ref[hh * HEAD_DIM:(hh + 1) * HEAD_DIM, pl.ds(start, tk)],
                               ones_rows], axis=0) for hh in range(2)]
        for m in range(n_maps):
            if j + 1 < n_tiles:
                stage_scores(j + 1, m)
            if far:
                s = s_ref[slot, m]
                const = bias[m // 2][0:1, :]
                m_new = jnp.maximum(m_run[m], jnp.max(s, axis=0, keepdims=True) + const)
                e = jnp.exp2(s - (m_new - const)).astype(BF16)
            else:
                s = s_ref[slot, m] + bias[m // 2]
                m_new = jnp.maximum(m_run[m], jnp.max(s, axis=0, keepdims=True))
                e = jnp.exp2(s - m_new).astype(BF16)
            pv = _dot(vt[m // 2], e)
            acc_ref[m] = acc_ref[m] * jnp.exp2(m_run[m] - m_new) + pv
            m_run[m] = m_new
    parts = [acc_ref[m, :HEAD_DIM, :] / acc_ref[m, HEAD_DIM:HEAD_DIM + 1, :] for m in range(n_maps)]
    heads = []
    for hh in range(2):
        head = parts[2 * hh] - lam * parts[2 * hh + 1]
        ms = jnp.mean(head * head, axis=0, keepdims=True)
        heads.append(head * lax.rsqrt(ms + NORM_EPS))
    out = jnp.transpose(jnp.concatenate(heads, axis=0))
    o_ref[...] = out * subw_ref[...] * (1.0 - lambda_init)


def _diff_attention(proj, band, lam_vecs, subln_w, lambda_init, batch, seq):
    tq = min(ATTN_TQ, seq)
    nq = seq // tq
    npair = DIFF_HEADS // 2
    col0 = (4 * D_RET + 4 * D_RWKV) // LANES
    width = band.shape[1]
    tk = tq
    assert 2 * _bucket_saturation_distance() < tk
    return pl.pallas_call(
        functools.partial(_diff_attn_kernel, seq=seq, tq=tq, tk=tk, lambda_init=lambda_init),
        grid=(batch, npair, nq),
        in_specs=[pl.BlockSpec((tq, LANES), lambda b, p, i: (b * nq + i, col0 + p)),
                  pl.BlockSpec((seq, LANES), lambda b, p, i: (b, col0 + npair + p)),
                  pl.BlockSpec((seq, LANES), lambda b, p, i: (b, col0 + 2 * npair + p)),
                  pl.BlockSpec((2, width, tq), lambda b, p, i: (p, 0, 0)),
                  pl.BlockSpec((4, DIFF_QK_DIM), lambda b, p, i: (0, 0)),
                  pl.BlockSpec((1, LANES), lambda b, p, i: (0, 0))],
        out_specs=pl.BlockSpec((tq, LANES), lambda b, p, i: (b * nq + i, p)),
        out_shape=jax.ShapeDtypeStruct((batch * seq, D_DIFF), F32),
        scratch_shapes=[pltpu.VMEM((seq, LANES), BF16), pltpu.VMEM((LANES, seq), BF16),
                        pltpu.VMEM((4, HEAD_DIM + BF16_ROWS, tq), F32),
                        pltpu.VMEM((2, 4, tk, tq), F32)],
        compiler_params=_cparams(("parallel", "parallel", "arbitrary")),
        name="diff_attention",
    )(proj, proj, proj, band, lam_vecs, jnp.tile(subln_w, 2).reshape(1, LANES))


def _t5_bucket(rel):
    nb = REL_BUCKETS // 2
    max_exact = nb // 2
    n = jnp.abs(rel)
    nf = jnp.maximum(n, 1).astype(jnp.float32)
    large = max_exact + (jnp.log(nf / max_exact) / math.log(REL_MAX_DIST / max_exact)
                         * (nb - max_exact)).astype(jnp.int32)
    large = jnp.minimum(large, nb - 1)
    return jnp.where(rel > 0, nb, 0) + jnp.where(n < max_exact, n, large)


def _bucket_saturation_distance():
    nb = REL_BUCKETS // 2
    max_exact = nb // 2
    n = np.arange(1, 4 * REL_MAX_DIST)
    large = max_exact + (np.log(n / max_exact) / math.log(REL_MAX_DIST / max_exact)
                         * (nb - max_exact)).astype(np.int64)
    bucket = np.where(n < max_exact, n, np.minimum(large, nb - 1))
    return int(n[bucket < nb - 1].max())


def _bias_band(rel_bias, seq, tq):
    width = 2 * seq - tq
    period = 2 * seq - 1
    m = jnp.arange(period, dtype=jnp.int32)
    rel = jnp.where(m < width, m, m - period) - (seq - tq)
    vec = rel_bias.astype(F32)[_t5_bucket(rel)].T * LOG2E
    rows = jnp.tile(vec, (1, tq))[:, :tq * (period - 1)].reshape(-1, tq, period - 1)
    return jnp.transpose(rows[:, :, :width], (0, 2, 1))


def _outproj_kernel(x_ref, yr_ref, y0_ref, y1_ref, bonus_ref, gate_ref, yd_ref, lnw_ref, lnb_ref,
                    w_ref, o_ref):
    avg = _head_avg_matrix()
    acc = x_ref[...]
    acc = acc + _dot(yr_ref[...].astype(BF16), w_ref[0:D_RET, :])
    for p in range(RWKV_HEADS // 2):
        cols = slice(p * LANES, (p + 1) * LANES)
        y = _group_norm(y0_ref[:, cols] + y1_ref[:, cols], avg, lnw_ref[:, cols], lnb_ref[:, cols],
                        RWKV_GN_EPS)
        y = (y + bonus_ref[:, cols]) * gate_ref[:, cols]
        acc = acc + _dot(y.astype(BF16), w_ref[D_RET + p * LANES:D_RET + (p + 1) * LANES, :])
    acc = acc + _dot(yd_ref[...].astype(BF16), w_ref[D_RET + D_RWKV:, :])
    o_ref[...] = acc


def _outproj(x2, y_ret, y0, y1, bonus, gate, y_diff, ln_w, ln_b, w_bf16, tm=512):
    m, d = x2.shape
    tm = min(tm, m)
    row = lambda n: pl.BlockSpec((tm, n), lambda i: (i, 0))
    vec = pl.BlockSpec((1, D_RWKV), lambda i: (0, 0))
    return pl.pallas_call(
        _outproj_kernel,
        grid=(m // tm,),
        in_specs=[row(d), row(D_RET), row(D_RWKV), row(D_RWKV), row(D_RWKV), row(D_RWKV), row(D_DIFF),
                  vec, vec, pl.BlockSpec(w_bf16.shape, lambda i: (0, 0))],
        out_specs=row(d),
        out_shape=jax.ShapeDtypeStruct((m, d), F32),
        compiler_params=_cparams(("parallel",)),
        name="outproj",
    )(x2, y_ret, y0, y1, bonus, gate, y_diff, ln_w.reshape(1, D_RWKV), ln_b.reshape(1, D_RWKV), w_bf16)


def _ffn_up_kernel(x_ref, g_ref, wg_ref, wu_ref, o_ref, h_ref):
    @pl.when(pl.program_id(1) == 0)
    def _():
        x = x_ref[...]
        ms = jnp.mean(x * x, axis=-1, keepdims=True)
        h_ref[...] = (x * lax.rsqrt(ms + NORM_EPS) * g_ref[...]).astype(BF16)

    h = h_ref[...]
    gate = _dot(h, wg_ref[...])
    up = _dot(h, wu_ref[...])
    o_ref[...] = (gate * jax.nn.sigmoid(gate) * up).astype(BF16)


def _ffn_up(x2, g, wg_bf16, wu_bf16, tm=512, tf=1408):
    m, d = x2.shape
    tm = min(tm, m)
    f = wg_bf16.shape[1]
    return pl.pallas_call(
        _ffn_up_kernel,
        grid=(m // tm, f // tf),
        in_specs=[pl.BlockSpec((tm, d), lambda i, j: (i, 0)),
                  pl.BlockSpec((1, d), lambda i, j: (0, 0)),
                  pl.BlockSpec((d, tf), lambda i, j: (0, j)),
                  pl.BlockSpec((d, tf), lambda i, j: (0, j))],
        out_specs=pl.BlockSpec((tm, tf), lambda i, j: (i, j)),
        out_shape=jax.ShapeDtypeStruct((m, f), BF16),
        scratch_shapes=[pltpu.VMEM((tm, d), BF16)],
        compiler_params=_cparams(("parallel", "arbitrary")),
        name="ffn_up",
    )(x2, g.reshape(1, d), wg_bf16, wu_bf16)


def _ffn_down_kernel(x_ref, h_ref, w_ref, g_ref, o_ref, *, final_norm):
    y = x_ref[...] + _dot(h_ref[...], w_ref[...])
    if final_norm:
        ms = jnp.mean(y * y, axis=-1, keepdims=True)
        y = y * lax.rsqrt(ms + NORM_EPS) * g_ref[...]
    o_ref[...] = y


def _ffn_down(x2, hidden, w_bf16, g, final_norm, tm=512):
    m, d = x2.shape
    tm = min(tm, m)
    f = hidden.shape[1]
    return pl.pallas_call(
        functools.partial(_ffn_down_kernel, final_norm=final_norm),
        grid=(m // tm,),
        in_specs=[pl.BlockSpec((tm, d), lambda i: (i, 0)),
                  pl.BlockSpec((tm, f), lambda i: (i, 0)),
                  pl.BlockSpec((f, d), lambda i: (0, 0)),
                  pl.BlockSpec((1, d), lambda i: (0, 0))],
        out_specs=pl.BlockSpec((tm, d), lambda i: (i, 0)),
        out_shape=jax.ShapeDtypeStruct((m, d), F32),
        compiler_params=_cparams(("parallel",)),
        name="ffn_down",
    )(x2, hidden, w_bf16, g.reshape(1, d))


def _rope_tables(seq):
    half = HEAD_DIM // 2
    freqs = ROPE_BASE ** (-jnp.arange(half, dtype=F32) / half)
    ang = jnp.arange(seq, dtype=jnp.int32).astype(F32)[:, None] * freqs[None, :]
    cos = jnp.cos(ang)
    sin = jnp.sin(ang)
    cos_tab = jnp.tile(cos, (1, LANES // half))
    sin_tab = jnp.tile(jnp.concatenate([-sin, sin], axis=1), (1, LANES // HEAD_DIM))
    return cos_tab, sin_tab


def kernel(x, mix_norm_g, w_in, w_out, ret_gn_w, ret_gn_b, rwkv_mu, rwkv_w0, rwkv_w1, rwkv_w2, rwkv_a0, rwkv_a1, rwkv_a2, rwkv_g1, rwkv_g2, rwkv_k_k, rwkv_k_a, rwkv_r_k, rwkv_ln_w, rwkv_ln_b, diff_lambda, diff_subln_w, rel_bias, ffn_norm_g, w_gate, w_up, w_down, final_norm_g):
    batch, seq, d = x.shape
    depth = w_in.shape[0]
    cos_tab, sin_tab = _rope_tables(seq)
    band = _bias_band(rel_bias, seq, min(ATTN_TQ, seq))
    x2 = x.reshape(batch * seq, d)
    for l in range(depth):
        proj = _norm_matmul(x2, mix_norm_g[l], w_in[l].astype(BF16))
        y_ret = _retention(proj, cos_tab, sin_tab, ret_gn_w[l], ret_gn_b[l], batch, seq)
        feats = _rwkv_prep(proj, rwkv_mu[l], rwkv_w0[l], rwkv_w1[l], rwkv_w2[l], rwkv_a0[l],
                           rwkv_a1[l], rwkv_a2[l], rwkv_g1[l], rwkv_g2[l], rwkv_k_k[l],
                           rwkv_k_a[l], rwkv_r_k[l].reshape(-1), batch, seq)
        y0, y1 = _rwkv_scan(feats[:9], batch, seq)
        lambda_init = 0.8 - 0.6 * math.exp(-0.3 * l)
        y_diff = _diff_attention(proj, band, diff_lambda[l], diff_subln_w[l], lambda_init,
                                 batch, seq)
        x2 = _outproj(x2, y_ret, y0, y1, feats[10], feats[9], y_diff, rwkv_ln_w[l], rwkv_ln_b[l],
                      w_out[l].astype(BF16))
        hidden = _ffn_up(x2, ffn_norm_g[l], w_gate[l].astype(BF16), w_up[l].astype(BF16))
        x2 = _ffn_down(x2, hidden, w_down[l].astype(BF16), final_norm_g, l == depth - 1)
    return x2.reshape(batch, seq, d)
```

```python
import functools
import math

import numpy as np
import jax
import jax.numpy as jnp
from jax import lax
from jax.experimental import pallas as pl
from jax.experimental.pallas import tpu as pltpu

F32 = jnp.float32
BF16 = jnp.bfloat16

D_MODEL = 1024
HEAD_DIM = 64
LANES = 128
BF16_ROWS = 16
LOG2E = math.log2(math.e)
N_HEADS = D_MODEL // HEAD_DIM
RET_HEADS = (3 * N_HEADS) // 8
RWKV_HEADS = (3 * N_HEADS) // 8
DIFF_HEADS = N_HEADS - RET_HEADS - RWKV_HEADS
D_RET = RET_HEADS * HEAD_DIM
D_RWKV = RWKV_HEADS * HEAD_DIM
D_DIFF = DIFF_HEADS * HEAD_DIM
DIFF_QK_DIM = HEAD_DIM // 2
D_IN = 4 * D_RET + 4 * D_RWKV + 3 * D_DIFF
ROPE_BASE = 10000.0
D_FF = -(-8 * D_MODEL // (3 * 256)) * 256
REL_BUCKETS = 32
REL_MAX_DIST = 128
NORM_EPS = 1e-6
RET_GN_EPS = 1e-5
RWKV_GN_EPS = 64e-5

RET_CHUNK = 128
RWKV_CHUNK = 64
RWKV_GROUP = 4
ATTN_TQ = 256
VMEM_LIMIT = 56 * 1024 * 1024


def _cparams(sem):
    return pltpu.CompilerParams(dimension_semantics=sem, vmem_limit_bytes=VMEM_LIMIT)


def _dot(a, b):
    return jnp.dot(a, b, preferred_element_type=F32)


def _bdot(a, b):
    return _dot(a.astype(BF16), b.astype(BF16))


def _dot_nt(a, b):
    return lax.dot_general(a, b, (((1,), (1,)), ((), ())), preferred_element_type=F32)


def _dot_tn(a, b):
    return lax.dot_general(a, b, (((0,), (0,)), ((), ())), preferred_element_type=F32)


def _head_avg_matrix():
    r = lax.broadcasted_iota(jnp.int32, (LANES, LANES), 0) // HEAD_DIM
    c = lax.broadcasted_iota(jnp.int32, (LANES, LANES), 1) // HEAD_DIM
    return jnp.where(r == c, 1.0 / HEAD_DIM, 0.0).astype(BF16)


def _head_mean(x, avg):
    hi = x.astype(BF16)
    lo = (x - hi.astype(F32)).astype(BF16)
    return _dot(hi, avg) + _dot(lo, avg)


def _group_norm(y, avg, w, b, eps):
    yc = y - _head_mean(y, avg)
    var = _head_mean(yc * yc, avg)
    return yc * lax.rsqrt(var + eps) * w + b


def _norm_matmul_kernel(x_ref, g_ref, w_ref, o_ref):
    x = x_ref[...]
    ms = jnp.mean(x * x, axis=-1, keepdims=True)
    h = x * lax.rsqrt(ms + NORM_EPS) * g_ref[...]
    o_ref[...] = _dot(h.astype(BF16), w_ref[...])


def _norm_matmul(x2, g, w_bf16, tm=256):
    m, d = x2.shape
    n = w_bf16.shape[1]
    return pl.pallas_call(
        _norm_matmul_kernel,
        grid=(m // tm,),
        in_specs=[pl.BlockSpec((tm, d), lambda i: (i, 0)),
                  pl.BlockSpec((1, d), lambda i: (0, 0)),
                  pl.BlockSpec((d, n), lambda i: (0, 0))],
        out_specs=pl.BlockSpec((tm, n), lambda i: (i, 0)),
        out_shape=jax.ShapeDtypeStruct((m, n), F32),
        compiler_params=_cparams(("parallel",)),
        name="norm_inproj",
    )(x2, g.reshape(1, d), w_bf16)


def _retention_kernel(q_ref, k_ref, v_ref, g_ref, cos_ref, sin_ref, gnw_ref, gnb_ref, o_ref,
                      qs_ref, ks_ref, sf_ref, sb_ref, *, seq, chunk, log_gamma):
    nc = seq // chunk
    npair = RET_HEADS // 2
    lane = lax.broadcasted_iota(jnp.int32, (1, LANES), 1)
    first_half = (lane & (HEAD_DIM - 1)) < (HEAD_DIM // 2)
    head_masks = (lane < HEAD_DIM, lane >= HEAD_DIM)
    ri = lax.broadcasted_iota(jnp.int32, (LANES, LANES), 0)
    ci = lax.broadcasted_iota(jnp.int32, (LANES, LANES), 1)
    block_diag = (ri // HEAD_DIM) == (ci // HEAD_DIM)
    avg = _head_avg_matrix()
    ti = lax.broadcasted_iota(jnp.int32, (chunk, 2 * chunk), 0)
    tj = lax.broadcasted_iota(jnp.int32, (chunk, 2 * chunk), 1)
    dist = jnp.abs(ti - jnp.where(tj >= chunk, tj - chunk, tj)).astype(F32)
    pos = lax.broadcasted_iota(jnp.int32, (chunk, 1), 0).astype(F32)
    pairs = range(npair)
    cols = [slice(p * LANES, (p + 1) * LANES) for p in pairs]
    lg = [jnp.where(head_masks[0], log_gamma[2 * p], log_gamma[2 * p + 1]).astype(F32) for p in pairs]
    decay = [jnp.exp(dist * jnp.where(tj >= chunk, log_gamma[2 * p + 1], log_gamma[2 * p])) for p in pairs]
    xi_f = [jnp.exp((pos + 1.0) * x) for x in lg]
    xi_b = [jnp.exp((chunk - pos) * x) for x in lg]
    zeta_f = [jnp.exp((chunk - 1.0 - pos) * x) for x in lg]
    zeta_b = [jnp.exp(pos * x) for x in lg]
    dec_c = [jnp.exp(chunk * x) for x in lg]

    def rope(x, cos, sin):
        partner = jnp.where(first_half, pltpu.roll(x, LANES - HEAD_DIM // 2, 1),
                            pltpu.roll(x, HEAD_DIM // 2, 1))
        return x * cos + partner * sin

    def blockdiag(x):
        return jnp.concatenate([jnp.where(head_masks[0], x, 0.0), jnp.where(head_masks[1], x, 0.0)],
                               axis=0).astype(BF16)

    def pass1(n, carry):
        rows = pl.ds(pl.multiple_of(n * chunk, chunk), chunk)
        cos = cos_ref[rows, :]
        sin = sin_ref[rows, :]
        q = [rope(q_ref[rows, c], cos, sin) * (HEAD_DIM ** -0.5) for c in cols]
        k = [rope(k_ref[rows, c], cos, sin) for c in cols]
        kz = [jnp.concatenate([x * zf, x * zb], axis=1).astype(BF16) for x, zf, zb in zip(k, zeta_f, zeta_b)]
        kv = [_dot_tn(x, v_ref[rows, c].astype(BF16)) for x, c in zip(kz, cols)]
        for p in pairs:
            qs_ref[rows, cols[p]] = q[p]
            ks_ref[rows, cols[p]] = k[p]
            sf_ref[n, p] = jnp.where(block_diag, kv[p][:LANES], 0.0)
            sb_ref[n, p] = jnp.where(block_diag, kv[p][LANES:], 0.0)
        return carry

    lax.fori_loop(0, nc, pass1, 0)

    def scan_states(i, states):
        left, right = states
        new_left, new_right = [], []
        for p in pairs:
            kv = sf_ref[i, p]
            sf_ref[i, p] = left[p]
            new_left.append(left[p] * dec_c[p] + kv)
            kv = sb_ref[nc - 1 - i, p]
            sb_ref[nc - 1 - i, p] = right[p]
            new_right.append(right[p] * dec_c[p] + kv)
        return tuple(new_left), tuple(new_right)

    zeros = (jnp.zeros((LANES, LANES), F32),) * npair
    lax.fori_loop(0, nc, scan_states, (zeros, zeros))

    width = 2 if nc % 2 == 0 else 1

    def pass2(i, carry):
        work = [(i * width + w, p) for w in range(width) for p in pairs]
        rows = [pl.ds(pl.multiple_of(n * chunk, chunk), chunk) for n, _ in work]
        q = [qs_ref[r, cols[p]] for r, (_, p) in zip(rows, work)]
        k_bd = [blockdiag(ks_ref[r, cols[p]]) for r, (_, p) in zip(rows, work)]
        v_bd = [blockdiag(v_ref[r, cols[p]]) for r, (_, p) in zip(rows, work)]
        s = [(_dot_nt(x.astype(BF16), y) * decay[p]).astype(BF16) for x, y, (_, p) in zip(q, k_bd, work)]
        qx = [jnp.concatenate([x * xi_f[p], x * xi_b[p]], axis=1).astype(BF16) for x, (_, p) in zip(q, work)]
        st = [jnp.concatenate([sf_ref[n, p], sb_ref[n, p]], axis=0).astype(BF16) for n, p in work]
        o = [_dot(x, y) + _dot(z, w) for x, y, z, w in zip(s, v_bd, qx, st)]
        mean = [_head_mean(x, avg) for x in o]
        centred = [x - m for x, m in zip(o, mean)]
        var = [_head_mean(x * x, avg) for x in centred]
        for x, vr, r, (_, p) in zip(centred, var, rows, work):
            y = x * lax.rsqrt(vr + RET_GN_EPS) * gnw_ref[:, cols[p]] + gnb_ref[:, cols[p]]
            g = g_ref[r, cols[p]]
            o_ref[r, cols[p]] = g * jax.nn.sigmoid(g) * y
        return carry

    lax.fori_loop(0, nc // width, pass2, 0)


def _retention(proj, cos_tab, sin_tab, gn_w, gn_b, batch, seq):
    chunk = min(RET_CHUNK, seq)
    nc = seq // chunk
    log_gamma = tuple(float(np.log1p(-np.exp2(-5.0 - h))) for h in range(RET_HEADS))
    kern = functools.partial(_retention_kernel, seq=seq, chunk=chunk, log_gamma=log_gamma)
    col = lambda j: pl.BlockSpec((seq, D_RET), lambda b, j=j: (b, j))
    full = lambda shape: pl.BlockSpec(shape, lambda b: (0,) * len(shape))
    state = pltpu.VMEM((nc, RET_HEADS // 2, LANES, LANES), F32)
    return pl.pallas_call(
        kern,
        grid=(batch,),
        in_specs=[col(0), col(1), col(2), col(3), full((seq, LANES)), full((seq, LANES)),
                  full((1, D_RET)), full((1, D_RET))],
        out_specs=pl.BlockSpec((seq, D_RET), lambda b: (b, 0)),
        out_shape=jax.ShapeDtypeStruct((batch * seq, D_RET), F32),
        scratch_shapes=[pltpu.VMEM((seq, D_RET), F32), pltpu.VMEM((seq, D_RET), F32), state, state],
        compiler_params=_cparams(("parallel",)),
        name="retention",
    )(proj, proj, proj, proj, cos_tab, sin_tab, gn_w.reshape(1, D_RET), gn_b.reshape(1, D_RET))


def _rwkv_prep_kernel(r_ref, k_ref, v_ref, z_ref, rp_ref, kp_ref, vp_ref, zp_ref,
                      rn_ref, kn_ref, vn_ref, zn_ref, mu_ref, w0_ref, w1_ref, w2_ref,
                      a0_ref, a1_ref, a2_ref, g1_ref, g2_ref, kk_ref, ka_ref, rk_ref,
                      lw0_o, lw1_o, kd0_o, kd1_o, bb0_o, bb1_o, v_o, r_o, kk_o, gate_o, bonus_o,
                      *, tb):
    i = pl.program_id(1)
    has_prev = (i > 0).astype(F32)
    has_next = (i < pl.num_programs(1) - 1).astype(F32)
    row = lax.broadcasted_iota(jnp.int32, (tb, 1), 0)
    avg = _head_avg_matrix()

    def head_sum(x):
        return jnp.concatenate([_head_mean(x[:, c * LANES:(c + 1) * LANES], avg)
                                for c in range(D_RWKV // LANES)], axis=1) * HEAD_DIM

    def shifted(f_ref, p_ref, n_ref):
        f = f_ref[...]
        prev = jnp.where(row == 0, p_ref[7:8, :] * has_prev, pltpu.roll(f, 1, 0))
        nxt = jnp.where(row == tb - 1, n_ref[0:1, :] * has_next, pltpu.roll(f, tb - 1, 0))
        return f, prev - f, nxt - f

    def mix(parts, idx):
        f, dp, dn = parts
        return f + mu_ref[idx, 0:1, :] * dp + mu_ref[idx, 1:2, :] * dn

    xr = mix(shifted(r_ref, rp_ref, rn_ref), 0)
    xk = mix(shifted(k_ref, kp_ref, kn_ref), 1)
    xv = mix(shifted(v_ref, vp_ref, vn_ref), 2)
    zparts = shifted(z_ref, zp_ref, zn_ref)
    xw = mix(zparts, 3)
    xa = mix(zparts, 4)
    xg = mix(zparts, 5)

    gate = _bdot(jax.nn.sigmoid(_bdot(xg, g1_ref[...])), g2_ref[...])
    w_lo = _bdot(jnp.tanh(_bdot(xw, w1_ref[...])), w2_ref[...])
    a_lo = _bdot(_bdot(xa, a1_ref[...]), a2_ref[...])
    kk = xk * kk_ref[...]
    kk = kk / jnp.maximum(jnp.sqrt(head_sum(kk * kk)), 1e-12)

    ksum = jnp.zeros_like(xk)
    outs = ((lw0_o, kd0_o, bb0_o), (lw1_o, kd1_o, bb1_o))
    for d in range(2):
        cols = slice(d * D_RWKV, (d + 1) * D_RWKV)
        w_raw = w0_ref[d:d + 1, :] + w_lo[:, cols]
        log_w = -math.exp(-0.5) * jax.nn.sigmoid(w_raw)
        a = jax.nn.sigmoid(a0_ref[d:d + 1, :] + a_lo[:, cols])
        k_dir = xk * (1.0 + (a - 1.0) * ka_ref[...])
        ksum = ksum + k_dir
        lw_o, kd_o, bb_o = outs[d]
        lw_o[...] = log_w
        kd_o[...] = k_dir
        bb_o[...] = kk * a
    bonus = head_sum(xr * ksum * rk_ref[...])
    v_o[...] = xv
    r_o[...] = xr
    kk_o[...] = kk
    gate_o[...] = gate
    bonus_o[...] = bonus * xv


def _rwkv_prep(proj, mu, w0, w1, w2, a0, a1, a2, g1, g2, k_k, k_a, r_k, batch, seq):
    tb = min(256, seq)
    nt = seq // tb
    d = D_RWKV
    col0 = (4 * D_RET) // d
    main = lambda j: pl.BlockSpec((tb, d), lambda b, i, j=j: (b * nt + i, col0 + j))
    prev = lambda j: pl.BlockSpec(
        (8, d), lambda b, i, j=j: (b * (seq // 8) + jnp.maximum(i * (tb // 8) - 1, 0), col0 + j))
    nxt = lambda j: pl.BlockSpec(
        (8, d), lambda b, i, j=j: (b * (seq // 8) + jnp.minimum((i + 1) * (tb // 8), seq // 8 - 1),
                                   col0 + j))
    full = lambda shape: pl.BlockSpec(shape, lambda b, i: (0,) * len(shape))
    out_spec = pl.BlockSpec((tb, d), lambda b, i: (b * nt + i, 0))
    out_sd = jax.ShapeDtypeStruct((batch * seq, d), F32)
    row = lambda a: a.reshape(1, d)

    def both_dirs(first, second):
        r = first.shape[-1]
        blk = jnp.zeros((2 * r, 2 * d), F32)
        blk = blk.at[:r, :d].set(second[0]).at[r:, d:].set(second[1])
        return jnp.concatenate([first[0], first[1]], axis=1).astype(BF16), blk.astype(BF16)

    w1, w2 = both_dirs(w1, w2)
    a1, a2 = both_dirs(a1, a2)
    g1, g2 = g1.astype(BF16), g2.astype(BF16)
    return pl.pallas_call(
        functools.partial(_rwkv_prep_kernel, tb=tb),
        grid=(batch, nt),
        in_specs=[main(0), main(1), main(2), main(3), prev(0), prev(1), prev(2), prev(3),
                  nxt(0), nxt(1), nxt(2), nxt(3),
                  full(mu.shape), full(w0.shape), full(w1.shape), full(w2.shape),
                  full(a0.shape), full(a1.shape), full(a2.shape), full(g1.shape), full(g2.shape),
                  full((1, d)), full((1, d)), full((1, d))],
        out_specs=[out_spec] * 11,
        out_shape=[out_sd] * 11,
        compiler_params=_cparams(("parallel", "parallel")),
        name="rwkv_features",
    )(*([proj] * 12), mu, w0, w1, w2, a0, a1, a2, g1, g2, row(k_k), row(k_a), row(r_k))


def _split3(x):
    hi = x.astype(BF16)
    rest = x - hi.astype(F32)
    mid = rest.astype(BF16)
    lo = (rest - mid.astype(F32)).astype(BF16)
    return hi, mid, lo


def _rwkv_scan_kernel(lw0_ref, kd0_ref, bb0_ref, v0_ref, r0_ref, kk0_ref,
                      lw1_ref, kd1_ref, bb1_ref, v1_ref, r1_ref, kk1_ref,
                      y0_ref, y1_ref, s_ref, gr_ref, u0_ref, mr_ref, vbd_ref, ev_ref, dec_ref,
                      *, chunk, group):
    npair = RWKV_HEADS // 2
    assert 2 * chunk == LANES

    @pl.when(pl.program_id(1) == 0)
    def _():
        s_ref[...] = jnp.zeros_like(s_ref)

    lane = lax.broadcasted_iota(jnp.int32, (1, LANES), 1)
    halves = (lane < chunk, lane >= chunk)
    ri = lax.broadcasted_iota(jnp.int32, (LANES, LANES), 0)
    ci = lax.broadcasted_iota(jnp.int32, (LANES, LANES), 1)
    same_head = (ri // HEAD_DIM) == (ci // HEAD_DIM)
    ti = lax.broadcasted_iota(jnp.int32, (chunk, LANES), 0)
    tj = lax.broadcasted_iota(jnp.int32, (chunk, LANES), 1) % chunk
    eye = (ti == tj).astype(F32)
    strict = (tj < ti, tj > ti)
    incl = (tj <= ti, tj >= ti)
    si = lax.broadcasted_iota(jnp.int32, (chunk, chunk), 0)
    sj = lax.broadcasted_iota(jnp.int32, (chunk, chunk), 1)
    cum_mat = ((sj <= si).astype(BF16), (sj >= si).astype(BF16))
    ins = ((lw0_ref, kd0_ref, bb0_ref, v0_ref, r0_ref, kk0_ref),
           (lw1_ref, kd1_ref, bb1_ref, v1_ref, r1_ref, kk1_ref))
    outs = (y0_ref, y1_ref)
    items = [(c, d, p) for c in range(group) for p in range(npair) for d in range(2)]
    rows = lambda c, d: slice((c if d == 0 else group - 1 - c) * chunk,
                              ((c if d == 0 else group - 1 - c) + 1) * chunk)

    def blockdiag(x):
        return jnp.concatenate([jnp.where(halves[0], x, 0.0), jnp.where(halves[1], x, 0.0)],
                               axis=0).astype(BF16)

    feats = {}
    for c in range(group):
        for d in range(2):
            lw_ref, kd_ref, bb_ref, v_ref, r_ref, kk_ref = ins[d]
            rs = rows(c, d)
            lw = lw_ref[rs, :]
            cum_in = sum(_dot(cum_mat[d], part) for part in _split3(lw))
            total = jnp.sum(lw, axis=0, keepdims=True)
            kd = kd_ref[rs, :]
            bb = bb_ref[rs, :]
            inv_p = jnp.exp(-cum_in)
            to_end = jnp.exp(total - cum_in)
            feats[c, d] = dict(a=-kk_ref[rs, :] * jnp.exp(cum_in - lw), r=r_ref[rs, :] * jnp.exp(cum_in),
                               b=bb * inv_p, k=kd * inv_p, v=v_ref[rs, :],
                               bend=bb * to_end, kend=kd * to_end, dec=jnp.exp(total))

    def per_item(name):
        return [feats[c, d][name][:, p * LANES:(p + 1) * LANES] for c, d, p in items]

    a = per_item("a")
    v = per_item("v")
    v_bd = [blockdiag(x) for x in v]
    ar = [jnp.concatenate(pair, axis=0).astype(BF16) for pair in zip(a, per_item("r"))]
    bk = [jnp.concatenate([blockdiag(x), blockdiag(y)], axis=0)
          for x, y in zip(per_item("b"), per_item("k"))]
    gram = [_dot_nt(x, y) for x, y in zip(ar, bk)]
    l_ab = [jnp.where(strict[d], g[:chunk, :LANES], 0.0) for (_, d, _), g in zip(items, gram)]
    l_ak = [jnp.where(strict[d], g[:chunk, LANES:], 0.0).astype(BF16) for (_, d, _), g in zip(items, gram)]
    for n, ((_, d, _), g) in enumerate(zip(items, gram)):
        mr_ref[n] = jnp.where(jnp.concatenate([incl[d], incl[d]], axis=1), g[chunk:, :], 0.0).astype(BF16)
    lv = [_dot(x, y) for x, y in zip(l_ak, v_bd)]

    power = [_dot(x.astype(BF16), blockdiag(x)) for x in l_ab]
    inv = [eye + x for x in l_ab]
    rounds = int(math.log2(chunk)) - 1
    for j in range(rounds):
        pd = [blockdiag(x) for x in power]
        if j == rounds - 1:
            inv = [t + _dot(t.astype(BF16), x) for t, x in zip(inv, pd)]
        else:
            both = [_dot(jnp.concatenate([x, t], axis=0).astype(BF16), y)
                    for x, t, y in zip(power, inv, pd)]
            power = [x[:chunk] for x in both]
            inv = [t + x[chunk:] for t, x in zip(inv, both)]
    ta = [_dot(t.astype(BF16), jnp.concatenate([blockdiag(x), blockdiag(y)], axis=1))
          for t, x, y in zip(inv, a, lv)]
    for n, (c, d, p) in enumerate(items):
        cols = slice(p * LANES, (p + 1) * LANES)
        gr_ref[n] = jnp.concatenate([ta[n][:, :LANES].astype(BF16), ar[n][chunk:]], axis=0)
        u0_ref[n] = ta[n][:, LANES:]
        vbd_ref[n] = v_bd[n]
        ev_ref[n] = jnp.concatenate([feats[c, d]["bend"][:, cols], feats[c, d]["kend"][:, cols],
                                     v[n]], axis=0).astype(BF16)
        dec_ref[n] = jnp.broadcast_to(feats[c, d]["dec"][:, cols], (8, LANES))

    per_chunk = 2 * npair
    for c in range(group):
        ns = range(c * per_chunk, (c + 1) * per_chunk)
        states = [s_ref[j] for j in range(per_chunk)]
        prod = [_dot_nt(gr_ref[n], s.astype(BF16)) for n, s in zip(ns, states)]
        u = [x[:chunk] + u0_ref[n] for n, x in zip(ns, prod)]
        y = [x[chunk:] + _dot(mr_ref[n], jnp.concatenate([blockdiag(w), vbd_ref[n]], axis=0))
             for n, x, w in zip(ns, prod, u)]
        upd = [_dot_tn(jnp.concatenate([w.astype(BF16), ev_ref[n, 2 * chunk:, :]], axis=0),
                       ev_ref[n, :2 * chunk, :]) for n, w in zip(ns, u)]
        for j, n in enumerate(ns):
            _, d, p = items[n]
            outs[d][rows(c, d), p * LANES:(p + 1) * LANES] = y[j]
            s_ref[j] = states[j] * dec_ref[n, 0:1, :] + jnp.where(same_head, upd[j], 0.0)


def _rwkv_scan(feats, batch, seq):
    lw0, lw1, kd0, kd1, bb0, bb1, xv, xr, kk = feats
    chunk = min(RWKV_CHUNK, seq)
    group = min(RWKV_GROUP, seq // chunk)
    tb = chunk * group
    nt = seq // tb
    n_items = group * RWKV_HEADS
    fwd = pl.BlockSpec((tb, D_RWKV), lambda b, i: (b * nt + i, 0))
    bwd = pl.BlockSpec((tb, D_RWKV), lambda b, i: (b * nt + nt - 1 - i, 0))
    out_sd = jax.ShapeDtypeStruct((batch * seq, D_RWKV), F32)
    return pl.pallas_call(
        functools.partial(_rwkv_scan_kernel, chunk=chunk, group=group),
        grid=(batch, nt),
        in_specs=[fwd] * 6 + [bwd] * 6,
        out_specs=[fwd, bwd],
        out_shape=[out_sd, out_sd],
        scratch_shapes=[pltpu.VMEM((RWKV_HEADS, LANES, LANES), F32),
                        pltpu.VMEM((n_items, 2 * chunk, LANES), BF16),
                        pltpu.VMEM((n_items, chunk, LANES), F32),
                        pltpu.VMEM((n_items, chunk, 4 * chunk), BF16),
                        pltpu.VMEM((n_items, 2 * chunk, LANES), BF16),
                        pltpu.VMEM((n_items, 3 * chunk, LANES), BF16),
                        pltpu.VMEM((n_items, 8, LANES), F32)],
        compiler_params=_cparams(("parallel", "arbitrary")),
        name="rwkv_scan",
    )(lw0, kd0, bb0, xv, xr, kk, lw1, kd1, bb1, xv, xr, kk)


def _diff_attn_kernel(q_ref, k_ref, v_ref, band_ref, lam_ref, subw_ref, o_ref, kb_ref, vt_ref,
                      acc_ref, s_ref, *, seq, tq, tk, lambda_init):
    i = pl.program_id(2)

    @pl.when(i == 0)
    def _():
        kb_ref[...] = k_ref[...].astype(BF16)
        vt_ref[...] = jnp.transpose(v_ref[...]).astype(BF16)

    lane = lax.broadcasted_iota(jnp.int32, (1, LANES), 1)
    lv = lam_ref[...]
    lam = (jnp.exp(jnp.sum(lv[0:1] * lv[1:2], axis=1, keepdims=True))
           - jnp.exp(jnp.sum(lv[2:3] * lv[3:4], axis=1, keepdims=True)) + lambda_init)
    q = q_ref[...] * (DIFF_QK_DIM ** -0.5 * LOG2E)
    off = (seq - tq) - i * tq
    ones_rows = jnp.ones((BF16_ROWS, tk), BF16)
    n_maps = 4
    qms = [jnp.where((lane >= m * DIFF_QK_DIM) & (lane < (m + 1) * DIFF_QK_DIM), q, 0.0).astype(BF16)
           for m in range(n_maps)]
    n_tiles = seq // tk
    acc_ref[...] = jnp.zeros_like(acc_ref)

    def tile_start(j):
        t = i + j
        return pl.multiple_of(jnp.where(t >= n_tiles, t - n_tiles, t) * tk, tk)

    def stage_scores(j, m):
        s_ref[j % 2, m] = _dot_nt(kb_ref[pl.ds(tile_start(j), tk), :], qms[m])

    m_run = [jnp.full((1, tq), -1e30, F32)] * n_maps
    for m in range(n_maps):
        stage_scores(0, m)
    for j in range(n_tiles):
        slot = j % 2
        start = tile_start(j)
        far = 2 <= j <= n_tiles - 2
        rows = BF16_ROWS // 2 if far else tk
        bias = [band_ref[hh, pl.ds(pl.multiple_of(off + start, rows), rows), :] for hh in range(2)]
        vt = [jnp.concatenate([vt_ref[hh * HEAD_DIM:(hh + 1) * HEAD_DIM, pl.ds(start, tk)],
                               ones_rows], axis=0) for hh in range(2)]
        for m in range(n_maps):
            if j + 1 < n_tiles:
                stage_scores(j + 1, m)
            if far:
                s = s_ref[slot, m]
                const = bias[m // 2][0:1, :]
                m_new = jnp.maximum(m_run[m], jnp.max(s, axis=0, keepdims=True) + const)
                e = jnp.exp2(s - (m_new - const)).astype(BF16)
            else:
                s = s_ref[slot, m] + bias[m // 2]
                m_new = jnp.maximum(m_run[m], jnp.max(s, axis=0, keepdims=True))
                e = jnp.exp2(s - m_new).astype(BF16)
            pv = _dot(vt[m // 2], e)
            acc_ref[m] = acc_ref[m] * jnp.exp2(m_run[m] - m_new) + pv
            m_run[m] = m_new
    parts = [acc_ref[m, :HEAD_DIM, :] / acc_ref[m, HEAD_DIM:HEAD_DIM + 1, :] for m in range(n_maps)]
    heads = []
    for hh in range(2):
        head = parts[2 * hh] - lam * parts[2 * hh + 1]
        ms = jnp.mean(head * head, axis=0, keepdims=True)
        heads.append(head * lax.rsqrt(ms + NORM_EPS))
    out = jnp.transpose(jnp.concatenate(heads, axis=0))
    o_ref[...] = out * subw_ref[...] * (1.0 - lambda_init)


def _diff_attention(proj, band, lam_vecs, subln_w, lambda_init, batch, seq):
    tq = min(ATTN_TQ, seq)
    nq = seq // tq
    npair = DIFF_HEADS // 2
    col0 = (4 * D_RET + 4 * D_RWKV) // LANES
    width = band.shape[1]
    tk = tq
    assert 2 * _bucket_saturation_distance() < tk
    return pl.pallas_call(
        functools.partial(_diff_attn_kernel, seq=seq, tq=tq, tk=tk, lambda_init=lambda_init),
        grid=(batch, npair, nq),
        in_specs=[pl.BlockSpec((tq, LANES), lambda b, p, i: (b * nq + i, col0 + p)),
                  pl.BlockSpec((seq, LANES), lambda b, p, i: (b, col0 + npair + p)),
                  pl.BlockSpec((seq, LANES), lambda b, p, i: (b, col0 + 2 * npair + p)),
                  pl.BlockSpec((2, width, tq), lambda b, p, i: (p, 0, 0)),
                  pl.BlockSpec((4, DIFF_QK_DIM), lambda b, p, i: (0, 0)),
                  pl.BlockSpec((1, LANES), lambda b, p, i: (0, 0))],
        out_specs=pl.BlockSpec((tq, LANES), lambda b, p, i: (b * nq + i, p)),
        out_shape=jax.ShapeDtypeStruct((batch * seq, D_DIFF), F32),
        scratch_shapes=[pltpu.VMEM((seq, LANES), BF16), pltpu.VMEM((LANES, seq), BF16),
                        pltpu.VMEM((4, HEAD_DIM + BF16_ROWS, tq), F32),
                        pltpu.VMEM((2, 4, tk, tq), F32)],
        compiler_params=_cparams(("parallel", "parallel", "arbitrary")),
        name="diff_attention",
    )(proj, proj, proj, band, lam_vecs, jnp.tile(subln_w, 2).reshape(1, LANES))


def _t5_bucket(rel):
    nb = REL_BUCKETS // 2
    max_exact = nb // 2
    n = jnp.abs(rel)
    nf = jnp.maximum(n, 1).astype(jnp.float32)
    large = max_exact + (jnp.log(nf / max_exact) / math.log(REL_MAX_DIST / max_exact)
                         * (nb - max_exact)).astype(jnp.int32)
    large = jnp.minimum(large, nb - 1)
    return jnp.where(rel > 0, nb, 0) + jnp.where(n < max_exact, n, large)


def _bucket_saturation_distance():
    nb = REL_BUCKETS // 2
    max_exact = nb // 2
    n = np.arange(1, 4 * REL_MAX_DIST)
    large = max_exact + (np.log(n / max_exact) / math.log(REL_MAX_DIST / max_exact)
                         * (nb - max_exact)).astype(np.int64)
    bucket = np.where(n < max_exact, n, np.minimum(large, nb - 1))
    return int(n[bucket < nb - 1].max())


def _bias_band(rel_bias, seq, tq):
    width = 2 * seq - tq
    period = 2 * seq - 1
    m = jnp.arange(period, dtype=jnp.int32)
    rel = jnp.where(m < width, m, m - period) - (seq - tq)
    vec = rel_bias.astype(F32)[_t5_bucket(rel)].T * LOG2E
    rows = jnp.tile(vec, (1, tq))[:, :tq * (period - 1)].reshape(-1, tq, period - 1)
    return jnp.transpose(rows[:, :, :width], (0, 2, 1))


def _outproj_kernel(x_ref, yr_ref, y0_ref, y1_ref, bonus_ref, gate_ref, yd_ref, lnw_ref, lnb_ref,
                    w_ref, o_ref):
    avg = _head_avg_matrix()
    mixed = [yr_ref[...].astype(BF16)]
    for p in range(RWKV_HEADS // 2):
        cols = slice(p * LANES, (p + 1) * LANES)
        y = _group_norm(y0_ref[:, cols] + y1_ref[:, cols], avg, lnw_ref[:, cols], lnb_ref[:, cols],
                        RWKV_GN_EPS)
        mixed.append(((y + bonus_ref[:, cols]) * gate_ref[:, cols]).astype(BF16))
    mixed.append(yd_ref[...].astype(BF16))
    o_ref[...] = x_ref[...] + _dot(jnp.concatenate(mixed, axis=1), w_ref[...])


def _outproj(x2, y_ret, y0, y1, bonus, gate, y_diff, ln_w, ln_b, w_bf16, tm=512):
    m, d = x2.shape
    tm = min(tm, m)
    row = lambda n: pl.BlockSpec((tm, n), lambda i: (i, 0))
    vec = pl.BlockSpec((1, D_RWKV), lambda i: (0, 0))
    return pl.pallas_call(
        _outproj_kernel,
        grid=(m // tm,),
        in_specs=[row(d), row(D_RET), row(D_RWKV), row(D_RWKV), row(D_RWKV), row(D_RWKV), row(D_DIFF),
                  vec, vec, pl.BlockSpec(w_bf16.shape, lambda i: (0, 0))],
        out_specs=row(d),
        out_shape=jax.ShapeDtypeStruct((m, d), F32),
        compiler_params=_cparams(("parallel",)),
        name="outproj",
    )(x2, y_ret, y0, y1, bonus, gate, y_diff, ln_w.reshape(1, D_RWKV), ln_b.reshape(1, D_RWKV), w_bf16)


def _ffn_up_kernel(x_ref, g_ref, wg_ref, wu_ref, o_ref, *, tf):
    x = x_ref[...]
    ms = jnp.mean(x * x, axis=-1, keepdims=True)
    h = (x * lax.rsqrt(ms + NORM_EPS) * g_ref[...]).astype(BF16)
    for j in range(wg_ref.shape[1] // tf):
        cols = slice(j * tf, (j + 1) * tf)
        gate = _dot(h, wg_ref[:, cols])
        up = _dot(h, wu_ref[:, cols])
        o_ref[:, cols] = (gate * jax.nn.sigmoid(gate) * up).astype(BF16)


def _ffn_up(x2, g, wg_bf16, wu_bf16, tm=512, tf=256):
    m, d = x2.shape
    tm = min(tm, m)
    f = wg_bf16.shape[1]
    return pl.pallas_call(
        functools.partial(_ffn_up_kernel, tf=tf),
        grid=(m // tm,),
        in_specs=[pl.BlockSpec((tm, d), lambda i: (i, 0)),
                  pl.BlockSpec((1, d), lambda i: (0, 0)),
                  pl.BlockSpec((d, f), lambda i: (0, 0)),
                  pl.BlockSpec((d, f), lambda i: (0, 0))],
        out_specs=pl.BlockSpec((tm, f), lambda i: (i, 0)),
        out_shape=jax.ShapeDtypeStruct((m, f), BF16),
        compiler_params=_cparams(("parallel",)),
        name="ffn_up",
    )(x2, g.reshape(1, d), wg_bf16, wu_bf16)


def _ffn_down_kernel(x_ref, h_ref, w_ref, g_ref, o_ref, *, final_norm):
    y = x_ref[...] + _dot(h_ref[...], w_ref[...])
    if final_norm:
        ms = jnp.mean(y * y, axis=-1, keepdims=True)
        y = y * lax.rsqrt(ms + NORM_EPS) * g_ref[...]
    o_ref[...] = y


def _ffn_down(x2, hidden, w_bf16, g, final_norm, tm=512):
    m, d = x2.shape
    tm = min(tm, m)
    f = hidden.shape[1]
    return pl.pallas_call(
        functools.partial(_ffn_down_kernel, final_norm=final_norm),
        grid=(m // tm,),
        in_specs=[pl.BlockSpec((tm, d), lambda i: (i, 0)),
                  pl.BlockSpec((tm, f), lambda i: (i, 0)),
                  pl.BlockSpec((f, d), lambda i: (0, 0)),
                  pl.BlockSpec((1, d), lambda i: (0, 0))],
        out_specs=pl.BlockSpec((tm, d), lambda i: (i, 0)),
        out_shape=jax.ShapeDtypeStruct((m, d), F32),
        compiler_params=_cparams(("parallel",)),
        name="ffn_down",
    )(x2, hidden, w_bf16, g.reshape(1, d))


def _rope_tables(seq):
    half = HEAD_DIM // 2
    freqs = ROPE_BASE ** (-jnp.arange(half, dtype=F32) / half)
    ang = jnp.arange(seq, dtype=jnp.int32).astype(F32)[:, None] * freqs[None, :]
    cos = jnp.cos(ang)
    sin = jnp.sin(ang)
    cos_tab = jnp.tile(cos, (1, LANES // half))
    sin_tab = jnp.tile(jnp.concatenate([-sin, sin], axis=1), (1, LANES // HEAD_DIM))
    return cos_tab, sin_tab


def kernel(x, mix_norm_g, w_in, w_out, ret_gn_w, ret_gn_b, rwkv_mu, rwkv_w0, rwkv_w1, rwkv_w2, rwkv_a0, rwkv_a1, rwkv_a2, rwkv_g1, rwkv_g2, rwkv_k_k, rwkv_k_a, rwkv_r_k, rwkv_ln_w, rwkv_ln_b, diff_lambda, diff_subln_w, rel_bias, ffn_norm_g, w_gate, w_up, w_down, final_norm_g):
    batch, seq, d = x.shape
    depth = w_in.shape[0]
    cos_tab, sin_tab = _rope_tables(seq)
    band = _bias_band(rel_bias, seq, min(ATTN_TQ, seq))
    x2 = x.reshape(batch * seq, d)
    for l in range(depth):
        proj = _norm_matmul(x2, mix_norm_g[l], w_in[l].astype(BF16))
        y_ret = _retention(proj, cos_tab, sin_tab, ret_gn_w[l], ret_gn_b[l], batch, seq)
        feats = _rwkv_prep(proj, rwkv_mu[l], rwkv_w0[l], rwkv_w1[l], rwkv_w2[l], rwkv_a0[l],
                           rwkv_a1[l], rwkv_a2[l], rwkv_g1[l], rwkv_g2[l], rwkv_k_k[l],
                           rwkv_k_a[l], rwkv_r_k[l].reshape(-1), batch, seq)
        y0, y1 = _rwkv_scan(feats[:9], batch, seq)
        lambda_init = 0.8 - 0.6 * math.exp(-0.3 * l)
        y_diff = _diff_attention(proj, band, diff_lambda[l], diff_subln_w[l], lambda_init,
                                 batch, seq)
        x2 = _outproj(x2, y_ret, y0, y1, feats[10], feats[9], y_diff, rwkv_ln_w[l], rwkv_ln_b[l],
                      w_out[l].astype(BF16))
        hidden = _ffn_up(x2, ffn_norm_g[l], w_gate[l].astype(BF16), w_up[l].astype(BF16))
        x2 = _ffn_down(x2, hidden, w_down[l].astype(BF16), final_norm_g, l == depth - 1)
    return x2.reshape(batch, seq, d)
```

```python
import functools
import math

import numpy as np
import jax
import jax.numpy as jnp
from jax import lax
from jax.experimental import pallas as pl
from jax.experimental.pallas import tpu as pltpu

F32 = jnp.float32
BF16 = jnp.bfloat16

D_MODEL = 1024
HEAD_DIM = 64
LANES = 128
BF16_ROWS = 16
LOG2E = math.log2(math.e)
N_HEADS = D_MODEL // HEAD_DIM
RET_HEADS = (3 * N_HEADS) // 8
RWKV_HEADS = (3 * N_HEADS) // 8
DIFF_HEADS = N_HEADS - RET_HEADS - RWKV_HEADS
D_RET = RET_HEADS * HEAD_DIM
D_RWKV = RWKV_HEADS * HEAD_DIM
D_DIFF = DIFF_HEADS * HEAD_DIM
DIFF_QK_DIM = HEAD_DIM // 2
D_IN = 4 * D_RET + 4 * D_RWKV + 3 * D_DIFF
ROPE_BASE = 10000.0
D_FF = -(-8 * D_MODEL // (3 * 256)) * 256
REL_BUCKETS = 32
REL_MAX_DIST = 128
NORM_EPS = 1e-6
RET_GN_EPS = 1e-5
RWKV_GN_EPS = 64e-5

RET_CHUNK = 128
RWKV_CHUNK = 64
RWKV_GROUP = 4
ATTN_TQ = 256
VMEM_LIMIT = 56 * 1024 * 1024


def _cparams(sem):
    return pltpu.CompilerParams(dimension_semantics=sem, vmem_limit_bytes=VMEM_LIMIT)


def _dot(a, b):
    return jnp.dot(a, b, preferred_element_type=F32)


def _bdot(a, b):
    return _dot(a.astype(BF16), b.astype(BF16))


def _dot_nt(a, b):
    return lax.dot_general(a, b, (((1,), (1,)), ((), ())), preferred_element_type=F32)


def _dot_tn(a, b):
    return lax.dot_general(a, b, (((0,), (0,)), ((), ())), preferred_element_type=F32)


def _head_avg_matrix():
    r = lax.broadcasted_iota(jnp.int32, (LANES, LANES), 0) // HEAD_DIM
    c = lax.broadcasted_iota(jnp.int32, (LANES, LANES), 1) // HEAD_DIM
    return jnp.where(r == c, 1.0 / HEAD_DIM, 0.0).astype(BF16)


def _head_mean(x, avg):
    hi = x.astype(BF16)
    lo = (x - hi.astype(F32)).astype(BF16)
    return _dot(hi, avg) + _dot(lo, avg)


def _group_norm(y, avg, w, b, eps):
    yc = y - _head_mean(y, avg)
    var = _head_mean(yc * yc, avg)
    return yc * lax.rsqrt(var + eps) * w + b


def _norm_matmul_kernel(x_ref, g_ref, w_ref, o_ref):
    x = x_ref[...]
    ms = jnp.mean(x * x, axis=-1, keepdims=True)
    h = x * lax.rsqrt(ms + NORM_EPS) * g_ref[...]
    o_ref[...] = _dot(h.astype(BF16), w_ref[...])


def _norm_matmul(x2, g, w_bf16, tm=256):
    m, d = x2.shape
    n = w_bf16.shape[1]
    return pl.pallas_call(
        _norm_matmul_kernel,
        grid=(m // tm,),
        in_specs=[pl.BlockSpec((tm, d), lambda i: (i, 0)),
                  pl.BlockSpec((1, d), lambda i: (0, 0)),
                  pl.BlockSpec((d, n), lambda i: (0, 0))],
        out_specs=pl.BlockSpec((tm, n), lambda i: (i, 0)),
        out_shape=jax.ShapeDtypeStruct((m, n), F32),
        compiler_params=_cparams(("parallel",)),
        name="norm_inproj",
    )(x2, g.reshape(1, d), w_bf16)


def _retention_kernel(q_ref, k_ref, v_ref, g_ref, cos_ref, sin_ref, gnw_ref, gnb_ref, o_ref,
                      qs_ref, ks_ref, sf_ref, sb_ref, *, seq, chunk, log_gamma):
    nc = seq // chunk
    npair = RET_HEADS // 2
    lane = lax.broadcasted_iota(jnp.int32, (1, LANES), 1)
    first_half = (lane & (HEAD_DIM - 1)) < (HEAD_DIM // 2)
    head_masks = (lane < HEAD_DIM, lane >= HEAD_DIM)
    ri = lax.broadcasted_iota(jnp.int32, (LANES, LANES), 0)
    ci = lax.broadcasted_iota(jnp.int32, (LANES, LANES), 1)
    block_diag = (ri // HEAD_DIM) == (ci // HEAD_DIM)
    avg = _head_avg_matrix()
    ti = lax.broadcasted_iota(jnp.int32, (chunk, 2 * chunk), 0)
    tj = lax.broadcasted_iota(jnp.int32, (chunk, 2 * chunk), 1)
    dist = jnp.abs(ti - jnp.where(tj >= chunk, tj - chunk, tj)).astype(F32)
    pos = lax.broadcasted_iota(jnp.int32, (chunk, 1), 0).astype(F32)
    pairs = range(npair)
    cols = [slice(p * LANES, (p + 1) * LANES) for p in pairs]
    lg = [jnp.where(head_masks[0], log_gamma[2 * p], log_gamma[2 * p + 1]).astype(F32) for p in pairs]
    decay = [jnp.exp(dist * jnp.where(tj >= chunk, log_gamma[2 * p + 1], log_gamma[2 * p])) for p in pairs]
    xi_f = [jnp.exp((pos + 1.0) * x) for x in lg]
    xi_b = [jnp.exp((chunk - pos) * x) for x in lg]
    zeta_f = [jnp.exp((chunk - 1.0 - pos) * x) for x in lg]
    zeta_b = [jnp.exp(pos * x) for x in lg]
    dec_c = [jnp.exp(chunk * x) for x in lg]

    def rope(x, cos, sin):
        partner = jnp.where(first_half, pltpu.roll(x, LANES - HEAD_DIM // 2, 1),
                            pltpu.roll(x, HEAD_DIM // 2, 1))
        return x * cos + partner * sin

    def blockdiag(x):
        return jnp.concatenate([jnp.where(head_masks[0], x, 0.0), jnp.where(head_masks[1], x, 0.0)],
                               axis=0).astype(BF16)

    def pass1(n, carry):
        rows = pl.ds(pl.multiple_of(n * chunk, chunk), chunk)
        cos = cos_ref[rows, :]
        sin = sin_ref[rows, :]
        q = [rope(q_ref[rows, c], cos, sin) * (HEAD_DIM ** -0.5) for c in cols]
        k = [rope(k_ref[rows, c], cos, sin) for c in cols]
        kz = [jnp.concatenate([x * zf, x * zb], axis=1).astype(BF16) for x, zf, zb in zip(k, zeta_f, zeta_b)]
        kv = [_dot_tn(x, v_ref[rows, c].astype(BF16)) for x, c in zip(kz, cols)]
        for p in pairs:
            qs_ref[rows, cols[p]] = q[p]
            ks_ref[rows, cols[p]] = k[p]
            sf_ref[n, p] = jnp.where(block_diag, kv[p][:LANES], 0.0)
            sb_ref[n, p] = jnp.where(block_diag, kv[p][LANES:], 0.0)
        return carry

    lax.fori_loop(0, nc, pass1, 0)

    def scan_states(i, states):
        left, right = states
        new_left, new_right = [], []
        for p in pairs:
            kv = sf_ref[i, p]
            sf_ref[i, p] = left[p]
            new_left.append(left[p] * dec_c[p] + kv)
            kv = sb_ref[nc - 1 - i, p]
            sb_ref[nc - 1 - i, p] = right[p]
            new_right.append(right[p] * dec_c[p] + kv)
        return tuple(new_left), tuple(new_right)

    zeros = (jnp.zeros((LANES, LANES), F32),) * npair
    lax.fori_loop(0, nc, scan_states, (zeros, zeros))

    width = 2 if nc % 2 == 0 else 1

    def pass2(i, carry):
        work = [(i * width + w, p) for w in range(width) for p in pairs]
        rows = [pl.ds(pl.multiple_of(n * chunk, chunk), chunk) for n, _ in work]
        q = [qs_ref[r, cols[p]] for r, (_, p) in zip(rows, work)]
        k_bd = [blockdiag(ks_ref[r, cols[p]]) for r, (_, p) in zip(rows, work)]
        v_bd = [blockdiag(v_ref[r, cols[p]]) for r, (_, p) in zip(rows, work)]
        s = [(_dot_nt(x.astype(BF16), y) * decay[p]).astype(BF16) for x, y, (_, p) in zip(q, k_bd, work)]
        qx = [jnp.concatenate([x * xi_f[p], x * xi_b[p]], axis=1).astype(BF16) for x, (_, p) in zip(q, work)]
        st = [jnp.concatenate([sf_ref[n, p], sb_ref[n, p]], axis=0).astype(BF16) for n, p in work]
        o = [_dot(x, y) + _dot(z, w) for x, y, z, w in zip(s, v_bd, qx, st)]
        mean = [_head_mean(x, avg) for x in o]
        centred = [x - m for x, m in zip(o, mean)]
        var = [_head_mean(x * x, avg) for x in centred]
        for x, vr, r, (_, p) in zip(centred, var, rows, work):
            y = x * lax.rsqrt(vr + RET_GN_EPS) * gnw_ref[:, cols[p]] + gnb_ref[:, cols[p]]
            g = g_ref[r, cols[p]]
            o_ref[r, cols[p]] = g * jax.nn.sigmoid(g) * y
        return carry

    lax.fori_loop(0, nc // width, pass2, 0)


def _retention(proj, cos_tab, sin_tab, gn_w, gn_b, batch, seq):
    chunk = min(RET_CHUNK, seq)
    nc = seq // chunk
    log_gamma = tuple(float(np.log1p(-np.exp2(-5.0 - h))) for h in range(RET_HEADS))
    kern = functools.partial(_retention_kernel, seq=seq, chunk=chunk, log_gamma=log_gamma)
    col = lambda j: pl.BlockSpec((seq, D_RET), lambda b, j=j: (b, j))
    full = lambda shape: pl.BlockSpec(shape, lambda b: (0,) * len(shape))
    state = pltpu.VMEM((nc, RET_HEADS // 2, LANES, LANES), F32)
    return pl.pallas_call(
        kern,
        grid=(batch,),
        in_specs=[col(0), col(1), col(2), col(3), full((seq, LANES)), full((seq, LANES)),
                  full((1, D_RET)), full((1, D_RET))],
        out_specs=pl.BlockSpec((seq, D_RET), lambda b: (b, 0)),
        out_shape=jax.ShapeDtypeStruct((batch * seq, D_RET), F32),
        scratch_shapes=[pltpu.VMEM((seq, D_RET), F32), pltpu.VMEM((seq, D_RET), F32), state, state],
        compiler_params=_cparams(("parallel",)),
        name="retention",
    )(proj, proj, proj, proj, cos_tab, sin_tab, gn_w.reshape(1, D_RET), gn_b.reshape(1, D_RET))


def _rwkv_prep_kernel(r_ref, k_ref, v_ref, z_ref, rp_ref, kp_ref, vp_ref, zp_ref,
                      rn_ref, kn_ref, vn_ref, zn_ref, mu_ref, w0_ref, w1_ref, w2_ref,
                      a0_ref, a1_ref, a2_ref, g1_ref, g2_ref, kk_ref, ka_ref, rk_ref,
                      lw0_o, lw1_o, kd0_o, kd1_o, bb0_o, bb1_o, v_o, r_o, kk_o, gate_o, bonus_o,
                      *, tb):
    i = pl.program_id(1)
    has_prev = (i > 0).astype(F32)
    has_next = (i < pl.num_programs(1) - 1).astype(F32)
    row = lax.broadcasted_iota(jnp.int32, (tb, 1), 0)
    avg = _head_avg_matrix()

    def head_sum(x):
        return jnp.concatenate([_head_mean(x[:, c * LANES:(c + 1) * LANES], avg)
                                for c in range(D_RWKV // LANES)], axis=1) * HEAD_DIM

    def shifted(f_ref, p_ref, n_ref):
        f = f_ref[...]
        prev = jnp.where(row == 0, p_ref[7:8, :] * has_prev, pltpu.roll(f, 1, 0))
        nxt = jnp.where(row == tb - 1, n_ref[0:1, :] * has_next, pltpu.roll(f, tb - 1, 0))
        return f, prev - f, nxt - f

    def mix(parts, idx):
        f, dp, dn = parts
        return f + mu_ref[idx, 0:1, :] * dp + mu_ref[idx, 1:2, :] * dn

    xr = mix(shifted(r_ref, rp_ref, rn_ref), 0)
    xk = mix(shifted(k_ref, kp_ref, kn_ref), 1)
    xv = mix(shifted(v_ref, vp_ref, vn_ref), 2)
    zparts = shifted(z_ref, zp_ref, zn_ref)
    xw = mix(zparts, 3)
    xa = mix(zparts, 4)
    xg = mix(zparts, 5)

    gate = _bdot(jax.nn.sigmoid(_bdot(xg, g1_ref[...])), g2_ref[...])
    w_lo = _bdot(jnp.tanh(_bdot(xw, w1_ref[...])), w2_ref[...])
    a_lo = _bdot(_bdot(xa, a1_ref[...]), a2_ref[...])
    kk = xk * kk_ref[...]
    kk = kk / jnp.maximum(jnp.sqrt(head_sum(kk * kk)), 1e-12)

    ksum = jnp.zeros_like(xk)
    outs = ((lw0_o, kd0_o, bb0_o), (lw1_o, kd1_o, bb1_o))
    for d in range(2):
        cols = slice(d * D_RWKV, (d + 1) * D_RWKV)
        w_raw = w0_ref[d:d + 1, :] + w_lo[:, cols]
        log_w = -math.exp(-0.5) * jax.nn.sigmoid(w_raw)
        a = jax.nn.sigmoid(a0_ref[d:d + 1, :] + a_lo[:, cols])
        k_dir = xk * (1.0 + (a - 1.0) * ka_ref[...])
        ksum = ksum + k_dir
        lw_o, kd_o, bb_o = outs[d]
        lw_o[...] = log_w
        kd_o[...] = k_dir
        bb_o[...] = kk * a
    bonus = head_sum(xr * ksum * rk_ref[...])
    v_o[...] = xv
    r_o[...] = xr
    kk_o[...] = kk
    gate_o[...] = gate
    bonus_o[...] = bonus * xv


def _rwkv_prep(proj, mu, w0, w1, w2, a0, a1, a2, g1, g2, k_k, k_a, r_k, batch, seq):
    tb = min(256, seq)
    nt = seq // tb
    d = D_RWKV
    col0 = (4 * D_RET) // d
    main = lambda j: pl.BlockSpec((tb, d), lambda b, i, j=j: (b * nt + i, col0 + j))
    prev = lambda j: pl.BlockSpec(
        (8, d), lambda b, i, j=j: (b * (seq // 8) + jnp.maximum(i * (tb // 8) - 1, 0), col0 + j))
    nxt = lambda j: pl.BlockSpec(
        (8, d), lambda b, i, j=j: (b * (seq // 8) + jnp.minimum((i + 1) * (tb // 8), seq // 8 - 1),
                                   col0 + j))
    full = lambda shape: pl.BlockSpec(shape, lambda b, i: (0,) * len(shape))
    out_spec = pl.BlockSpec((tb, d), lambda b, i: (b * nt + i, 0))
    out_sd = jax.ShapeDtypeStruct((batch * seq, d), F32)
    row = lambda a: a.reshape(1, d)

    def both_dirs(first, second):
        r = first.shape[-1]
        blk = jnp.zeros((2 * r, 2 * d), F32)
        blk = blk.at[:r, :d].set(second[0]).at[r:, d:].set(second[1])
        return jnp.concatenate([first[0], first[1]], axis=1).astype(BF16), blk.astype(BF16)

    w1, w2 = both_dirs(w1, w2)
    a1, a2 = both_dirs(a1, a2)
    g1, g2 = g1.astype(BF16), g2.astype(BF16)
    return pl.pallas_call(
        functools.partial(_rwkv_prep_kernel, tb=tb),
        grid=(batch, nt),
        in_specs=[main(0), main(1), main(2), main(3), prev(0), prev(1), prev(2), prev(3),
                  nxt(0), nxt(1), nxt(2), nxt(3),
                  full(mu.shape), full(w0.shape), full(w1.shape), full(w2.shape),
                  full(a0.shape), full(a1.shape), full(a2.shape), full(g1.shape), full(g2.shape),
                  full((1, d)), full((1, d)), full((1, d))],
        out_specs=[out_spec] * 11,
        out_shape=[out_sd] * 11,
        compiler_params=_cparams(("parallel", "parallel")),
        name="rwkv_features",
    )(*([proj] * 12), mu, w0, w1, w2, a0, a1, a2, g1, g2, row(k_k), row(k_a), row(r_k))


def _split3(x):
    hi = x.astype(BF16)
    rest = x - hi.astype(F32)
    mid = rest.astype(BF16)
    lo = (rest - mid.astype(F32)).astype(BF16)
    return hi, mid, lo


def _rwkv_scan_kernel(lw0_ref, kd0_ref, bb0_ref, v0_ref, r0_ref, kk0_ref,
                      lw1_ref, kd1_ref, bb1_ref, v1_ref, r1_ref, kk1_ref,
                      y0_ref, y1_ref, s_ref, gr_ref, u0_ref, mr_ref, vbd_ref, ev_ref, dec_ref,
                      *, chunk, group, nb):
    npair = RWKV_HEADS // 2
    assert 2 * chunk == LANES

    @pl.when(pl.program_id(1) == 0)
    def _():
        s_ref[...] = jnp.zeros_like(s_ref)

    lane = lax.broadcasted_iota(jnp.int32, (1, LANES), 1)
    halves = (lane < chunk, lane >= chunk)
    ri = lax.broadcasted_iota(jnp.int32, (LANES, LANES), 0)
    ci = lax.broadcasted_iota(jnp.int32, (LANES, LANES), 1)
    same_head = (ri // HEAD_DIM) == (ci // HEAD_DIM)
    ti = lax.broadcasted_iota(jnp.int32, (chunk, LANES), 0)
    tj = lax.broadcasted_iota(jnp.int32, (chunk, LANES), 1) % chunk
    eye = (ti == tj).astype(F32)
    strict = (tj < ti, tj > ti)
    incl = (tj <= ti, tj >= ti)
    si = lax.broadcasted_iota(jnp.int32, (chunk, chunk), 0)
    sj = lax.broadcasted_iota(jnp.int32, (chunk, chunk), 1)
    cum_mat = ((sj <= si).astype(BF16), (sj >= si).astype(BF16))
    ins = ((lw0_ref, kd0_ref, bb0_ref, v0_ref, r0_ref, kk0_ref),
           (lw1_ref, kd1_ref, bb1_ref, v1_ref, r1_ref, kk1_ref))
    outs = (y0_ref, y1_ref)
    items = [(c, bb, d, p) for c in range(group) for bb in range(nb) for p in range(npair) for d in range(2)]
    rows = lambda c, d: slice((c if d == 0 else group - 1 - c) * chunk,
                              ((c if d == 0 else group - 1 - c) + 1) * chunk)

    def blockdiag(x):
        return jnp.concatenate([jnp.where(halves[0], x, 0.0), jnp.where(halves[1], x, 0.0)],
                               axis=0).astype(BF16)

    feats = {}
    for c, bb, d in [(c, bb, d) for c in range(group) for bb in range(nb) for d in range(2)]:
        lw_ref, kd_ref, bb_ref, v_ref, r_ref, kk_ref = ins[d]
        rs = rows(c, d)
        lw = lw_ref[bb, rs, :]
        cum_in = sum(_dot(cum_mat[d], part) for part in _split3(lw))
        total = jnp.sum(lw, axis=0, keepdims=True)
        kd = kd_ref[bb, rs, :]
        bv = bb_ref[bb, rs, :]
        inv_p = jnp.exp(-cum_in)
        to_end = jnp.exp(total - cum_in)
        feats[c, bb, d] = dict(a=-kk_ref[bb, rs, :] * jnp.exp(cum_in - lw),
                               r=r_ref[bb, rs, :] * jnp.exp(cum_in),
                               b=bv * inv_p, k=kd * inv_p, v=v_ref[bb, rs, :],
                               bend=bv * to_end, kend=kd * to_end, dec=jnp.exp(total))

    def per_item(name):
        return [feats[c, bb, d][name][:, p * LANES:(p + 1) * LANES] for c, bb, d, p in items]

    a = per_item("a")
    v = per_item("v")
    v_bd = [blockdiag(x) for x in v]
    ar = [jnp.concatenate(pair, axis=0).astype(BF16) for pair in zip(a, per_item("r"))]
    bk = [jnp.concatenate([blockdiag(x), blockdiag(y)], axis=0)
          for x, y in zip(per_item("b"), per_item("k"))]
    gram = [_dot_nt(x, y) for x, y in zip(ar, bk)]
    l_ab = [jnp.where(strict[d], g[:chunk, :LANES], 0.0) for (_, _, d, _), g in zip(items, gram)]
    l_ak = [jnp.where(strict[d], g[:chunk, LANES:], 0.0).astype(BF16)
            for (_, _, d, _), g in zip(items, gram)]
    for n, ((_, _, d, _), g) in enumerate(zip(items, gram)):
        mr_ref[n] = jnp.where(jnp.concatenate([incl[d], incl[d]], axis=1), g[chunk:, :], 0.0).astype(BF16)
    lv = [_dot(x, y) for x, y in zip(l_ak, v_bd)]

    power = [_dot(x.astype(BF16), blockdiag(x)) for x in l_ab]
    inv = [eye + x for x in l_ab]
    rounds = int(math.log2(chunk)) - 1
    for j in range(rounds):
        pd = [blockdiag(x) for x in power]
        if j == rounds - 1:
            inv = [t + _dot(t.astype(BF16), x) for t, x in zip(inv, pd)]
        else:
            both = [_dot(jnp.concatenate([x, t], axis=0).astype(BF16), y)
                    for x, t, y in zip(power, inv, pd)]
            power = [x[:chunk] for x in both]
            inv = [t + x[chunk:] for t, x in zip(inv, both)]
    ta = [_dot(t.astype(BF16), jnp.concatenate([blockdiag(x), blockdiag(y)], axis=1))
          for t, x, y in zip(inv, a, lv)]
    for n, (c, bb, d, p) in enumerate(items):
        cols = slice(p * LANES, (p + 1) * LANES)
        f = feats[c, bb, d]
        gr_ref[n] = jnp.concatenate([ta[n][:, :LANES].astype(BF16), ar[n][chunk:]], axis=0)
        u0_ref[n] = ta[n][:, LANES:]
        vbd_ref[n] = v_bd[n]
        ev_ref[n] = jnp.concatenate([f["bend"][:, cols], f["kend"][:, cols], v[n]], axis=0).astype(BF16)
        dec_ref[n] = jnp.broadcast_to(f["dec"][:, cols], (8, LANES))

    per_chunk = 2 * npair * nb
    for c in range(group):
        ns = range(c * per_chunk, (c + 1) * per_chunk)
        states = [s_ref[j] for j in range(per_chunk)]
        prod = [_dot_nt(gr_ref[n], s.astype(BF16)) for n, s in zip(ns, states)]
        u = [x[:chunk] + u0_ref[n] for n, x in zip(ns, prod)]
        y = [x[chunk:] + _dot(mr_ref[n], jnp.concatenate([blockdiag(w), vbd_ref[n]], axis=0))
             for n, x, w in zip(ns, prod, u)]
        upd = [_dot_tn(jnp.concatenate([w.astype(BF16), ev_ref[n, 2 * chunk:, :]], axis=0),
                       ev_ref[n, :2 * chunk, :]) for n, w in zip(ns, u)]
        for j, n in enumerate(ns):
            _, bb, d, p = items[n]
            outs[d][bb, rows(c, d), p * LANES:(p + 1) * LANES] = y[j]
            s_ref[j] = states[j] * dec_ref[n, 0:1, :] + jnp.where(same_head, upd[j], 0.0)


def _rwkv_scan(feats, batch, seq):
    chunk = min(RWKV_CHUNK, seq)
    group = min(RWKV_GROUP, seq // chunk)
    tb = chunk * group
    nt = seq // tb
    nb = 2 if batch % 2 == 0 else 1
    n_items = group * RWKV_HEADS * nb
    fwd = pl.BlockSpec((nb, tb, D_RWKV), lambda b, i: (b, i, 0))
    bwd = pl.BlockSpec((nb, tb, D_RWKV), lambda b, i: (b, nt - 1 - i, 0))
    out_sd = jax.ShapeDtypeStruct((batch, seq, D_RWKV), F32)
    lw0, lw1, kd0, kd1, bb0, bb1, xv, xr, kk = [x.reshape(batch, seq, D_RWKV) for x in feats]
    y0, y1 = pl.pallas_call(
        functools.partial(_rwkv_scan_kernel, chunk=chunk, group=group, nb=nb),
        grid=(batch // nb, nt),
        in_specs=[fwd] * 6 + [bwd] * 6,
        out_specs=[fwd, bwd],
        out_shape=[out_sd, out_sd],
        scratch_shapes=[pltpu.VMEM((RWKV_HEADS * nb, LANES, LANES), F32),
                        pltpu.VMEM((n_items, 2 * chunk, LANES), BF16),
                        pltpu.VMEM((n_items, chunk, LANES), F32),
                        pltpu.VMEM((n_items, chunk, 4 * chunk), BF16),
                        pltpu.VMEM((n_items, 2 * chunk, LANES), BF16),
                        pltpu.VMEM((n_items, 3 * chunk, LANES), BF16),
                        pltpu.VMEM((n_items, 8, LANES), F32)],
        compiler_params=_cparams(("parallel", "arbitrary")),
        name="rwkv_scan",
    )(lw0, kd0, bb0, xv, xr, kk, lw1, kd1, bb1, xv, xr, kk)
    return y0.reshape(batch * seq, D_RWKV), y1.reshape(batch * seq, D_RWKV)


def _diff_attn_kernel(q_ref, k_ref, v_ref, band_ref, lam_ref, subw_ref, o_ref, kb_ref, vt_ref,
                      acc_ref, s_ref, *, seq, tq, tk, lambda_init):
    i = pl.program_id(2)

    @pl.when(i == 0)
    def _():
        kb_ref[...] = k_ref[...].astype(BF16)
        vt_ref[...] = jnp.transpose(v_ref[...]).astype(BF16)

    lane = lax.broadcasted_iota(jnp.int32, (1, LANES), 1)
    lv = lam_ref[...]
    lam = (jnp.exp(jnp.sum(lv[0:1] * lv[1:2], axis=1, keepdims=True))
           - jnp.exp(jnp.sum(lv[2:3] * lv[3:4], axis=1, keepdims=True)) + lambda_init)
    q = q_ref[...] * (DIFF_QK_DIM ** -0.5 * LOG2E)
    ones_rows = jnp.ones((BF16_ROWS, tk), BF16)
    n_maps = 4
    qms = [jnp.where((lane >= m * DIFF_QK_DIM) & (lane < (m + 1) * DIFF_QK_DIM), q, 0.0).astype(BF16)
           for m in range(n_maps)]
    n_tiles = seq // tk
    acc_ref[...] = jnp.zeros_like(acc_ref)

    def tile_index(j):
        t = i + j
        return jnp.where(t >= n_tiles, t - n_tiles, t)

    def stage_scores(j, m):
        start = pl.multiple_of(tile_index(j) * tk, tk)
        s_ref[j % 2, m] = _dot_nt(kb_ref[pl.ds(start, tk), :], qms[m])

    m_run = [jnp.full((1, tq), -1e30, F32)] * n_maps
    for m in range(n_maps):
        stage_scores(0, m)
    for j in range(n_tiles):
        slot = j % 2
        t = tile_index(j)
        start = pl.multiple_of(t * tk, tk)
        far = 2 <= j <= n_tiles - 2
        rows = BF16_ROWS // 2 if far else tk
        band_start = pl.multiple_of((jnp.clip(t - i, -2, 2) + 2) * tk, tk)
        bias = [band_ref[hh, pl.ds(band_start, rows), :] for hh in range(2)]
        vt = [jnp.concatenate([vt_ref[hh * HEAD_DIM:(hh + 1) * HEAD_DIM, pl.ds(start, tk)],
                               ones_rows], axis=0) for hh in range(2)]
        for m in range(n_maps):
            if j + 1 < n_tiles:
                stage_scores(j + 1, m)
            if far:
                s = s_ref[slot, m]
                const = bias[m // 2][0:1, :]
                m_new = jnp.maximum(m_run[m], jnp.max(s, axis=0, keepdims=True) + const)
                e = jnp.exp2(s - (m_new - const)).astype(BF16)
            else:
                s = s_ref[slot, m] + bias[m // 2]
                m_new = jnp.maximum(m_run[m], jnp.max(s, axis=0, keepdims=True))
                e = jnp.exp2(s - m_new).astype(BF16)
            pv = _dot(vt[m // 2], e)
            acc_ref[m] = acc_ref[m] * jnp.exp2(m_run[m] - m_new) + pv
            m_run[m] = m_new
    parts = [acc_ref[m, :HEAD_DIM, :] / acc_ref[m, HEAD_DIM:HEAD_DIM + 1, :] for m in range(n_maps)]
    heads = []
    for hh in range(2):
        head = parts[2 * hh] - lam * parts[2 * hh + 1]
        ms = jnp.mean(head * head, axis=0, keepdims=True)
        heads.append(head * lax.rsqrt(ms + NORM_EPS))
    out = jnp.transpose(jnp.concatenate(heads, axis=0))
    o_ref[...] = out * subw_ref[...] * (1.0 - lambda_init)


def _diff_attention(proj, band, lam_vecs, subln_w, lambda_init, batch, seq):
    tq = min(ATTN_TQ, seq)
    nq = seq // tq
    npair = DIFF_HEADS // 2
    col0 = (4 * D_RET + 4 * D_RWKV) // LANES
    width = band.shape[1]
    tk = tq
    assert 2 * _bucket_saturation_distance() < tk
    return pl.pallas_call(
        functools.partial(_diff_attn_kernel, seq=seq, tq=tq, tk=tk, lambda_init=lambda_init),
        grid=(batch, npair, nq),
        in_specs=[pl.BlockSpec((tq, LANES), lambda b, p, i: (b * nq + i, col0 + p)),
                  pl.BlockSpec((seq, LANES), lambda b, p, i: (b, col0 + npair + p)),
                  pl.BlockSpec((seq, LANES), lambda b, p, i: (b, col0 + 2 * npair + p)),
                  pl.BlockSpec((2, width, tq), lambda b, p, i: (p, 0, 0)),
                  pl.BlockSpec((4, DIFF_QK_DIM), lambda b, p, i: (0, 0)),
                  pl.BlockSpec((1, LANES), lambda b, p, i: (0, 0))],
        out_specs=pl.BlockSpec((tq, LANES), lambda b, p, i: (b * nq + i, p)),
        out_shape=jax.ShapeDtypeStruct((batch * seq, D_DIFF), F32),
        scratch_shapes=[pltpu.VMEM((seq, LANES), BF16), pltpu.VMEM((LANES, seq), BF16),
                        pltpu.VMEM((4, HEAD_DIM + BF16_ROWS, tq), F32),
                        pltpu.VMEM((2, 4, tk, tq), F32)],
        compiler_params=_cparams(("parallel", "parallel", "arbitrary")),
        name="diff_attention",
    )(proj, proj, proj, band, lam_vecs, jnp.tile(subln_w, 2).reshape(1, LANES))


def _t5_bucket(rel):
    nb = REL_BUCKETS // 2
    max_exact = nb // 2
    n = jnp.abs(rel)
    nf = jnp.maximum(n, 1).astype(jnp.float32)
    large = max_exact + (jnp.log(nf / max_exact) / math.log(REL_MAX_DIST / max_exact)
                         * (nb - max_exact)).astype(jnp.int32)
    large = jnp.minimum(large, nb - 1)
    return jnp.where(rel > 0, nb, 0) + jnp.where(n < max_exact, n, large)


def _bucket_saturation_distance():
    nb = REL_BUCKETS // 2
    max_exact = nb // 2
    n = np.arange(1, 4 * REL_MAX_DIST)
    large = max_exact + (np.log(n / max_exact) / math.log(REL_MAX_DIST / max_exact)
                         * (nb - max_exact)).astype(np.int64)
    bucket = np.where(n < max_exact, n, np.minimum(large, nb - 1))
    return int(n[bucket < nb - 1].max())


def _bias_band(rel_bias, tq):
    width = 5 * tq
    period = 6 * tq - 1
    m = jnp.arange(period, dtype=jnp.int32)
    rel = jnp.where(m < width, m, m - period) - 2 * tq
    vec = rel_bias.astype(F32)[_t5_bucket(rel)].T * LOG2E
    rows = jnp.tile(vec, (1, tq))[:, :tq * (period - 1)].reshape(-1, tq, period - 1)
    return jnp.transpose(rows[:, :, :width], (0, 2, 1))


def _outproj_kernel(x_ref, yr_ref, y0_ref, y1_ref, bonus_ref, gate_ref, yd_ref, lnw_ref, lnb_ref,
                    w_ref, o_ref):
    avg = _head_avg_matrix()
    mixed = [yr_ref[...].astype(BF16)]
    for p in range(RWKV_HEADS // 2):
        cols = slice(p * LANES, (p + 1) * LANES)
        y = _group_norm(y0_ref[:, cols] + y1_ref[:, cols], avg, lnw_ref[:, cols], lnb_ref[:, cols],
                        RWKV_GN_EPS)
        mixed.append(((y + bonus_ref[:, cols]) * gate_ref[:, cols]).astype(BF16))
    mixed.append(yd_ref[...].astype(BF16))
    o_ref[...] = x_ref[...] + _dot(jnp.concatenate(mixed, axis=1), w_ref[...])


def _outproj(x2, y_ret, y0, y1, bonus, gate, y_diff, ln_w, ln_b, w_bf16, tm=512):
    m, d = x2.shape
    tm = min(tm, m)
    row = lambda n: pl.BlockSpec((tm, n), lambda i: (i, 0))
    vec = pl.BlockSpec((1, D_RWKV), lambda i: (0, 0))
    return pl.pallas_call(
        _outproj_kernel,
        grid=(m // tm,),
        in_specs=[row(d), row(D_RET), row(D_RWKV), row(D_RWKV), row(D_RWKV), row(D_RWKV), row(D_DIFF),
                  vec, vec, pl.BlockSpec(w_bf16.shape, lambda i: (0, 0))],
        out_specs=row(d),
        out_shape=jax.ShapeDtypeStruct((m, d), F32),
        compiler_params=_cparams(("parallel",)),
        name="outproj",
    )(x2, y_ret, y0, y1, bonus, gate, y_diff, ln_w.reshape(1, D_RWKV), ln_b.reshape(1, D_RWKV), w_bf16)


def _ffn_up_kernel(x_ref, g_ref, wg_ref, wu_ref, o_ref, *, tf):
    x = x_ref[...]
    ms = jnp.mean(x * x, axis=-1, keepdims=True)
    h = (x * lax.rsqrt(ms + NORM_EPS) * g_ref[...]).astype(BF16)
    for j in range(wg_ref.shape[1] // tf):
        cols = slice(j * tf, (j + 1) * tf)
        gate = _dot(h, wg_ref[:, cols])
        up = _dot(h, wu_ref[:, cols])
        o_ref[:, cols] = (gate * jax.nn.sigmoid(gate) * up).astype(BF16)


def _ffn_up(x2, g, wg_bf16, wu_bf16, tm=512, tf=256):
    m, d = x2.shape
    tm = min(tm, m)
    f = wg_bf16.shape[1]
    return pl.pallas_call(
        functools.partial(_ffn_up_kernel, tf=tf),
        grid=(m // tm,),
        in_specs=[pl.BlockSpec((tm, d), lambda i: (i, 0)),
                  pl.BlockSpec((1, d), lambda i: (0, 0)),
                  pl.BlockSpec((d, f), lambda i: (0, 0)),
                  pl.BlockSpec((d, f), lambda i: (0, 0))],
        out_specs=pl.BlockSpec((tm, f), lambda i: (i, 0)),
        out_shape=jax.ShapeDtypeStruct((m, f), BF16),
        compiler_params=_cparams(("parallel",)),
        name="ffn_up",
    )(x2, g.reshape(1, d), wg_bf16, wu_bf16)


def _ffn_down_kernel(x_ref, h_ref, w_ref, g_ref, o_ref, *, final_norm):
    y = x_ref[...] + _dot(h_ref[...], w_ref[...])
    if final_norm:
        ms = jnp.mean(y * y, axis=-1, keepdims=True)
        y = y * lax.rsqrt(ms + NORM_EPS) * g_ref[...]
    o_ref[...] = y


def _ffn_down(x2, hidden, w_bf16, g, final_norm, tm=512):
    m, d = x2.shape
    tm = min(tm, m)
    f = hidden.shape[1]
    return pl.pallas_call(
        functools.partial(_ffn_down_kernel, final_norm=final_norm),
        grid=(m // tm,),
        in_specs=[pl.BlockSpec((tm, d), lambda i: (i, 0)),
                  pl.BlockSpec((tm, f), lambda i: (i, 0)),
                  pl.BlockSpec((f, d), lambda i: (0, 0)),
                  pl.BlockSpec((1, d), lambda i: (0, 0))],
        out_specs=pl.BlockSpec((tm, d), lambda i: (i, 0)),
        out_shape=jax.ShapeDtypeStruct((m, d), F32),
        compiler_params=_cparams(("parallel",)),
        name="ffn_down",
    )(x2, hidden, w_bf16, g.reshape(1, d))


def _rope_tables(seq):
    half = HEAD_DIM // 2
    freqs = ROPE_BASE ** (-jnp.arange(half, dtype=F32) / half)
    ang = jnp.arange(seq, dtype=jnp.int32).astype(F32)[:, None] * freqs[None, :]
    cos = jnp.cos(ang)
    sin = jnp.sin(ang)
    cos_tab = jnp.tile(cos, (1, LANES // half))
    sin_tab = jnp.tile(jnp.concatenate([-sin, sin], axis=1), (1, LANES // HEAD_DIM))
    return cos_tab, sin_tab


def kernel(x, mix_norm_g, w_in, w_out, ret_gn_w, ret_gn_b, rwkv_mu, rwkv_w0, rwkv_w1, rwkv_w2, rwkv_a0, rwkv_a1, rwkv_a2, rwkv_g1, rwkv_g2, rwkv_k_k, rwkv_k_a, rwkv_r_k, rwkv_ln_w, rwkv_ln_b, diff_lambda, diff_subln_w, rel_bias, ffn_norm_g, w_gate, w_up, w_down, final_norm_g):
    batch, seq, d = x.shape
    depth = w_in.shape[0]
    cos_tab, sin_tab = _rope_tables(seq)
    band = _bias_band(rel_bias, min(ATTN_TQ, seq))
    x2 = x.reshape(batch * seq, d)
    for l in range(depth):
        proj = _norm_matmul(x2, mix_norm_g[l], w_in[l].astype(BF16))
        y_ret = _retention(proj, cos_tab, sin_tab, ret_gn_w[l], ret_gn_b[l], batch, seq)
        feats = _rwkv_prep(proj, rwkv_mu[l], rwkv_w0[l], rwkv_w1[l], rwkv_w2[l], rwkv_a0[l],
                           rwkv_a1[l], rwkv_a2[l], rwkv_g1[l], rwkv_g2[l], rwkv_k_k[l],
                           rwkv_k_a[l], rwkv_r_k[l].reshape(-1), batch, seq)
        y0, y1 = _rwkv_scan(feats[:9], batch, seq)
        lambda_init = 0.8 - 0.6 * math.exp(-0.3 * l)
        y_diff = _diff_attention(proj, band, diff_lambda[l], diff_subln_w[l], lambda_init,
                                 batch, seq)
        x2 = _outproj(x2, y_ret, y0, y1, feats[10], feats[9], y_diff, rwkv_ln_w[l], rwkv_ln_b[l],
                      w_out[l].astype(BF16))
        hidden = _ffn_up(x2, ffn_norm_g[l], w_gate[l].astype(BF16), w_up[l].astype(BF16))
        x2 = _ffn_down(x2, hidden, w_down[l].astype(BF16), final_norm_g, l == depth - 1)
    return x2.reshape(batch, seq, d)
```

```python
import functools
import math

import numpy as np
import jax
import jax.numpy as jnp
from jax import lax
from jax.experimental import pallas as pl
from jax.experimental.pallas import tpu as pltpu

F32 = jnp.float32
BF16 = jnp.bfloat16

D_MODEL = 1024
HEAD_DIM = 64
LANES = 128
BF16_ROWS = 16
LOG2E = math.log2(math.e)
N_HEADS = D_MODEL // HEAD_DIM
RET_HEADS = (3 * N_HEADS) // 8
RWKV_HEADS = (3 * N_HEADS) // 8
DIFF_HEADS = N_HEADS - RET_HEADS - RWKV_HEADS
D_RET = RET_HEADS * HEAD_DIM
D_RWKV = RWKV_HEADS * HEAD_DIM
D_DIFF = DIFF_HEADS * HEAD_DIM
DIFF_QK_DIM = HEAD_DIM // 2
D_IN = 4 * D_RET + 4 * D_RWKV + 3 * D_DIFF
ROPE_BASE = 10000.0
D_FF = -(-8 * D_MODEL // (3 * 256)) * 256
REL_BUCKETS = 32
REL_MAX_DIST = 128
NORM_EPS = 1e-6
RET_GN_EPS = 1e-5
RWKV_GN_EPS = 64e-5

RET_CHUNK = 128
RWKV_CHUNK = 64
RWKV_GROUP = 4
ATTN_TQ = 256
VMEM_LIMIT = 56 * 1024 * 1024


def _cparams(sem):
    return pltpu.CompilerParams(dimension_semantics=sem, vmem_limit_bytes=VMEM_LIMIT)


def _dot(a, b):
    return jnp.dot(a, b, preferred_element_type=F32)


def _bdot(a, b):
    return _dot(a.astype(BF16), b.astype(BF16))


def _dot_nt(a, b):
    return lax.dot_general(a, b, (((1,), (1,)), ((), ())), preferred_element_type=F32)


def _dot_tn(a, b):
    return lax.dot_general(a, b, (((0,), (0,)), ((), ())), preferred_element_type=F32)


def _head_avg_matrix():
    r = lax.broadcasted_iota(jnp.int32, (LANES, LANES), 0) // HEAD_DIM
    c = lax.broadcasted_iota(jnp.int32, (LANES, LANES), 1) // HEAD_DIM
    return jnp.where(r == c, 1.0 / HEAD_DIM, 0.0).astype(BF16)


def _head_mean(x, avg):
    hi = x.astype(BF16)
    lo = (x - hi.astype(F32)).astype(BF16)
    return _dot(hi, avg) + _dot(lo, avg)


def _group_norm(y, avg, w, b, eps):
    yc = y - _head_mean(y, avg)
    var = _head_mean(yc * yc, avg)
    return yc * lax.rsqrt(var + eps) * w + b


def _norm_matmul_kernel(x_ref, g_ref, w_ref, o_ref):
    x = x_ref[...]
    ms = jnp.mean(x * x, axis=-1, keepdims=True)
    h = x * lax.rsqrt(ms + NORM_EPS) * g_ref[...]
    o_ref[...] = _dot(h.astype(BF16), w_ref[...])


def _norm_matmul(x2, g, w_bf16, tm=256):
    m, d = x2.shape
    n = w_bf16.shape[1]
    return pl.pallas_call(
        _norm_matmul_kernel,
        grid=(m // tm,),
        in_specs=[pl.BlockSpec((tm, d), lambda i: (i, 0)),
                  pl.BlockSpec((1, d), lambda i: (0, 0)),
                  pl.BlockSpec((d, n), lambda i: (0, 0))],
        out_specs=pl.BlockSpec((tm, n), lambda i: (i, 0)),
        out_shape=jax.ShapeDtypeStruct((m, n), F32),
        compiler_params=_cparams(("parallel",)),
        name="norm_inproj",
    )(x2, g.reshape(1, d), w_bf16)


def _retention_kernel(q_ref, k_ref, v_ref, g_ref, cos_ref, sin_ref, gnw_ref, gnb_ref, o_ref,
                      qs_ref, ks_ref, sf_ref, sb_ref, *, seq, chunk, log_gamma):
    nc = seq // chunk
    npair = RET_HEADS // 2
    lane = lax.broadcasted_iota(jnp.int32, (1, LANES), 1)
    first_half = (lane & (HEAD_DIM - 1)) < (HEAD_DIM // 2)
    head_masks = (lane < HEAD_DIM, lane >= HEAD_DIM)
    ri = lax.broadcasted_iota(jnp.int32, (LANES, LANES), 0)
    ci = lax.broadcasted_iota(jnp.int32, (LANES, LANES), 1)
    block_diag = (ri // HEAD_DIM) == (ci // HEAD_DIM)
    avg = _head_avg_matrix()
    ti = lax.broadcasted_iota(jnp.int32, (chunk, 2 * chunk), 0)
    tj = lax.broadcasted_iota(jnp.int32, (chunk, 2 * chunk), 1)
    dist = jnp.abs(ti - jnp.where(tj >= chunk, tj - chunk, tj)).astype(F32)
    pos = lax.broadcasted_iota(jnp.int32, (chunk, 1), 0).astype(F32)
    pairs = range(npair)
    cols = [slice(p * LANES, (p + 1) * LANES) for p in pairs]
    lg = [jnp.where(head_masks[0], log_gamma[2 * p], log_gamma[2 * p + 1]).astype(F32) for p in pairs]
    decay = [jnp.exp(dist * jnp.where(tj >= chunk, log_gamma[2 * p + 1], log_gamma[2 * p])) for p in pairs]
    xi_f = [jnp.exp((pos + 1.0) * x) for x in lg]
    xi_b = [jnp.exp((chunk - pos) * x) for x in lg]
    zeta_f = [jnp.exp((chunk - 1.0 - pos) * x) for x in lg]
    zeta_b = [jnp.exp(pos * x) for x in lg]
    dec_c = [jnp.exp(chunk * x) for x in lg]

    def rope(x, cos, sin):
        partner = jnp.where(first_half, pltpu.roll(x, LANES - HEAD_DIM // 2, 1),
                            pltpu.roll(x, HEAD_DIM // 2, 1))
        return x * cos + partner * sin

    def blockdiag(x):
        return jnp.concatenate([jnp.where(head_masks[0], x, 0.0), jnp.where(head_masks[1], x, 0.0)],
                               axis=0).astype(BF16)

    def pass1(n, carry):
        rows = pl.ds(pl.multiple_of(n * chunk, chunk), chunk)
        cos = cos_ref[rows, :]
        sin = sin_ref[rows, :]
        q = [rope(q_ref[rows, c], cos, sin) * (HEAD_DIM ** -0.5) for c in cols]
        k = [rope(k_ref[rows, c], cos, sin) for c in cols]
        kz = [jnp.concatenate([x * zf, x * zb], axis=1).astype(BF16) for x, zf, zb in zip(k, zeta_f, zeta_b)]
        kv = [_dot_tn(x, v_ref[rows, c].astype(BF16)) for x, c in zip(kz, cols)]
        for p in pairs:
            qs_ref[rows, cols[p]] = q[p]
            ks_ref[rows, cols[p]] = k[p]
            sf_ref[n, p] = jnp.where(block_diag, kv[p][:LANES], 0.0)
            sb_ref[n, p] = jnp.where(block_diag, kv[p][LANES:], 0.0)
        return carry

    lax.fori_loop(0, nc, pass1, 0)

    def scan_states(i, states):
        left, right = states
        new_left, new_right = [], []
        for p in pairs:
            kv = sf_ref[i, p]
            sf_ref[i, p] = left[p]
            new_left.append(left[p] * dec_c[p] + kv)
            kv = sb_ref[nc - 1 - i, p]
            sb_ref[nc - 1 - i, p] = right[p]
            new_right.append(right[p] * dec_c[p] + kv)
        return tuple(new_left), tuple(new_right)

    zeros = (jnp.zeros((LANES, LANES), F32),) * npair
    lax.fori_loop(0, nc, scan_states, (zeros, zeros))

    width = 2 if nc % 2 == 0 else 1

    def pass2(i, carry):
        work = [(i * width + w, p) for w in range(width) for p in pairs]
        rows = [pl.ds(pl.multiple_of(n * chunk, chunk), chunk) for n, _ in work]
        q = [qs_ref[r, cols[p]] for r, (_, p) in zip(rows, work)]
        k_bd = [blockdiag(ks_ref[r, cols[p]]) for r, (_, p) in zip(rows, work)]
        v_bd = [blockdiag(v_ref[r, cols[p]]) for r, (_, p) in zip(rows, work)]
        s = [(_dot_nt(x.astype(BF16), y) * decay[p]).astype(BF16) for x, y, (_, p) in zip(q, k_bd, work)]
        qx = [jnp.concatenate([x * xi_f[p], x * xi_b[p]], axis=1).astype(BF16) for x, (_, p) in zip(q, work)]
        st = [jnp.concatenate([sf_ref[n, p], sb_ref[n, p]], axis=0).astype(BF16) for n, p in work]
        o = [_dot(x, y) + _dot(z, w) for x, y, z, w in zip(s, v_bd, qx, st)]
        mean = [_head_mean(x, avg) for x in o]
        centred = [x - m for x, m in zip(o, mean)]
        var = [_head_mean(x * x, avg) for x in centred]
        for x, vr, r, (_, p) in zip(centred, var, rows, work):
            y = x * lax.rsqrt(vr + RET_GN_EPS) * gnw_ref[:, cols[p]] + gnb_ref[:, cols[p]]
            g = g_ref[r, cols[p]]
            o_ref[r, cols[p]] = (g * jax.nn.sigmoid(g) * y).astype(BF16)
        return carry

    lax.fori_loop(0, nc // width, pass2, 0)


def _retention(proj, cos_tab, sin_tab, gn_w, gn_b, batch, seq):
    chunk = min(RET_CHUNK, seq)
    nc = seq // chunk
    log_gamma = tuple(float(np.log1p(-np.exp2(-5.0 - h))) for h in range(RET_HEADS))
    kern = functools.partial(_retention_kernel, seq=seq, chunk=chunk, log_gamma=log_gamma)
    col = lambda j: pl.BlockSpec((seq, D_RET), lambda b, j=j: (b, j))
    full = lambda shape: pl.BlockSpec(shape, lambda b: (0,) * len(shape))
    state = pltpu.VMEM((nc, RET_HEADS // 2, LANES, LANES), F32)
    return pl.pallas_call(
        kern,
        grid=(batch,),
        in_specs=[col(0), col(1), col(2), col(3), full((seq, LANES)), full((seq, LANES)),
                  full((1, D_RET)), full((1, D_RET))],
        out_specs=pl.BlockSpec((seq, D_RET), lambda b: (b, 0)),
        out_shape=jax.ShapeDtypeStruct((batch * seq, D_RET), BF16),
        scratch_shapes=[pltpu.VMEM((seq, D_RET), F32), pltpu.VMEM((seq, D_RET), F32), state, state],
        compiler_params=_cparams(("parallel",)),
        name="retention",
    )(proj, proj, proj, proj, cos_tab, sin_tab, gn_w.reshape(1, D_RET), gn_b.reshape(1, D_RET))


def _rwkv_prep_kernel(r_ref, k_ref, v_ref, z_ref, rp_ref, kp_ref, vp_ref, zp_ref,
                      rn_ref, kn_ref, vn_ref, zn_ref, mu_ref, w0_ref, w1_ref, w2_ref,
                      a0_ref, a1_ref, a2_ref, g1_ref, g2_ref, kk_ref, ka_ref, rk_ref,
                      lw0_o, lw1_o, kd0_o, kd1_o, bb0_o, bb1_o, v_o, r_o, kk_o, gate_o, bonus_o,
                      *, tb):
    i = pl.program_id(1)
    has_prev = (i > 0).astype(F32)
    has_next = (i < pl.num_programs(1) - 1).astype(F32)
    row = lax.broadcasted_iota(jnp.int32, (tb, 1), 0)
    avg = _head_avg_matrix()

    def head_sum(x):
        return jnp.concatenate([_head_mean(x[:, c * LANES:(c + 1) * LANES], avg)
                                for c in range(D_RWKV // LANES)], axis=1) * HEAD_DIM

    def shifted(f_ref, p_ref, n_ref):
        f = f_ref[...]
        prev = jnp.where(row == 0, p_ref[7:8, :] * has_prev, pltpu.roll(f, 1, 0))
        nxt = jnp.where(row == tb - 1, n_ref[0:1, :] * has_next, pltpu.roll(f, tb - 1, 0))
        return f, prev - f, nxt - f

    def mix(parts, idx):
        f, dp, dn = parts
        return f + mu_ref[idx, 0:1, :] * dp + mu_ref[idx, 1:2, :] * dn

    xr = mix(shifted(r_ref, rp_ref, rn_ref), 0)
    xk = mix(shifted(k_ref, kp_ref, kn_ref), 1)
    xv = mix(shifted(v_ref, vp_ref, vn_ref), 2)
    zparts = shifted(z_ref, zp_ref, zn_ref)
    xw = mix(zparts, 3)
    xa = mix(zparts, 4)
    xg = mix(zparts, 5)

    gate = _bdot(jax.nn.sigmoid(_bdot(xg, g1_ref[...])), g2_ref[...])
    w_lo = _bdot(jnp.tanh(_bdot(xw, w1_ref[...])), w2_ref[...])
    a_lo = _bdot(_bdot(xa, a1_ref[...]), a2_ref[...])
    kk = xk * kk_ref[...]
    kk = kk / jnp.maximum(jnp.sqrt(head_sum(kk * kk)), 1e-12)

    ksum = jnp.zeros_like(xk)
    outs = ((lw0_o, kd0_o, bb0_o), (lw1_o, kd1_o, bb1_o))
    for d in range(2):
        cols = slice(d * D_RWKV, (d + 1) * D_RWKV)
        w_raw = w0_ref[d:d + 1, :] + w_lo[:, cols]
        log_w = -math.exp(-0.5) * jax.nn.sigmoid(w_raw)
        a = jax.nn.sigmoid(a0_ref[d:d + 1, :] + a_lo[:, cols])
        k_dir = xk * (1.0 + (a - 1.0) * ka_ref[...])
        ksum = ksum + k_dir
        lw_o, kd_o, bb_o = outs[d]
        lw_o[...] = log_w
        kd_o[...] = k_dir.astype(BF16)
        bb_o[...] = (kk * a).astype(BF16)
    bonus = head_sum(xr * ksum * rk_ref[...])
    v_o[...] = xv.astype(BF16)
    r_o[...] = xr.astype(BF16)
    kk_o[...] = kk.astype(BF16)
    gate_o[...] = gate.astype(BF16)
    bonus_o[...] = (bonus * xv).astype(BF16)


def _rwkv_prep(proj, mu, w0, w1, w2, a0, a1, a2, g1, g2, k_k, k_a, r_k, batch, seq):
    tb = min(256, seq)
    nt = seq // tb
    d = D_RWKV
    col0 = (4 * D_RET) // d
    main = lambda j: pl.BlockSpec((tb, d), lambda b, i, j=j: (b * nt + i, col0 + j))
    prev = lambda j: pl.BlockSpec(
        (8, d), lambda b, i, j=j: (b * (seq // 8) + jnp.maximum(i * (tb // 8) - 1, 0), col0 + j))
    nxt = lambda j: pl.BlockSpec(
        (8, d), lambda b, i, j=j: (b * (seq // 8) + jnp.minimum((i + 1) * (tb // 8), seq // 8 - 1),
                                   col0 + j))
    full = lambda shape: pl.BlockSpec(shape, lambda b, i: (0,) * len(shape))
    out_spec = pl.BlockSpec((tb, d), lambda b, i: (b * nt + i, 0))
    out_sd = lambda dtype: jax.ShapeDtypeStruct((batch * seq, d), dtype)
    row = lambda a: a.reshape(1, d)

    def both_dirs(first, second):
        r = first.shape[-1]
        blk = jnp.zeros((2 * r, 2 * d), F32)
        blk = blk.at[:r, :d].set(second[0]).at[r:, d:].set(second[1])
        return jnp.concatenate([first[0], first[1]], axis=1).astype(BF16), blk.astype(BF16)

    w1, w2 = both_dirs(w1, w2)
    a1, a2 = both_dirs(a1, a2)
    g1, g2 = g1.astype(BF16), g2.astype(BF16)
    return pl.pallas_call(
        functools.partial(_rwkv_prep_kernel, tb=tb),
        grid=(batch, nt),
        in_specs=[main(0), main(1), main(2), main(3), prev(0), prev(1), prev(2), prev(3),
                  nxt(0), nxt(1), nxt(2), nxt(3),
                  full(mu.shape), full(w0.shape), full(w1.shape), full(w2.shape),
                  full(a0.shape), full(a1.shape), full(a2.shape), full(g1.shape), full(g2.shape),
                  full((1, d)), full((1, d)), full((1, d))],
        out_specs=[out_spec] * 11,
        out_shape=[out_sd(F32)] * 2 + [out_sd(BF16)] * 9,
        compiler_params=_cparams(("parallel", "parallel")),
        name="rwkv_features",
    )(*([proj] * 12), mu, w0, w1, w2, a0, a1, a2, g1, g2, row(k_k), row(k_a), row(r_k))


def _split3(x):
    hi = x.astype(BF16)
    rest = x - hi.astype(F32)
    mid = rest.astype(BF16)
    lo = (rest - mid.astype(F32)).astype(BF16)
    return hi, mid, lo


def _rwkv_scan_kernel(lw0_ref, kd0_ref, bb0_ref, v0_ref, r0_ref, kk0_ref,
                      lw1_ref, kd1_ref, bb1_ref, v1_ref, r1_ref, kk1_ref,
                      y0_ref, y1_ref, s_ref, gr_ref, u0_ref, mr_ref, vbd_ref, ev_ref, dec_ref,
                      *, chunk, group, nb):
    npair = RWKV_HEADS // 2
    assert 2 * chunk == LANES

    @pl.when(pl.program_id(1) == 0)
    def _():
        s_ref[...] = jnp.zeros_like(s_ref)

    lane = lax.broadcasted_iota(jnp.int32, (1, LANES), 1)
    halves = (lane < chunk, lane >= chunk)
    ri = lax.broadcasted_iota(jnp.int32, (LANES, LANES), 0)
    ci = lax.broadcasted_iota(jnp.int32, (LANES, LANES), 1)
    same_head = (ri // HEAD_DIM) == (ci // HEAD_DIM)
    ti = lax.broadcasted_iota(jnp.int32, (chunk, LANES), 0)
    tj = lax.broadcasted_iota(jnp.int32, (chunk, LANES), 1) % chunk
    eye = (ti == tj).astype(F32)
    strict = (tj < ti, tj > ti)
    incl = (tj <= ti, tj >= ti)
    si = lax.broadcasted_iota(jnp.int32, (chunk, chunk), 0)
    sj = lax.broadcasted_iota(jnp.int32, (chunk, chunk), 1)
    cum_mat = ((sj <= si).astype(BF16), (sj >= si).astype(BF16))
    ins = ((lw0_ref, kd0_ref, bb0_ref, v0_ref, r0_ref, kk0_ref),
           (lw1_ref, kd1_ref, bb1_ref, v1_ref, r1_ref, kk1_ref))
    outs = (y0_ref, y1_ref)
    items = [(c, bb, d, p) for c in range(group) for bb in range(nb) for p in range(npair) for d in range(2)]
    rows = lambda c, d: slice((c if d == 0 else group - 1 - c) * chunk,
                              ((c if d == 0 else group - 1 - c) + 1) * chunk)

    def blockdiag(x):
        return jnp.concatenate([jnp.where(halves[0], x, 0.0), jnp.where(halves[1], x, 0.0)],
                               axis=0).astype(BF16)

    feats = {}
    for c, bb, d in [(c, bb, d) for c in range(group) for bb in range(nb) for d in range(2)]:
        lw_ref, kd_ref, bb_ref, v_ref, r_ref, kk_ref = ins[d]
        rs = rows(c, d)
        lw = lw_ref[bb, rs, :]
        cum_in = sum(_dot(cum_mat[d], part) for part in _split3(lw))
        total = jnp.sum(lw, axis=0, keepdims=True)
        kd = kd_ref[bb, rs, :].astype(F32)
        bv = bb_ref[bb, rs, :].astype(F32)
        inv_p = jnp.exp(-cum_in)
        to_end = jnp.exp(total - cum_in)
        feats[c, bb, d] = dict(a=-kk_ref[bb, rs, :].astype(F32) * jnp.exp(cum_in - lw),
                               r=r_ref[bb, rs, :].astype(F32) * jnp.exp(cum_in),
                               b=bv * inv_p, k=kd * inv_p, v=v_ref[bb, rs, :].astype(F32),
                               bend=bv * to_end, kend=kd * to_end, dec=jnp.exp(total))

    def per_item(name):
        return [feats[c, bb, d][name][:, p * LANES:(p + 1) * LANES] for c, bb, d, p in items]

    a = per_item("a")
    v = per_item("v")
    v_bd = [blockdiag(x) for x in v]
    ar = [jnp.concatenate(pair, axis=0).astype(BF16) for pair in zip(a, per_item("r"))]
    bk = [jnp.concatenate([blockdiag(x), blockdiag(y)], axis=0)
          for x, y in zip(per_item("b"), per_item("k"))]
    gram = [_dot_nt(x, y) for x, y in zip(ar, bk)]
    l_ab = [jnp.where(strict[d], g[:chunk, :LANES], 0.0) for (_, _, d, _), g in zip(items, gram)]
    l_ak = [jnp.where(strict[d], g[:chunk, LANES:], 0.0).astype(BF16)
            for (_, _, d, _), g in zip(items, gram)]
    for n, ((_, _, d, _), g) in enumerate(zip(items, gram)):
        mr_ref[n] = jnp.where(jnp.concatenate([incl[d], incl[d]], axis=1), g[chunk:, :], 0.0).astype(BF16)
    lv = [_dot(x, y) for x, y in zip(l_ak, v_bd)]

    power = [_dot(x.astype(BF16), blockdiag(x)) for x in l_ab]
    inv = [eye + x for x in l_ab]
    rounds = int(math.log2(chunk)) - 1
    for j in range(rounds):
        pd = [blockdiag(x) for x in power]
        if j == rounds - 1:
            inv = [t + _dot(t.astype(BF16), x) for t, x in zip(inv, pd)]
        else:
            both = [_dot(jnp.concatenate([x, t], axis=0).astype(BF16), y)
                    for x, t, y in zip(power, inv, pd)]
            power = [x[:chunk] for x in both]
            inv = [t + x[chunk:] for t, x in zip(inv, both)]
    ta = [_dot(t.astype(BF16), jnp.concatenate([blockdiag(x), blockdiag(y)], axis=1))
          for t, x, y in zip(inv, a, lv)]
    for n, (c, bb, d, p) in enumerate(items):
        cols = slice(p * LANES, (p + 1) * LANES)
        f = feats[c, bb, d]
        gr_ref[n] = jnp.concatenate([ta[n][:, :LANES].astype(BF16), ar[n][chunk:]], axis=0)
        u0_ref[n] = ta[n][:, LANES:]
        vbd_ref[n] = v_bd[n]
        ev_ref[n] = jnp.concatenate([f["bend"][:, cols], f["kend"][:, cols], v[n]], axis=0).astype(BF16)
        dec_ref[n] = jnp.broadcast_to(f["dec"][:, cols], (8, LANES))

    per_chunk = 2 * npair * nb
    for c in range(group):
        ns = range(c * per_chunk, (c + 1) * per_chunk)
        states = [s_ref[j] for j in range(per_chunk)]
        prod = [_dot_nt(gr_ref[n], s.astype(BF16)) for n, s in zip(ns, states)]
        u = [x[:chunk] + u0_ref[n] for n, x in zip(ns, prod)]
        y = [x[chunk:] + _dot(mr_ref[n], jnp.concatenate([blockdiag(w), vbd_ref[n]], axis=0))
             for n, x, w in zip(ns, prod, u)]
        upd = [_dot_tn(jnp.concatenate([w.astype(BF16), ev_ref[n, 2 * chunk:, :]], axis=0),
                       ev_ref[n, :2 * chunk, :]) for n, w in zip(ns, u)]
        for j, n in enumerate(ns):
            _, bb, d, p = items[n]
            outs[d][bb, rows(c, d), p * LANES:(p + 1) * LANES] = y[j]
            s_ref[j] = states[j] * dec_ref[n, 0:1, :] + jnp.where(same_head, upd[j], 0.0)


def _rwkv_scan(feats, batch, seq):
    chunk = min(RWKV_CHUNK, seq)
    group = min(RWKV_GROUP, seq // chunk)
    tb = chunk * group
    nt = seq // tb
    nb = 2 if batch % 2 == 0 else 1
    n_items = group * RWKV_HEADS * nb
    fwd = pl.BlockSpec((nb, tb, D_RWKV), lambda b, i: (b, i, 0))
    bwd = pl.BlockSpec((nb, tb, D_RWKV), lambda b, i: (b, nt - 1 - i, 0))
    out_sd = jax.ShapeDtypeStruct((batch, seq, D_RWKV), F32)
    lw0, lw1, kd0, kd1, bb0, bb1, xv, xr, kk = [x.reshape(batch, seq, D_RWKV) for x in feats]
    y0, y1 = pl.pallas_call(
        functools.partial(_rwkv_scan_kernel, chunk=chunk, group=group, nb=nb),
        grid=(batch // nb, nt),
        in_specs=[fwd] * 6 + [bwd] * 6,
        out_specs=[fwd, bwd],
        out_shape=[out_sd, out_sd],
        scratch_shapes=[pltpu.VMEM((RWKV_HEADS * nb, LANES, LANES), F32),
                        pltpu.VMEM((n_items, 2 * chunk, LANES), BF16),
                        pltpu.VMEM((n_items, chunk, LANES), F32),
                        pltpu.VMEM((n_items, chunk, 4 * chunk), BF16),
                        pltpu.VMEM((n_items, 2 * chunk, LANES), BF16),
                        pltpu.VMEM((n_items, 3 * chunk, LANES), BF16),
                        pltpu.VMEM((n_items, 8, LANES), F32)],
        compiler_params=_cparams(("parallel", "arbitrary")),
        name="rwkv_scan",
    )(lw0, kd0, bb0, xv, xr, kk, lw1, kd1, bb1, xv, xr, kk)
    return y0.reshape(batch * seq, D_RWKV), y1.reshape(batch * seq, D_RWKV)


def _diff_attn_kernel(q_ref, k_ref, v_ref, band_ref, lam_ref, subw_ref, o_ref, kb_ref, vt_ref,
                      acc_ref, s_ref, *, seq, tq, tk, lambda_init):
    i = pl.program_id(2)

    @pl.when(i == 0)
    def _():
        kb_ref[...] = k_ref[...].astype(BF16)
        vt_ref[...] = jnp.transpose(v_ref[...]).astype(BF16)

    lane = lax.broadcasted_iota(jnp.int32, (1, LANES), 1)
    lv = lam_ref[...]
    lam = (jnp.exp(jnp.sum(lv[0:1] * lv[1:2], axis=1, keepdims=True))
           - jnp.exp(jnp.sum(lv[2:3] * lv[3:4], axis=1, keepdims=True)) + lambda_init)
    q = q_ref[...] * (DIFF_QK_DIM ** -0.5 * LOG2E)
    ones_rows = jnp.ones((BF16_ROWS, tk), BF16)
    n_maps = 4
    qms = [jnp.where((lane >= m * DIFF_QK_DIM) & (lane < (m + 1) * DIFF_QK_DIM), q, 0.0).astype(BF16)
           for m in range(n_maps)]
    n_tiles = seq // tk
    acc_ref[...] = jnp.zeros_like(acc_ref)

    def tile_index(j):
        t = i + j
        return jnp.where(t >= n_tiles, t - n_tiles, t)

    def stage_scores(j, m):
        start = pl.multiple_of(tile_index(j) * tk, tk)
        s_ref[j % 2, m] = _dot_nt(kb_ref[pl.ds(start, tk), :], qms[m])

    m_run = [jnp.full((1, tq), -1e30, F32)] * n_maps
    for m in range(n_maps):
        stage_scores(0, m)
    for j in range(n_tiles):
        slot = j % 2
        t = tile_index(j)
        start = pl.multiple_of(t * tk, tk)
        far = 2 <= j <= n_tiles - 2
        rows = BF16_ROWS // 2 if far else tk
        band_start = pl.multiple_of((jnp.clip(t - i, -2, 2) + 2) * tk, tk)
        bias = [band_ref[hh, pl.ds(band_start, rows), :] for hh in range(2)]
        vt = [jnp.concatenate([vt_ref[hh * HEAD_DIM:(hh + 1) * HEAD_DIM, pl.ds(start, tk)],
                               ones_rows], axis=0) for hh in range(2)]
        for m in range(n_maps):
            if j + 1 < n_tiles:
                stage_scores(j + 1, m)
            if far:
                s = s_ref[slot, m]
                const = bias[m // 2][0:1, :]
                m_new = jnp.maximum(m_run[m], jnp.max(s, axis=0, keepdims=True) + const)
                e = jnp.exp2(s - (m_new - const)).astype(BF16)
            else:
                s = s_ref[slot, m] + bias[m // 2]
                m_new = jnp.maximum(m_run[m], jnp.max(s, axis=0, keepdims=True))
                e = jnp.exp2(s - m_new).astype(BF16)
            pv = _dot(vt[m // 2], e)
            acc_ref[m] = acc_ref[m] * jnp.exp2(m_run[m] - m_new) + pv
            m_run[m] = m_new
    parts = [acc_ref[m, :HEAD_DIM, :] / acc_ref[m, HEAD_DIM:HEAD_DIM + 1, :] for m in range(n_maps)]
    heads = []
    for hh in range(2):
        head = parts[2 * hh] - lam * parts[2 * hh + 1]
        ms = jnp.mean(head * head, axis=0, keepdims=True)
        heads.append(head * lax.rsqrt(ms + NORM_EPS))
    out = jnp.transpose(jnp.concatenate(heads, axis=0))
    o_ref[...] = (out * subw_ref[...] * (1.0 - lambda_init)).astype(BF16)


def _diff_attention(proj, band, lam_vecs, subln_w, lambda_init, batch, seq):
    tq = min(ATTN_TQ, seq)
    nq = seq // tq
    npair = DIFF_HEADS // 2
    col0 = (4 * D_RET + 4 * D_RWKV) // LANES
    width = band.shape[1]
    tk = tq
    assert 2 * _bucket_saturation_distance() < tk
    return pl.pallas_call(
        functools.partial(_diff_attn_kernel, seq=seq, tq=tq, tk=tk, lambda_init=lambda_init),
        grid=(batch, npair, nq),
        in_specs=[pl.BlockSpec((tq, LANES), lambda b, p, i: (b * nq + i, col0 + p)),
                  pl.BlockSpec((seq, LANES), lambda b, p, i: (b, col0 + npair + p)),
                  pl.BlockSpec((seq, LANES), lambda b, p, i: (b, col0 + 2 * npair + p)),
                  pl.BlockSpec((2, width, tq), lambda b, p, i: (p, 0, 0)),
                  pl.BlockSpec((4, DIFF_QK_DIM), lambda b, p, i: (0, 0)),
                  pl.BlockSpec((1, LANES), lambda b, p, i: (0, 0))],
        out_specs=pl.BlockSpec((tq, LANES), lambda b, p, i: (b * nq + i, p)),
        out_shape=jax.ShapeDtypeStruct((batch * seq, D_DIFF), BF16),
        scratch_shapes=[pltpu.VMEM((seq, LANES), BF16), pltpu.VMEM((LANES, seq), BF16),
                        pltpu.VMEM((4, HEAD_DIM + BF16_ROWS, tq), F32),
                        pltpu.VMEM((2, 4, tk, tq), F32)],
        compiler_params=_cparams(("parallel", "parallel", "arbitrary")),
        name="diff_attention",
    )(proj, proj, proj, band, lam_vecs, jnp.tile(subln_w, 2).reshape(1, LANES))


def _t5_bucket(rel):
    nb = REL_BUCKETS // 2
    max_exact = nb // 2
    n = jnp.abs(rel)
    nf = jnp.maximum(n, 1).astype(jnp.float32)
    large = max_exact + (jnp.log(nf / max_exact) / math.log(REL_MAX_DIST / max_exact)
                         * (nb - max_exact)).astype(jnp.int32)
    large = jnp.minimum(large, nb - 1)
    return jnp.where(rel > 0, nb, 0) + jnp.where(n < max_exact, n, large)


def _bucket_saturation_distance():
    nb = REL_BUCKETS // 2
    max_exact = nb // 2
    n = np.arange(1, 4 * REL_MAX_DIST)
    large = max_exact + (np.log(n / max_exact) / math.log(REL_MAX_DIST / max_exact)
                         * (nb - max_exact)).astype(np.int64)
    bucket = np.where(n < max_exact, n, np.minimum(large, nb - 1))
    return int(n[bucket < nb - 1].max())


def _bias_band(rel_bias, tq):
    width = 5 * tq
    period = 6 * tq - 1
    m = jnp.arange(period, dtype=jnp.int32)
    rel = jnp.where(m < width, m, m - period) - 2 * tq
    vec = rel_bias.astype(F32)[_t5_bucket(rel)].T * LOG2E
    rows = jnp.tile(vec, (1, tq))[:, :tq * (period - 1)].reshape(-1, tq, period - 1)
    return jnp.transpose(rows[:, :, :width], (0, 2, 1))


def _outproj_kernel(x_ref, yr_ref, y0_ref, y1_ref, bonus_ref, gate_ref, yd_ref, lnw_ref, lnb_ref,
                    w_ref, o_ref):
    avg = _head_avg_matrix()
    mixed = [yr_ref[...]]
    for p in range(RWKV_HEADS // 2):
        cols = slice(p * LANES, (p + 1) * LANES)
        y = _group_norm(y0_ref[:, cols] + y1_ref[:, cols], avg, lnw_ref[:, cols], lnb_ref[:, cols],
                        RWKV_GN_EPS)
        mixed.append(((y + bonus_ref[:, cols]) * gate_ref[:, cols]).astype(BF16))
    mixed.append(yd_ref[...])
    o_ref[...] = x_ref[...] + _dot(jnp.concatenate(mixed, axis=1), w_ref[...])


def _outproj(x2, y_ret, y0, y1, bonus, gate, y_diff, ln_w, ln_b, w_bf16, tm=512):
    m, d = x2.shape
    tm = min(tm, m)
    row = lambda n: pl.BlockSpec((tm, n), lambda i: (i, 0))
    vec = pl.BlockSpec((1, D_RWKV), lambda i: (0, 0))
    return pl.pallas_call(
        _outproj_kernel,
        grid=(m // tm,),
        in_specs=[row(d), row(D_RET), row(D_RWKV), row(D_RWKV), row(D_RWKV), row(D_RWKV), row(D_DIFF),
                  vec, vec, pl.BlockSpec(w_bf16.shape, lambda i: (0, 0))],
        out_specs=row(d),
        out_shape=jax.ShapeDtypeStruct((m, d), F32),
        compiler_params=_cparams(("parallel",)),
        name="outproj",
    )(x2, y_ret, y0, y1, bonus, gate, y_diff, ln_w.reshape(1, D_RWKV), ln_b.reshape(1, D_RWKV), w_bf16)


def _ffn_up_kernel(x_ref, g_ref, wg_ref, wu_ref, o_ref, *, tf):
    x = x_ref[...]
    ms = jnp.mean(x * x, axis=-1, keepdims=True)
    h = (x * lax.rsqrt(ms + NORM_EPS) * g_ref[...]).astype(BF16)
    for j in range(wg_ref.shape[1] // tf):
        cols = slice(j * tf, (j + 1) * tf)
        gate = _dot(h, wg_ref[:, cols])
        up = _dot(h, wu_ref[:, cols])
        o_ref[:, cols] = (gate * jax.nn.sigmoid(gate) * up).astype(BF16)


def _ffn_up(x2, g, wg_bf16, wu_bf16, tm=512, tf=256):
    m, d = x2.shape
    tm = min(tm, m)
    f = wg_bf16.shape[1]
    return pl.pallas_call(
        functools.partial(_ffn_up_kernel, tf=tf),
        grid=(m // tm,),
        in_specs=[pl.BlockSpec((tm, d), lambda i: (i, 0)),
                  pl.BlockSpec((1, d), lambda i: (0, 0)),
                  pl.BlockSpec((d, f), lambda i: (0, 0)),
                  pl.BlockSpec((d, f), lambda i: (0, 0))],
        out_specs=pl.BlockSpec((tm, f), lambda i: (i, 0)),
        out_shape=jax.ShapeDtypeStruct((m, f), BF16),
        compiler_params=_cparams(("parallel",)),
        name="ffn_up",
    )(x2, g.reshape(1, d), wg_bf16, wu_bf16)


def _ffn_down_kernel(x_ref, h_ref, w_ref, g_ref, o_ref, *, final_norm):
    y = x_ref[...] + _dot(h_ref[...], w_ref[...])
    if final_norm:
        ms = jnp.mean(y * y, axis=-1, keepdims=True)
        y = y * lax.rsqrt(ms + NORM_EPS) * g_ref[...]
    o_ref[...] = y


def _ffn_down(x2, hidden, w_bf16, g, final_norm, tm=512):
    m, d = x2.shape
    tm = min(tm, m)
    f = hidden.shape[1]
    return pl.pallas_call(
        functools.partial(_ffn_down_kernel, final_norm=final_norm),
        grid=(m // tm,),
        in_specs=[pl.BlockSpec((tm, d), lambda i: (i, 0)),
                  pl.BlockSpec((tm, f), lambda i: (i, 0)),
                  pl.BlockSpec((f, d), lambda i: (0, 0)),
                  pl.BlockSpec((1, d), lambda i: (0, 0))],
        out_specs=pl.BlockSpec((tm, d), lambda i: (i, 0)),
        out_shape=jax.ShapeDtypeStruct((m, d), F32),
        compiler_params=_cparams(("parallel",)),
        name="ffn_down",
    )(x2, hidden, w_bf16, g.reshape(1, d))


def _rope_tables(seq):
    half = HEAD_DIM // 2
    freqs = ROPE_BASE ** (-jnp.arange(half, dtype=F32) / half)
    ang = jnp.arange(seq, dtype=jnp.int32).astype(F32)[:, None] * freqs[None, :]
    cos = jnp.cos(ang)
    sin = jnp.sin(ang)
    cos_tab = jnp.tile(cos, (1, LANES // half))
    sin_tab = jnp.tile(jnp.concatenate([-sin, sin], axis=1), (1, LANES // HEAD_DIM))
    return cos_tab, sin_tab


def kernel(x, mix_norm_g, w_in, w_out, ret_gn_w, ret_gn_b, rwkv_mu, rwkv_w0, rwkv_w1, rwkv_w2, rwkv_a0, rwkv_a1, rwkv_a2, rwkv_g1, rwkv_g2, rwkv_k_k, rwkv_k_a, rwkv_r_k, rwkv_ln_w, rwkv_ln_b, diff_lambda, diff_subln_w, rel_bias, ffn_norm_g, w_gate, w_up, w_down, final_norm_g):
    batch, seq, d = x.shape
    depth = w_in.shape[0]
    cos_tab, sin_tab = _rope_tables(seq)
    band = _bias_band(rel_bias, min(ATTN_TQ, seq))
    x2 = x.reshape(batch * seq, d)
    for l in range(depth):
        proj = _norm_matmul(x2, mix_norm_g[l], w_in[l].astype(BF16))
        y_ret = _retention(proj, cos_tab, sin_tab, ret_gn_w[l], ret_gn_b[l], batch, seq)
        feats = _rwkv_prep(proj, rwkv_mu[l], rwkv_w0[l], rwkv_w1[l], rwkv_w2[l], rwkv_a0[l],
                           rwkv_a1[l], rwkv_a2[l], rwkv_g1[l], rwkv_g2[l], rwkv_k_k[l],
                           rwkv_k_a[l], rwkv_r_k[l].reshape(-1), batch, seq)
        y0, y1 = _rwkv_scan(feats[:9], batch, seq)
        lambda_init = 0.8 - 0.6 * math.exp(-0.3 * l)
        y_diff = _diff_attention(proj, band, diff_lambda[l], diff_subln_w[l], lambda_init,
                                 batch, seq)
        x2 = _outproj(x2, y_ret, y0, y1, feats[10], feats[9], y_diff, rwkv_ln_w[l], rwkv_ln_b[l],
                      w_out[l].astype(BF16))
        hidden = _ffn_up(x2, ffn_norm_g[l], w_gate[l].astype(BF16), w_up[l].astype(BF16))
        x2 = _ffn_down(x2, hidden, w_down[l].astype(BF16), final_norm_g, l == depth - 1)
    return x2.reshape(batch, seq, d)
```

```python
import functools
import math

import numpy as np
import jax
import jax.numpy as jnp
from jax import lax
from jax.experimental import pallas as pl
from jax.experimental.pallas import tpu as pltpu

F32 = jnp.float32
BF16 = jnp.bfloat16

D_MODEL = 1024
HEAD_DIM = 64
LANES = 128
BF16_ROWS = 16
LOG2E = math.log2(math.e)
N_HEADS = D_MODEL // HEAD_DIM
RET_HEADS = (3 * N_HEADS) // 8
RWKV_HEADS = (3 * N_HEADS) // 8
DIFF_HEADS = N_HEADS - RET_HEADS - RWKV_HEADS
D_RET = RET_HEADS * HEAD_DIM
D_RWKV = RWKV_HEADS * HEAD_DIM
D_DIFF = DIFF_HEADS * HEAD_DIM
DIFF_QK_DIM = HEAD_DIM // 2
D_IN = 4 * D_RET + 4 * D_RWKV + 3 * D_DIFF
ROPE_BASE = 10000.0
D_FF = -(-8 * D_MODEL // (3 * 256)) * 256
REL_BUCKETS = 32
REL_MAX_DIST = 128
NORM_EPS = 1e-6
RET_GN_EPS = 1e-5
RWKV_GN_EPS = 64e-5

RET_CHUNK = 128
RWKV_CHUNK = 64
RWKV_GROUP = 4
ATTN_TQ = 256
ATTN_BLOCKS = 4
VMEM_LIMIT = 56 * 1024 * 1024


def _cparams(sem):
    return pltpu.CompilerParams(dimension_semantics=sem, vmem_limit_bytes=VMEM_LIMIT)


def _dot(a, b):
    return jnp.dot(a, b, preferred_element_type=F32)


def _bdot(a, b):
    return _dot(a.astype(BF16), b.astype(BF16))


def _dot_nt(a, b):
    return lax.dot_general(a, b, (((1,), (1,)), ((), ())), preferred_element_type=F32)


def _dot_tn(a, b):
    return lax.dot_general(a, b, (((0,), (0,)), ((), ())), preferred_element_type=F32)


def _head_avg_matrix():
    r = lax.broadcasted_iota(jnp.int32, (LANES, LANES), 0) // HEAD_DIM
    c = lax.broadcasted_iota(jnp.int32, (LANES, LANES), 1) // HEAD_DIM
    return jnp.where(r == c, 1.0 / HEAD_DIM, 0.0).astype(BF16)


def _head_mean(x, avg):
    hi = x.astype(BF16)
    lo = (x - hi.astype(F32)).astype(BF16)
    return _dot(hi, avg) + _dot(lo, avg)


def _group_norm(y, avg, w, b, eps):
    yc = y - _head_mean(y, avg)
    var = _head_mean(yc * yc, avg)
    return yc * lax.rsqrt(var + eps) * w + b


def _norm_matmul_kernel(x_ref, g_ref, w_ref, o_ref):
    x = x_ref[...]
    ms = jnp.mean(x * x, axis=-1, keepdims=True)
    h = x * lax.rsqrt(ms + NORM_EPS) * g_ref[...]
    o_ref[...] = _dot(h.astype(BF16), w_ref[...])


def _norm_matmul(x2, g, w_bf16, tm=256):
    m, d = x2.shape
    n = w_bf16.shape[1]
    return pl.pallas_call(
        _norm_matmul_kernel,
        grid=(m // tm,),
        in_specs=[pl.BlockSpec((tm, d), lambda i: (i, 0)),
                  pl.BlockSpec((1, d), lambda i: (0, 0)),
                  pl.BlockSpec((d, n), lambda i: (0, 0))],
        out_specs=pl.BlockSpec((tm, n), lambda i: (i, 0)),
        out_shape=jax.ShapeDtypeStruct((m, n), F32),
        compiler_params=_cparams(("parallel",)),
        name="norm_inproj",
    )(x2, g.reshape(1, d), w_bf16)


def _retention_kernel(q_ref, k_ref, v_ref, g_ref, cos_ref, sin_ref, gnw_ref, gnb_ref, o_ref,
                      qs_ref, ks_ref, sf_ref, sb_ref, *, seq, chunk, log_gamma):
    nc = seq // chunk
    npair = RET_HEADS // 2
    lane = lax.broadcasted_iota(jnp.int32, (1, LANES), 1)
    half = HEAD_DIM // 2
    head_masks = (lane < HEAD_DIM, lane >= HEAD_DIM)
    qk_masks = ((lane // half) % 2 == 0, (lane // half) % 2 == 1)
    ri = lax.broadcasted_iota(jnp.int32, (LANES, LANES), 0)
    ci = lax.broadcasted_iota(jnp.int32, (LANES, LANES), 1)
    block_diag = ((ri // half) % 2) == (ci // HEAD_DIM)
    avg = _head_avg_matrix()
    ti = lax.broadcasted_iota(jnp.int32, (chunk, 2 * chunk), 0)
    tj = lax.broadcasted_iota(jnp.int32, (chunk, 2 * chunk), 1)
    dist = jnp.abs(ti - jnp.where(tj >= chunk, tj - chunk, tj)).astype(F32)
    pos = lax.broadcasted_iota(jnp.int32, (chunk, 1), 0).astype(F32)
    pairs = range(npair)
    cols = [slice(p * LANES, (p + 1) * LANES) for p in pairs]
    lg = [jnp.where(qk_masks[0], log_gamma[2 * p], log_gamma[2 * p + 1]).astype(F32) for p in pairs]
    lg_v = [jnp.where(head_masks[0], log_gamma[2 * p], log_gamma[2 * p + 1]).astype(F32) for p in pairs]
    decay = [jnp.exp(dist * jnp.where(tj >= chunk, log_gamma[2 * p + 1], log_gamma[2 * p])) for p in pairs]
    xi_f = [jnp.exp((pos + 1.0) * x) for x in lg]
    xi_b = [jnp.exp((chunk - pos) * x) for x in lg]
    zeta_f = [jnp.exp((chunk - 1.0 - pos) * x) for x in lg]
    zeta_b = [jnp.exp(pos * x) for x in lg]
    dec_c = [jnp.exp(chunk * x) for x in lg_v]

    def rope(x, cos, sin):
        return x * cos + pltpu.roll(x, HEAD_DIM, 1) * sin

    def blockdiag(x, masks):
        return jnp.concatenate([jnp.where(masks[0], x, 0.0), jnp.where(masks[1], x, 0.0)],
                               axis=0).astype(BF16)

    width = 2 if nc % 2 == 0 else 1

    def pass1(i, carry):
        work = [(i * width + w, p) for w in range(width) for p in pairs]
        rows = [pl.ds(pl.multiple_of(n * chunk, chunk), chunk) for n, _ in work]
        cos = [cos_ref[r, :] for r in rows]
        sin = [sin_ref[r, :] for r in rows]
        q = [rope(q_ref[r, cols[p]], c, s) * (HEAD_DIM ** -0.5) for r, c, s, (_, p) in zip(rows, cos, sin, work)]
        k = [rope(k_ref[r, cols[p]], c, s) for r, c, s, (_, p) in zip(rows, cos, sin, work)]
        kz = [jnp.concatenate([x * zeta_f[p], x * zeta_b[p]], axis=1).astype(BF16) for x, (_, p) in zip(k, work)]
        kv = [_dot_tn(x, v_ref[r, cols[p]].astype(BF16)) for x, r, (_, p) in zip(kz, rows, work)]
        for x, y, z, r, (n, p) in zip(q, k, kv, rows, work):
            qs_ref[r, cols[p]] = x
            ks_ref[r, cols[p]] = y
            sf_ref[n, p] = jnp.where(block_diag, z[:LANES], 0.0)
            sb_ref[n, p] = jnp.where(block_diag, z[LANES:], 0.0)
        return carry

    lax.fori_loop(0, nc // width, pass1, 0)

    def scan_states(i, states):
        left, right = states
        new_left, new_right = [], []
        for p in pairs:
            kv = sf_ref[i, p]
            sf_ref[i, p] = left[p]
            new_left.append(left[p] * dec_c[p] + kv)
            kv = sb_ref[nc - 1 - i, p]
            sb_ref[nc - 1 - i, p] = right[p]
            new_right.append(right[p] * dec_c[p] + kv)
        return tuple(new_left), tuple(new_right)

    zeros = (jnp.zeros((LANES, LANES), F32),) * npair
    lax.fori_loop(0, nc, scan_states, (zeros, zeros))

    def pass2(i, carry):
        work = [(i * width + w, p) for w in range(width) for p in pairs]
        rows = [pl.ds(pl.multiple_of(n * chunk, chunk), chunk) for n, _ in work]
        q = [qs_ref[r, cols[p]] for r, (_, p) in zip(rows, work)]
        k_bd = [blockdiag(ks_ref[r, cols[p]], qk_masks) for r, (_, p) in zip(rows, work)]
        v_bd = [blockdiag(v_ref[r, cols[p]], head_masks) for r, (_, p) in zip(rows, work)]
        s = [(_dot_nt(x.astype(BF16), y) * decay[p]).astype(BF16) for x, y, (_, p) in zip(q, k_bd, work)]
        qx = [jnp.concatenate([x * xi_f[p], x * xi_b[p]], axis=1).astype(BF16) for x, (_, p) in zip(q, work)]
        st = [jnp.concatenate([sf_ref[n, p], sb_ref[n, p]], axis=0).astype(BF16) for n, p in work]
        o = [_dot(x, y) + _dot(z, w) for x, y, z, w in zip(s, v_bd, qx, st)]
        mean = [_head_mean(x, avg) for x in o]
        centred = [x - m for x, m in zip(o, mean)]
        var = [_head_mean(x * x, avg) for x in centred]
        for x, vr, r, (_, p) in zip(centred, var, rows, work):
            y = x * lax.rsqrt(vr + RET_GN_EPS) * gnw_ref[:, cols[p]] + gnb_ref[:, cols[p]]
            g = g_ref[r, cols[p]]
            o_ref[r, cols[p]] = (g * jax.nn.sigmoid(g) * y).astype(BF16)
        return carry

    lax.fori_loop(0, nc // width, pass2, 0)


def _retention(proj, cos_tab, sin_tab, gn_w, gn_b, batch, seq):
    chunk = min(RET_CHUNK, seq)
    nc = seq // chunk
    log_gamma = tuple(float(np.log1p(-np.exp2(-5.0 - h))) for h in range(RET_HEADS))
    kern = functools.partial(_retention_kernel, seq=seq, chunk=chunk, log_gamma=log_gamma)
    col = lambda j: pl.BlockSpec((seq, D_RET), lambda b, j=j: (b, j))
    full = lambda shape: pl.BlockSpec(shape, lambda b: (0,) * len(shape))
    state = pltpu.VMEM((nc, RET_HEADS // 2, LANES, LANES), F32)
    return pl.pallas_call(
        kern,
        grid=(batch,),
        in_specs=[col(0), col(1), col(2), col(3), full((seq, LANES)), full((seq, LANES)),
                  full((1, D_RET)), full((1, D_RET))],
        out_specs=pl.BlockSpec((seq, D_RET), lambda b: (b, 0)),
        out_shape=jax.ShapeDtypeStruct((batch * seq, D_RET), BF16),
        scratch_shapes=[pltpu.VMEM((seq, D_RET), F32), pltpu.VMEM((seq, D_RET), F32), state, state],
        compiler_params=_cparams(("parallel",)),
        name="retention",
    )(proj, proj, proj, proj, cos_tab, sin_tab, gn_w.reshape(1, D_RET), gn_b.reshape(1, D_RET))


def _rwkv_prep_kernel(r_ref, k_ref, v_ref, z_ref, rp_ref, kp_ref, vp_ref, zp_ref,
                      rn_ref, kn_ref, vn_ref, zn_ref, mu_ref, w0_ref, w1_ref, w2_ref,
                      a0_ref, a1_ref, a2_ref, g1_ref, g2_ref, kk_ref, ka_ref, rk_ref,
                      lw0_o, lw1_o, kd0_o, kd1_o, bb0_o, bb1_o, v_o, r_o, kk_o, gate_o, bonus_o,
                      *, tb):
    i = pl.program_id(1)
    has_prev = (i > 0).astype(F32)
    has_next = (i < pl.num_programs(1) - 1).astype(F32)
    row = lax.broadcasted_iota(jnp.int32, (tb, 1), 0)
    avg = _head_avg_matrix()

    def head_sum(x):
        return jnp.concatenate([_head_mean(x[:, c * LANES:(c + 1) * LANES], avg)
                                for c in range(D_RWKV // LANES)], axis=1) * HEAD_DIM

    def shifted(f_ref, p_ref, n_ref):
        f = f_ref[...]
        prev = jnp.where(row == 0, p_ref[7:8, :] * has_prev, pltpu.roll(f, 1, 0))
        nxt = jnp.where(row == tb - 1, n_ref[0:1, :] * has_next, pltpu.roll(f, tb - 1, 0))
        return f, prev - f, nxt - f

    def mix(parts, idx):
        f, dp, dn = parts
        return f + mu_ref[idx, 0:1, :] * dp + mu_ref[idx, 1:2, :] * dn

    xr = mix(shifted(r_ref, rp_ref, rn_ref), 0)
    xk = mix(shifted(k_ref, kp_ref, kn_ref), 1)
    xv = mix(shifted(v_ref, vp_ref, vn_ref), 2)
    zparts = shifted(z_ref, zp_ref, zn_ref)
    xw = mix(zparts, 3)
    xa = mix(zparts, 4)
    xg = mix(zparts, 5)

    gate = _bdot(jax.nn.sigmoid(_bdot(xg, g1_ref[...])), g2_ref[...])
    w_lo = _bdot(jnp.tanh(_bdot(xw, w1_ref[...])), w2_ref[...])
    a_lo = _bdot(_bdot(xa, a1_ref[...]), a2_ref[...])
    kk = xk * kk_ref[...]
    kk = kk / jnp.maximum(jnp.sqrt(head_sum(kk * kk)), 1e-12)

    ksum = jnp.zeros_like(xk)
    outs = ((lw0_o, kd0_o, bb0_o), (lw1_o, kd1_o, bb1_o))
    for d in range(2):
        cols = slice(d * D_RWKV, (d + 1) * D_RWKV)
        w_raw = w0_ref[d:d + 1, :] + w_lo[:, cols]
        log_w = -math.exp(-0.5) * jax.nn.sigmoid(w_raw)
        a = jax.nn.sigmoid(a0_ref[d:d + 1, :] + a_lo[:, cols])
        k_dir = xk * (1.0 + (a - 1.0) * ka_ref[...])
        ksum = ksum + k_dir
        lw_o, kd_o, bb_o = outs[d]
        lw_o[...] = log_w
        kd_o[...] = k_dir.astype(BF16)
        bb_o[...] = (kk * a).astype(BF16)
    bonus = head_sum(xr * ksum * rk_ref[...])
    v_o[...] = xv.astype(BF16)
    r_o[...] = xr.astype(BF16)
    kk_o[...] = kk.astype(BF16)
    gate_o[...] = gate.astype(BF16)
    bonus_o[...] = (bonus * xv).astype(BF16)


def _rwkv_prep(proj, mu, w0, w1, w2, a0, a1, a2, g1, g2, k_k, k_a, r_k, batch, seq):
    tb = min(512, seq)
    nt = seq // tb
    d = D_RWKV
    col0 = (4 * D_RET) // d
    main = lambda j: pl.BlockSpec((tb, d), lambda b, i, j=j: (b * nt + i, col0 + j))
    prev = lambda j: pl.BlockSpec(
        (8, d), lambda b, i, j=j: (b * (seq // 8) + jnp.maximum(i * (tb // 8) - 1, 0), col0 + j))
    nxt = lambda j: pl.BlockSpec(
        (8, d), lambda b, i, j=j: (b * (seq // 8) + jnp.minimum((i + 1) * (tb // 8), seq // 8 - 1),
                                   col0 + j))
    full = lambda shape: pl.BlockSpec(shape, lambda b, i: (0,) * len(shape))
    out_spec = pl.BlockSpec((tb, d), lambda b, i: (b * nt + i, 0))
    out_sd = lambda dtype: jax.ShapeDtypeStruct((batch * seq, d), dtype)
    row = lambda a: a.reshape(1, d)

    def both_dirs(first, second):
        r = first.shape[-1]
        blk = jnp.zeros((2 * r, 2 * d), F32)
        blk = blk.at[:r, :d].set(second[0]).at[r:, d:].set(second[1])
        return jnp.concatenate([first[0], first[1]], axis=1).astype(BF16), blk.astype(BF16)

    w1, w2 = both_dirs(w1, w2)
    a1, a2 = both_dirs(a1, a2)
    g1, g2 = g1.astype(BF16), g2.astype(BF16)
    return pl.pallas_call(
        functools.partial(_rwkv_prep_kernel, tb=tb),
        grid=(batch, nt),
        in_specs=[main(0), main(1), main(2), main(3), prev(0), prev(1), prev(2), prev(3),
                  nxt(0), nxt(1), nxt(2), nxt(3),
                  full(mu.shape), full(w0.shape), full(w1.shape), full(w2.shape),
                  full(a0.shape), full(a1.shape), full(a2.shape), full(g1.shape), full(g2.shape),
                  full((1, d)), full((1, d)), full((1, d))],
        out_specs=[out_spec] * 11,
        out_shape=[out_sd(F32)] * 2 + [out_sd(BF16)] * 9,
        compiler_params=_cparams(("parallel", "parallel")),
        name="rwkv_features",
    )(*([proj] * 12), mu, w0, w1, w2, a0, a1, a2, g1, g2, row(k_k), row(k_a), row(r_k))


def _split3(x):
    hi = x.astype(BF16)
    rest = x - hi.astype(F32)
    mid = rest.astype(BF16)
    lo = (rest - mid.astype(F32)).astype(BF16)
    return hi, mid, lo


def _rwkv_scan_kernel(lw0_ref, kd0_ref, bb0_ref, v0_ref, r0_ref, kk0_ref,
                      lw1_ref, kd1_ref, bb1_ref, v1_ref, r1_ref, kk1_ref,
                      y0_ref, y1_ref, s_ref, gr_ref, u0_ref, mr_ref, vbd_ref, ev_ref, dec_ref,
                      *, chunk, group, nb):
    npair = RWKV_HEADS // 2
    assert 2 * chunk == LANES

    @pl.when(pl.program_id(1) == 0)
    def _():
        s_ref[...] = jnp.zeros_like(s_ref)

    lane = lax.broadcasted_iota(jnp.int32, (1, LANES), 1)
    halves = (lane < chunk, lane >= chunk)
    ri = lax.broadcasted_iota(jnp.int32, (LANES, LANES), 0)
    ci = lax.broadcasted_iota(jnp.int32, (LANES, LANES), 1)
    same_head = (ri // HEAD_DIM) == (ci // HEAD_DIM)
    ti = lax.broadcasted_iota(jnp.int32, (chunk, LANES), 0)
    tj = lax.broadcasted_iota(jnp.int32, (chunk, LANES), 1) % chunk
    eye = (ti == tj).astype(F32)
    strict = (tj < ti, tj > ti)
    incl = (tj <= ti, tj >= ti)
    si = lax.broadcasted_iota(jnp.int32, (chunk, chunk), 0)
    sj = lax.broadcasted_iota(jnp.int32, (chunk, chunk), 1)
    cum_mat = ((sj <= si).astype(BF16), (sj >= si).astype(BF16))
    ins = ((lw0_ref, kd0_ref, bb0_ref, v0_ref, r0_ref, kk0_ref),
           (lw1_ref, kd1_ref, bb1_ref, v1_ref, r1_ref, kk1_ref))
    outs = (y0_ref, y1_ref)
    items = [(c, bb, d, p) for c in range(group) for bb in range(nb) for p in range(npair) for d in range(2)]
    rows = lambda c, d: slice((c if d == 0 else group - 1 - c) * chunk,
                              ((c if d == 0 else group - 1 - c) + 1) * chunk)

    def blockdiag(x):
        return jnp.concatenate([jnp.where(halves[0], x, 0.0), jnp.where(halves[1], x, 0.0)],
                               axis=0).astype(BF16)

    feats = {}
    for c, bb, d in [(c, bb, d) for c in range(group) for bb in range(nb) for d in range(2)]:
        lw_ref, kd_ref, bb_ref, v_ref, r_ref, kk_ref = ins[d]
        rs = rows(c, d)
        lw = lw_ref[bb, rs, :]
        cum_in = sum(_dot(cum_mat[d], part) for part in _split3(lw))
        total = jnp.sum(lw, axis=0, keepdims=True)
        kd = kd_ref[bb, rs, :].astype(F32)
        bv = bb_ref[bb, rs, :].astype(F32)
        inv_p = jnp.exp(-cum_in)
        to_end = jnp.exp(total - cum_in)
        feats[c, bb, d] = dict(a=-kk_ref[bb, rs, :].astype(F32) * jnp.exp(cum_in - lw),
                               r=r_ref[bb, rs, :].astype(F32) * jnp.exp(cum_in),
                               b=bv * inv_p, k=kd * inv_p, v=v_ref[bb, rs, :].astype(F32),
                               bend=bv * to_end, kend=kd * to_end, dec=jnp.exp(total))

    def per_item(name):
        return [feats[c, bb, d][name][:, p * LANES:(p + 1) * LANES] for c, bb, d, p in items]

    a = per_item("a")
    v = per_item("v")
    v_bd = [blockdiag(x) for x in v]
    ar = [jnp.concatenate(pair, axis=0).astype(BF16) for pair in zip(a, per_item("r"))]
    bk = [jnp.concatenate([blockdiag(x), blockdiag(y)], axis=0)
          for x, y in zip(per_item("b"), per_item("k"))]
    gram = [_dot_nt(x, y) for x, y in zip(ar, bk)]
    l_ab = [jnp.where(strict[d], g[:chunk, :LANES], 0.0) for (_, _, d, _), g in zip(items, gram)]
    l_ak = [jnp.where(strict[d], g[:chunk, LANES:], 0.0).astype(BF16)
            for (_, _, d, _), g in zip(items, gram)]
    for n, ((_, _, d, _), g) in enumerate(zip(items, gram)):
        mr_ref[n] = jnp.where(jnp.concatenate([incl[d], incl[d]], axis=1), g[chunk:, :], 0.0).astype(BF16)
    lv = [_dot(x, y) for x, y in zip(l_ak, v_bd)]

    power = [_dot(x.astype(BF16), blockdiag(x)) for x in l_ab]
    inv = [eye + x for x in l_ab]
    rounds = int(math.log2(chunk)) - 1
    for j in range(rounds):
        pd = [blockdiag(x) for x in power]
        if j == rounds - 1:
            inv = [t + _dot(t.astype(BF16), x) for t, x in zip(inv, pd)]
        else:
            both = [_dot(jnp.concatenate([x, t], axis=0).astype(BF16), y)
                    for x, t, y in zip(power, inv, pd)]
            power = [x[:chunk] for x in both]
            inv = [t + x[chunk:] for t, x in zip(inv, both)]
    ta = [_dot(t.astype(BF16), jnp.concatenate([blockdiag(x), blockdiag(y)], axis=1))
          for t, x, y in zip(inv, a, lv)]
    for n, (c, bb, d, p) in enumerate(items):
        cols = slice(p * LANES, (p + 1) * LANES)
        f = feats[c, bb, d]
        gr_ref[n] = jnp.concatenate([ta[n][:, :LANES].astype(BF16), ar[n][chunk:]], axis=0)
        u0_ref[n] = ta[n][:, LANES:]
        vbd_ref[n] = v_bd[n]
        ev_ref[n] = jnp.concatenate([f["bend"][:, cols], f["kend"][:, cols], v[n]], axis=0).astype(BF16)
        dec_ref[n] = jnp.broadcast_to(f["dec"][:, cols], (8, LANES))

    per_chunk = 2 * npair * nb
    for c in range(group):
        ns = range(c * per_chunk, (c + 1) * per_chunk)
        states = [s_ref[j] for j in range(per_chunk)]
        prod = [_dot_nt(gr_ref[n], s.astype(BF16)) for n, s in zip(ns, states)]
        u = [x[:chunk] + u0_ref[n] for n, x in zip(ns, prod)]
        y = [x[chunk:] + _dot(mr_ref[n], jnp.concatenate([blockdiag(w), vbd_ref[n]], axis=0))
             for n, x, w in zip(ns, prod, u)]
        upd = [_dot_tn(jnp.concatenate([w.astype(BF16), ev_ref[n, 2 * chunk:, :]], axis=0),
                       ev_ref[n, :2 * chunk, :]) for n, w in zip(ns, u)]
        for j, n in enumerate(ns):
            _, bb, d, p = items[n]
            outs[d][bb, rows(c, d), p * LANES:(p + 1) * LANES] = y[j]
            s_ref[j] = states[j] * dec_ref[n, 0:1, :] + jnp.where(same_head, upd[j], 0.0)


def _rwkv_scan(feats, batch, seq):
    chunk = min(RWKV_CHUNK, seq)
    group = min(RWKV_GROUP, seq // chunk)
    tb = chunk * group
    nt = seq // tb
    nb = 2 if batch % 2 == 0 else 1
    n_items = group * RWKV_HEADS * nb
    fwd = pl.BlockSpec((nb, tb, D_RWKV), lambda b, i: (b, i, 0))
    bwd = pl.BlockSpec((nb, tb, D_RWKV), lambda b, i: (b, nt - 1 - i, 0))
    out_sd = jax.ShapeDtypeStruct((batch, seq, D_RWKV), F32)
    lw0, lw1, kd0, kd1, bb0, bb1, xv, xr, kk = [x.reshape(batch, seq, D_RWKV) for x in feats]
    y0, y1 = pl.pallas_call(
        functools.partial(_rwkv_scan_kernel, chunk=chunk, group=group, nb=nb),
        grid=(batch // nb, nt),
        in_specs=[fwd] * 6 + [bwd] * 6,
        out_specs=[fwd, bwd],
        out_shape=[out_sd, out_sd],
        scratch_shapes=[pltpu.VMEM((RWKV_HEADS * nb, LANES, LANES), F32),
                        pltpu.VMEM((n_items, 2 * chunk, LANES), BF16),
                        pltpu.VMEM((n_items, chunk, LANES), F32),
                        pltpu.VMEM((n_items, chunk, 4 * chunk), BF16),
                        pltpu.VMEM((n_items, 2 * chunk, LANES), BF16),
                        pltpu.VMEM((n_items, 3 * chunk, LANES), BF16),
                        pltpu.VMEM((n_items, 8, LANES), F32)],
        compiler_params=_cparams(("parallel", "arbitrary")),
        name="rwkv_scan",
    )(lw0, kd0, bb0, xv, xr, kk, lw1, kd1, bb1, xv, xr, kk)
    return y0.reshape(batch * seq, D_RWKV), y1.reshape(batch * seq, D_RWKV)


def _diff_attn_kernel(q_ref, k_ref, v_ref, band_ref, lam_ref, subw_ref, o_ref, kb_ref, vt_ref,
                      acc_ref, s_ref, *, seq, tq, tk, nblk, lambda_init):
    i = pl.program_id(2)

    @pl.when(i == 0)
    def _():
        kb_ref[...] = k_ref[...].astype(BF16)
        vt_ref[...] = jnp.transpose(v_ref[...]).astype(BF16)

    lane = lax.broadcasted_iota(jnp.int32, (1, LANES), 1)
    lv = lam_ref[...]
    lam = (jnp.exp(jnp.sum(lv[0:1] * lv[1:2], axis=1, keepdims=True))
           - jnp.exp(jnp.sum(lv[2:3] * lv[3:4], axis=1, keepdims=True)) + lambda_init)
    def one_block(blk):
        qi = i * nblk + blk
        q = q_ref[blk * tq:(blk + 1) * tq, :] * (DIFF_QK_DIM ** -0.5 * LOG2E)
        ones_rows = jnp.ones((BF16_ROWS, tk), BF16)
        n_maps = 4
        qms = [jnp.where((lane >= m * DIFF_QK_DIM) & (lane < (m + 1) * DIFF_QK_DIM), q, 0.0).astype(BF16)
               for m in range(n_maps)]
        n_tiles = seq // tk
        acc_ref[blk] = jnp.zeros(acc_ref.shape[1:], F32)

        def tile_index(j):
            t = qi + j
            return jnp.where(t >= n_tiles, t - n_tiles, t)

        def stage_scores(j, m):
            start = pl.multiple_of(tile_index(j) * tk, tk)
            s_ref[blk, j % 2, m] = _dot_nt(kb_ref[pl.ds(start, tk), :], qms[m])

        m_run = [jnp.full((1, tq), -1e30, F32)] * n_maps
        for m in range(n_maps):
            stage_scores(0, m)
        for j in range(n_tiles):
            slot = j % 2
            t = tile_index(j)
            start = pl.multiple_of(t * tk, tk)
            far = 2 <= j <= n_tiles - 2
            rows = BF16_ROWS // 2 if far else tk
            band_start = pl.multiple_of((jnp.clip(t - qi, -2, 2) + 2) * tk, tk)
            bias = [band_ref[hh, pl.ds(band_start, rows), :] for hh in range(2)]
            vt = [jnp.concatenate([vt_ref[hh * HEAD_DIM:(hh + 1) * HEAD_DIM, pl.ds(start, tk)],
                                   ones_rows], axis=0) for hh in range(2)]
            for m in range(n_maps):
                if j + 1 < n_tiles:
                    stage_scores(j + 1, m)
                if far:
                    s = s_ref[blk, slot, m]
                    const = bias[m // 2][0:1, :]
                    m_new = jnp.maximum(m_run[m], jnp.max(s, axis=0, keepdims=True) + const)
                    e = jnp.exp2(s - (m_new - const)).astype(BF16)
                else:
                    s = s_ref[blk, slot, m] + bias[m // 2]
                    m_new = jnp.maximum(m_run[m], jnp.max(s, axis=0, keepdims=True))
                    e = jnp.exp2(s - m_new).astype(BF16)
                pv = _dot(vt[m // 2], e)
                acc_ref[blk, m] = acc_ref[blk, m] * jnp.exp2(m_run[m] - m_new) + pv
                m_run[m] = m_new
        parts = [acc_ref[blk, m, :HEAD_DIM, :] / acc_ref[blk, m, HEAD_DIM:HEAD_DIM + 1, :] for m in range(n_maps)]
        heads = []
        for hh in range(2):
            head = parts[2 * hh] - lam * parts[2 * hh + 1]
            ms = jnp.mean(head * head, axis=0, keepdims=True)
            heads.append(head * lax.rsqrt(ms + NORM_EPS))
        out = jnp.transpose(jnp.concatenate(heads, axis=0))
        o_ref[blk * tq:(blk + 1) * tq, :] = (out * subw_ref[...] * (1.0 - lambda_init)).astype(BF16)

    for blk in range(nblk):
        one_block(blk)


def _diff_attention(proj, band, lam_vecs, subln_w, lambda_init, batch, seq):
    tq = min(ATTN_TQ, seq)
    nq = seq // tq
    nblk = ATTN_BLOCKS if nq % ATTN_BLOCKS == 0 else 1
    nsteps = nq // nblk
    npair = DIFF_HEADS // 2
    col0 = (4 * D_RET + 4 * D_RWKV) // LANES
    width = band.shape[1]
    tk = tq
    assert 2 * _bucket_saturation_distance() < tk
    return pl.pallas_call(
        functools.partial(_diff_attn_kernel, seq=seq, tq=tq, tk=tk, nblk=nblk, lambda_init=lambda_init),
        grid=(batch, npair, nsteps),
        in_specs=[pl.BlockSpec((nblk * tq, LANES), lambda b, p, i: (b * nsteps + i, col0 + p)),
                  pl.BlockSpec((seq, LANES), lambda b, p, i: (b, col0 + npair + p)),
                  pl.BlockSpec((seq, LANES), lambda b, p, i: (b, col0 + 2 * npair + p)),
                  pl.BlockSpec((2, width, tq), lambda b, p, i: (p, 0, 0)),
                  pl.BlockSpec((4, DIFF_QK_DIM), lambda b, p, i: (0, 0)),
                  pl.BlockSpec((1, LANES), lambda b, p, i: (0, 0))],
        out_specs=pl.BlockSpec((nblk * tq, LANES), lambda b, p, i: (b * nsteps + i, p)),
        out_shape=jax.ShapeDtypeStruct((batch * seq, D_DIFF), BF16),
        scratch_shapes=[pltpu.VMEM((seq, LANES), BF16), pltpu.VMEM((LANES, seq), BF16),
                        pltpu.VMEM((nblk, 4, HEAD_DIM + BF16_ROWS, tq), F32),
                        pltpu.VMEM((nblk, 2, 4, tk, tq), F32)],
        compiler_params=_cparams(("parallel", "parallel", "arbitrary")),
        name="diff_attention",
    )(proj, proj, proj, band, lam_vecs, jnp.tile(subln_w, 2).reshape(1, LANES))


def _t5_bucket(rel):
    nb = REL_BUCKETS // 2
    max_exact = nb // 2
    n = jnp.abs(rel)
    nf = jnp.maximum(n, 1).astype(jnp.float32)
    large = max_exact + (jnp.log(nf / max_exact) / math.log(REL_MAX_DIST / max_exact)
                         * (nb - max_exact)).astype(jnp.int32)
    large = jnp.minimum(large, nb - 1)
    return jnp.where(rel > 0, nb, 0) + jnp.where(n < max_exact, n, large)


def _bucket_saturation_distance():
    nb = REL_BUCKETS // 2
    max_exact = nb // 2
    n = np.arange(1, 4 * REL_MAX_DIST)
    large = max_exact + (np.log(n / max_exact) / math.log(REL_MAX_DIST / max_exact)
                         * (nb - max_exact)).astype(np.int64)
    bucket = np.where(n < max_exact, n, np.minimum(large, nb - 1))
    return int(n[bucket < nb - 1].max())


def _bias_band(rel_bias, tq):
    width = 5 * tq
    period = 6 * tq - 1
    m = jnp.arange(period, dtype=jnp.int32)
    rel = jnp.where(m < width, m, m - period) - 2 * tq
    vec = rel_bias.astype(F32)[_t5_bucket(rel)].T * LOG2E
    rows = jnp.tile(vec, (1, tq))[:, :tq * (period - 1)].reshape(-1, tq, period - 1)
    return jnp.transpose(rows[:, :, :width], (0, 2, 1))


def _outproj_kernel(x_ref, yr_ref, y0_ref, y1_ref, bonus_ref, gate_ref, yd_ref, lnw_ref, lnb_ref,
                    w_ref, o_ref):
    avg = _head_avg_matrix()
    mixed = [yr_ref[...]]
    for p in range(RWKV_HEADS // 2):
        cols = slice(p * LANES, (p + 1) * LANES)
        y = _group_norm(y0_ref[:, cols] + y1_ref[:, cols], avg, lnw_ref[:, cols], lnb_ref[:, cols],
                        RWKV_GN_EPS)
        mixed.append(((y + bonus_ref[:, cols]) * gate_ref[:, cols]).astype(BF16))
    mixed.append(yd_ref[...])
    o_ref[...] = x_ref[...] + _dot(jnp.concatenate(mixed, axis=1), w_ref[...])


def _outproj(x2, y_ret, y0, y1, bonus, gate, y_diff, ln_w, ln_b, w_bf16, tm=512):
    m, d = x2.shape
    tm = min(tm, m)
    row = lambda n: pl.BlockSpec((tm, n), lambda i: (i, 0))
    vec = pl.BlockSpec((1, D_RWKV), lambda i: (0, 0))
    return pl.pallas_call(
        _outproj_kernel,
        grid=(m // tm,),
        in_specs=[row(d), row(D_RET), row(D_RWKV), row(D_RWKV), row(D_RWKV), row(D_RWKV), row(D_DIFF),
                  vec, vec, pl.BlockSpec(w_bf16.shape, lambda i: (0, 0))],
        out_specs=row(d),
        out_shape=jax.ShapeDtypeStruct((m, d), F32),
        compiler_params=_cparams(("parallel",)),
        name="outproj",
    )(x2, y_ret, y0, y1, bonus, gate, y_diff, ln_w.reshape(1, D_RWKV), ln_b.reshape(1, D_RWKV), w_bf16)


def _ffn_up_kernel(x_ref, g_ref, wg_ref, wu_ref, o_ref, *, tf):
    x = x_ref[...]
    ms = jnp.mean(x * x, axis=-1, keepdims=True)
    h = (x * lax.rsqrt(ms + NORM_EPS) * g_ref[...]).astype(BF16)
    for j in range(wg_ref.shape[1] // tf):
        cols = slice(j * tf, (j + 1) * tf)
        gate = _dot(h, wg_ref[:, cols])
        up = _dot(h, wu_ref[:, cols])
        o_ref[:, cols] = (gate * jax.nn.sigmoid(gate) * up).astype(BF16)


def _ffn_up(x2, g, wg_bf16, wu_bf16, tm=512, tf=256):
    m, d = x2.shape
    tm = min(tm, m)
    f = wg_bf16.shape[1]
    return pl.pallas_call(
        functools.partial(_ffn_up_kernel, tf=tf),
        grid=(m // tm,),
        in_specs=[pl.BlockSpec((tm, d), lambda i: (i, 0)),
                  pl.BlockSpec((1, d), lambda i: (0, 0)),
                  pl.BlockSpec((d, f), lambda i: (0, 0)),
                  pl.BlockSpec((d, f), lambda i: (0, 0))],
        out_specs=pl.BlockSpec((tm, f), lambda i: (i, 0)),
        out_shape=jax.ShapeDtypeStruct((m, f), BF16),
        compiler_params=_cparams(("parallel",)),
        name="ffn_up",
    )(x2, g.reshape(1, d), wg_bf16, wu_bf16)


def _ffn_down_kernel(x_ref, h_ref, w_ref, g_ref, o_ref, *, final_norm):
    y = x_ref[...] + _dot(h_ref[...], w_ref[...])
    if final_norm:
        ms = jnp.mean(y * y, axis=-1, keepdims=True)
        y = y * lax.rsqrt(ms + NORM_EPS) * g_ref[...]
    o_ref[...] = y


def _ffn_down(x2, hidden, w_bf16, g, final_norm, tm=512):
    m, d = x2.shape
    tm = min(tm, m)
    f = hidden.shape[1]
    return pl.pallas_call(
        functools.partial(_ffn_down_kernel, final_norm=final_norm),
        grid=(m // tm,),
        in_specs=[pl.BlockSpec((tm, d), lambda i: (i, 0)),
                  pl.BlockSpec((tm, f), lambda i: (i, 0)),
                  pl.BlockSpec((f, d), lambda i: (0, 0)),
                  pl.BlockSpec((1, d), lambda i: (0, 0))],
        out_specs=pl.BlockSpec((tm, d), lambda i: (i, 0)),
        out_shape=jax.ShapeDtypeStruct((m, d), F32),
        compiler_params=_cparams(("parallel",)),
        name="ffn_down",
    )(x2, hidden, w_bf16, g.reshape(1, d))


def _rope_tables(seq):
    half = HEAD_DIM // 2
    freqs = ROPE_BASE ** (-jnp.arange(half, dtype=F32) / half)
    ang = jnp.arange(seq, dtype=jnp.int32).astype(F32)[:, None] * freqs[None, :]
    cos = jnp.cos(ang)
    sin = jnp.sin(ang)
    cos_tab = jnp.tile(cos, (1, LANES // half))
    sin_tab = jnp.concatenate([-sin, -sin, sin, sin], axis=1)
    return cos_tab, sin_tab


def _retention_qk_layout(w):
    d = w.shape[0]
    half = HEAD_DIM // 2
    qk = w[:, :2 * D_RET].reshape(d, 2, RET_HEADS // 2, 2, 2, half)
    qk = jnp.swapaxes(qk, 3, 4).reshape(d, 2 * D_RET)
    return jnp.concatenate([qk, w[:, 2 * D_RET:]], axis=1)


def kernel(x, mix_norm_g, w_in, w_out, ret_gn_w, ret_gn_b, rwkv_mu, rwkv_w0, rwkv_w1, rwkv_w2, rwkv_a0, rwkv_a1, rwkv_a2, rwkv_g1, rwkv_g2, rwkv_k_k, rwkv_k_a, rwkv_r_k, rwkv_ln_w, rwkv_ln_b, diff_lambda, diff_subln_w, rel_bias, ffn_norm_g, w_gate, w_up, w_down, final_norm_g):
    batch, seq, d = x.shape
    depth = w_in.shape[0]
    cos_tab, sin_tab = _rope_tables(seq)
    band = _bias_band(rel_bias, min(ATTN_TQ, seq))
    x2 = x.reshape(batch * seq, d)
    for l in range(depth):
        proj = _norm_matmul(x2, mix_norm_g[l], _retention_qk_layout(w_in[l]).astype(BF16))
        y_ret = _retention(proj, cos_tab, sin_tab, ret_gn_w[l], ret_gn_b[l], batch, seq)
        feats = _rwkv_prep(proj, rwkv_mu[l], rwkv_w0[l], rwkv_w1[l], rwkv_w2[l], rwkv_a0[l],
                           rwkv_a1[l], rwkv_a2[l], rwkv_g1[l], rwkv_g2[l], rwkv_k_k[l],
                           rwkv_k_a[l], rwkv_r_k[l].reshape(-1), batch, seq)
        y0, y1 = _rwkv_scan(feats[:9], batch, seq)
        lambda_init = 0.8 - 0.6 * math.exp(-0.3 * l)
        y_diff = _diff_attention(proj, band, diff_lambda[l], diff_subln_w[l], lambda_init,
                                 batch, seq)
        x2 = _outproj(x2, y_ret, y0, y1, feats[10], feats[9], y_diff, rwkv_ln_w[l], rwkv_ln_b[l],
                      w_out[l].astype(BF16))
        hidden = _ffn_up(x2, ffn_norm_g[l], w_gate[l].astype(BF16), w_up[l].astype(BF16))
        x2 = _ffn_down(x2, hidden, w_down[l].astype(BF16), final_norm_g, l == depth - 1)
    return x2.reshape(batch, seq, d)
```

```python
import functools
import math

import numpy as np
import jax
import jax.numpy as jnp
from jax import lax
from jax.experimental import pallas as pl
from jax.experimental.pallas import tpu as pltpu

F32 = jnp.float32
BF16 = jnp.bfloat16

D_MODEL = 1024
HEAD_DIM = 64
LANES = 128
BF16_ROWS = 16
LOG2E = math.log2(math.e)
N_HEADS = D_MODEL // HEAD_DIM
RET_HEADS = (3 * N_HEADS) // 8
RWKV_HEADS = (3 * N_HEADS) // 8
DIFF_HEADS = N_HEADS - RET_HEADS - RWKV_HEADS
D_RET = RET_HEADS * HEAD_DIM
D_RWKV = RWKV_HEADS * HEAD_DIM
D_DIFF = DIFF_HEADS * HEAD_DIM
DIFF_QK_DIM = HEAD_DIM // 2
D_IN = 4 * D_RET + 4 * D_RWKV + 3 * D_DIFF
ROPE_BASE = 10000.0
D_FF = -(-8 * D_MODEL // (3 * 256)) * 256
REL_BUCKETS = 32
REL_MAX_DIST = 128
NORM_EPS = 1e-6
RET_GN_EPS = 1e-5
RWKV_GN_EPS = 64e-5

RET_CHUNK = 128
RWKV_CHUNK = 64
RWKV_GROUP = 4
ATTN_TQ = 256
ATTN_BLOCKS = 8
VMEM_LIMIT = 56 * 1024 * 1024


def _cparams(sem):
    return pltpu.CompilerParams(dimension_semantics=sem, vmem_limit_bytes=VMEM_LIMIT)


def _dot(a, b):
    return jnp.dot(a, b, preferred_element_type=F32)


def _bdot(a, b):
    return _dot(a.astype(BF16), b.astype(BF16))


def _dot_nt(a, b):
    return lax.dot_general(a, b, (((1,), (1,)), ((), ())), preferred_element_type=F32)


def _dot_tn(a, b):
    return lax.dot_general(a, b, (((0,), (0,)), ((), ())), preferred_element_type=F32)


def _head_avg_matrix():
    r = lax.broadcasted_iota(jnp.int32, (LANES, LANES), 0) // HEAD_DIM
    c = lax.broadcasted_iota(jnp.int32, (LANES, LANES), 1) // HEAD_DIM
    return jnp.where(r == c, 1.0 / HEAD_DIM, 0.0).astype(BF16)


def _head_mean(x, avg):
    hi = x.astype(BF16)
    lo = (x - hi.astype(F32)).astype(BF16)
    return _dot(hi, avg) + _dot(lo, avg)


def _group_norm(y, avg, w, b, eps):
    yc = y - _head_mean(y, avg)
    var = _head_mean(yc * yc, avg)
    return yc * lax.rsqrt(var + eps) * w + b


def _norm_matmul_kernel(x_ref, g_ref, w_ref, o_ref):
    x = x_ref[...]
    ms = jnp.mean(x * x, axis=-1, keepdims=True)
    h = x * lax.rsqrt(ms + NORM_EPS) * g_ref[...]
    o_ref[...] = _dot(h.astype(BF16), w_ref[...])


def _norm_matmul(x2, g, w_bf16, tm=512):
    m, d = x2.shape
    n = w_bf16.shape[1]
    return pl.pallas_call(
        _norm_matmul_kernel,
        grid=(m // tm,),
        in_specs=[pl.BlockSpec((tm, d), lambda i: (i, 0)),
                  pl.BlockSpec((1, d), lambda i: (0, 0)),
                  pl.BlockSpec((d, n), lambda i: (0, 0))],
        out_specs=pl.BlockSpec((tm, n), lambda i: (i, 0)),
        out_shape=jax.ShapeDtypeStruct((m, n), F32),
        compiler_params=_cparams(("parallel",)),
        name="norm_inproj",
    )(x2, g.reshape(1, d), w_bf16)


def _retention_kernel(q_ref, k_ref, v_ref, g_ref, cos_ref, sin_ref, gnw_ref, gnb_ref, o_ref,
                      qs_ref, ks_ref, sf_ref, sb_ref, *, seq, chunk, log_gamma):
    nc = seq // chunk
    npair = RET_HEADS // 2
    lane = lax.broadcasted_iota(jnp.int32, (1, LANES), 1)
    half = HEAD_DIM // 2
    head_masks = (lane < HEAD_DIM, lane >= HEAD_DIM)
    qk_masks = ((lane // half) % 2 == 0, (lane // half) % 2 == 1)
    ri = lax.broadcasted_iota(jnp.int32, (LANES, LANES), 0)
    ci = lax.broadcasted_iota(jnp.int32, (LANES, LANES), 1)
    block_diag = ((ri // half) % 2) == (ci // HEAD_DIM)
    avg = _head_avg_matrix()
    ti = lax.broadcasted_iota(jnp.int32, (chunk, 2 * chunk), 0)
    tj = lax.broadcasted_iota(jnp.int32, (chunk, 2 * chunk), 1)
    dist = jnp.abs(ti - jnp.where(tj >= chunk, tj - chunk, tj)).astype(F32)
    pos = lax.broadcasted_iota(jnp.int32, (chunk, 1), 0).astype(F32)
    pairs = range(npair)
    cols = [slice(p * LANES, (p + 1) * LANES) for p in pairs]
    lg = [jnp.where(qk_masks[0], log_gamma[2 * p], log_gamma[2 * p + 1]).astype(F32) for p in pairs]
    lg_v = [jnp.where(head_masks[0], log_gamma[2 * p], log_gamma[2 * p + 1]).astype(F32) for p in pairs]
    decay = [jnp.exp(dist * jnp.where(tj >= chunk, log_gamma[2 * p + 1], log_gamma[2 * p])) for p in pairs]
    xi_f = [jnp.exp((pos + 1.0) * x) for x in lg]
    xi_b = [jnp.exp((chunk - pos) * x) for x in lg]
    zeta_f = [jnp.exp((chunk - 1.0 - pos) * x) for x in lg]
    zeta_b = [jnp.exp(pos * x) for x in lg]
    dec_c = [jnp.exp(chunk * x) for x in lg_v]

    def rope(x, cos, sin):
        return x * cos + pltpu.roll(x, HEAD_DIM, 1) * sin

    def blockdiag(x, masks):
        return jnp.concatenate([jnp.where(masks[0], x, 0.0), jnp.where(masks[1], x, 0.0)],
                               axis=0).astype(BF16)

    width = 2 if nc % 2 == 0 else 1

    def pass1(i, carry):
        work = [(i * width + w, p) for w in range(width) for p in pairs]
        rows = [pl.ds(pl.multiple_of(n * chunk, chunk), chunk) for n, _ in work]
        cos = [cos_ref[r, :] for r in rows]
        sin = [sin_ref[r, :] for r in rows]
        q = [rope(q_ref[r, cols[p]], c, s) * (HEAD_DIM ** -0.5) for r, c, s, (_, p) in zip(rows, cos, sin, work)]
        k = [rope(k_ref[r, cols[p]], c, s) for r, c, s, (_, p) in zip(rows, cos, sin, work)]
        kz = [jnp.concatenate([x * zeta_f[p], x * zeta_b[p]], axis=1).astype(BF16) for x, (_, p) in zip(k, work)]
        kv = [_dot_tn(x, v_ref[r, cols[p]].astype(BF16)) for x, r, (_, p) in zip(kz, rows, work)]
        for x, y, z, r, (n, p) in zip(q, k, kv, rows, work):
            qs_ref[r, cols[p]] = x
            ks_ref[r, cols[p]] = y
            sf_ref[n, p] = jnp.where(block_diag, z[:LANES], 0.0)
            sb_ref[n, p] = jnp.where(block_diag, z[LANES:], 0.0)
        return carry

    lax.fori_loop(0, nc // width, pass1, 0)

    def scan_states(i, states):
        left, right = states
        new_left, new_right = [], []
        for p in pairs:
            kv = sf_ref[i, p]
            sf_ref[i, p] = left[p]
            new_left.append(left[p] * dec_c[p] + kv)
            kv = sb_ref[nc - 1 - i, p]
            sb_ref[nc - 1 - i, p] = right[p]
            new_right.append(right[p] * dec_c[p] + kv)
        return tuple(new_left), tuple(new_right)

    zeros = (jnp.zeros((LANES, LANES), F32),) * npair
    lax.fori_loop(0, nc, scan_states, (zeros, zeros))

    def pass2(i, carry):
        work = [(i * width + w, p) for w in range(width) for p in pairs]
        rows = [pl.ds(pl.multiple_of(n * chunk, chunk), chunk) for n, _ in work]
        q = [qs_ref[r, cols[p]] for r, (_, p) in zip(rows, work)]
        k_bd = [blockdiag(ks_ref[r, cols[p]], qk_masks) for r, (_, p) in zip(rows, work)]
        v_bd = [blockdiag(v_ref[r, cols[p]], head_masks) for r, (_, p) in zip(rows, work)]
        s = [(_dot_nt(x.astype(BF16), y) * decay[p]).astype(BF16) for x, y, (_, p) in zip(q, k_bd, work)]
        qx = [jnp.concatenate([x * xi_f[p], x * xi_b[p]], axis=1).astype(BF16) for x, (_, p) in zip(q, work)]
        st = [jnp.concatenate([sf_ref[n, p], sb_ref[n, p]], axis=0).astype(BF16) for n, p in work]
        o = [_dot(x, y) + _dot(z, w) for x, y, z, w in zip(s, v_bd, qx, st)]
        mean = [_head_mean(x, avg) for x in o]
        centred = [x - m for x, m in zip(o, mean)]
        var = [_head_mean(x * x, avg) for x in centred]
        for x, vr, r, (_, p) in zip(centred, var, rows, work):
            y = x * lax.rsqrt(vr + RET_GN_EPS) * gnw_ref[:, cols[p]] + gnb_ref[:, cols[p]]
            g = g_ref[r, cols[p]]
            o_ref[r, cols[p]] = (g * jax.nn.sigmoid(g) * y).astype(BF16)
        return carry

    lax.fori_loop(0, nc // width, pass2, 0)


def _retention(proj, cos_tab, sin_tab, gn_w, gn_b, batch, seq):
    chunk = min(RET_CHUNK, seq)
    nc = seq // chunk
    log_gamma = tuple(float(np.log1p(-np.exp2(-5.0 - h))) for h in range(RET_HEADS))
    kern = functools.partial(_retention_kernel, seq=seq, chunk=chunk, log_gamma=log_gamma)
    col = lambda j: pl.BlockSpec((seq, D_RET), lambda b, j=j: (b, j))
    full = lambda shape: pl.BlockSpec(shape, lambda b: (0,) * len(shape))
    state = pltpu.VMEM((nc, RET_HEADS // 2, LANES, LANES), F32)
    return pl.pallas_call(
        kern,
        grid=(batch,),
        in_specs=[col(0), col(1), col(2), col(3), full((seq, LANES)), full((seq, LANES)),
                  full((1, D_RET)), full((1, D_RET))],
        out_specs=pl.BlockSpec((seq, D_RET), lambda b: (b, 0)),
        out_shape=jax.ShapeDtypeStruct((batch * seq, D_RET), BF16),
        scratch_shapes=[pltpu.VMEM((seq, D_RET), F32), pltpu.VMEM((seq, D_RET), F32), state, state],
        compiler_params=_cparams(("parallel",)),
        name="retention",
    )(proj, proj, proj, proj, cos_tab, sin_tab, gn_w.reshape(1, D_RET), gn_b.reshape(1, D_RET))


def _rwkv_prep_kernel(r_ref, k_ref, v_ref, z_ref, rp_ref, kp_ref, vp_ref, zp_ref,
                      rn_ref, kn_ref, vn_ref, zn_ref, mu_ref, w0_ref, w1_ref, w2_ref,
                      a0_ref, a1_ref, a2_ref, g1_ref, g2_ref, kk_ref, ka_ref, rk_ref,
                      lw0_o, lw1_o, kd0_o, kd1_o, bb0_o, bb1_o, v_o, r_o, kk_o, gate_o, bonus_o,
                      *, tb):
    i = pl.program_id(1)
    has_prev = (i > 0).astype(F32)
    has_next = (i < pl.num_programs(1) - 1).astype(F32)
    row = lax.broadcasted_iota(jnp.int32, (tb, 1), 0)
    avg = _head_avg_matrix()

    def head_sum(x):
        return jnp.concatenate([_head_mean(x[:, c * LANES:(c + 1) * LANES], avg)
                                for c in range(D_RWKV // LANES)], axis=1) * HEAD_DIM

    def shifted(f_ref, p_ref, n_ref):
        f = f_ref[...]
        prev = jnp.where(row == 0, p_ref[7:8, :] * has_prev, pltpu.roll(f, 1, 0))
        nxt = jnp.where(row == tb - 1, n_ref[0:1, :] * has_next, pltpu.roll(f, tb - 1, 0))
        return f, prev - f, nxt - f

    def mix(parts, idx):
        f, dp, dn = parts
        return f + mu_ref[idx, 0:1, :] * dp + mu_ref[idx, 1:2, :] * dn

    xr = mix(shifted(r_ref, rp_ref, rn_ref), 0)
    xk = mix(shifted(k_ref, kp_ref, kn_ref), 1)
    xv = mix(shifted(v_ref, vp_ref, vn_ref), 2)
    zparts = shifted(z_ref, zp_ref, zn_ref)
    xw = mix(zparts, 3)
    xa = mix(zparts, 4)
    xg = mix(zparts, 5)

    gate = _bdot(jax.nn.sigmoid(_bdot(xg, g1_ref[...])), g2_ref[...])
    w_lo = _bdot(jnp.tanh(_bdot(xw, w1_ref[...])), w2_ref[...])
    a_lo = _bdot(_bdot(xa, a1_ref[...]), a2_ref[...])
    kk = xk * kk_ref[...]
    kk = kk / jnp.maximum(jnp.sqrt(head_sum(kk * kk)), 1e-12)

    ksum = jnp.zeros_like(xk)
    outs = ((lw0_o, kd0_o, bb0_o), (lw1_o, kd1_o, bb1_o))
    for d in range(2):
        cols = slice(d * D_RWKV, (d + 1) * D_RWKV)
        w_raw = w0_ref[d:d + 1, :] + w_lo[:, cols]
        log_w = -math.exp(-0.5) * jax.nn.sigmoid(w_raw)
        a = jax.nn.sigmoid(a0_ref[d:d + 1, :] + a_lo[:, cols])
        k_dir = xk * (1.0 + (a - 1.0) * ka_ref[...])
        ksum = ksum + k_dir
        lw_o, kd_o, bb_o = outs[d]
        lw_o[...] = log_w
        kd_o[...] = k_dir.astype(BF16)
        bb_o[...] = (kk * a).astype(BF16)
    bonus = head_sum(xr * ksum * rk_ref[...])
    v_o[...] = xv.astype(BF16)
    r_o[...] = xr.astype(BF16)
    kk_o[...] = kk.astype(BF16)
    gate_o[...] = gate.astype(BF16)
    bonus_o[...] = (bonus * xv).astype(BF16)


def _rwkv_prep(proj, mu, w0, w1, w2, a0, a1, a2, g1, g2, k_k, k_a, r_k, batch, seq):
    tb = min(512, seq)
    nt = seq // tb
    d = D_RWKV
    col0 = (4 * D_RET) // d
    main = lambda j: pl.BlockSpec((tb, d), lambda b, i, j=j: (b * nt + i, col0 + j))
    prev = lambda j: pl.BlockSpec(
        (8, d), lambda b, i, j=j: (b * (seq // 8) + jnp.maximum(i * (tb // 8) - 1, 0), col0 + j))
    nxt = lambda j: pl.BlockSpec(
        (8, d), lambda b, i, j=j: (b * (seq // 8) + jnp.minimum((i + 1) * (tb // 8), seq // 8 - 1),
                                   col0 + j))
    full = lambda shape: pl.BlockSpec(shape, lambda b, i: (0,) * len(shape))
    out_spec = pl.BlockSpec((tb, d), lambda b, i: (b * nt + i, 0))
    out_sd = lambda dtype: jax.ShapeDtypeStruct((batch * seq, d), dtype)
    row = lambda a: a.reshape(1, d)

    def both_dirs(first, second):
        r = first.shape[-1]
        blk = jnp.zeros((2 * r, 2 * d), F32)
        blk = blk.at[:r, :d].set(second[0]).at[r:, d:].set(second[1])
        return jnp.concatenate([first[0], first[1]], axis=1).astype(BF16), blk.astype(BF16)

    w1, w2 = both_dirs(w1, w2)
    a1, a2 = both_dirs(a1, a2)
    g1, g2 = g1.astype(BF16), g2.astype(BF16)
    return pl.pallas_call(
        functools.partial(_rwkv_prep_kernel, tb=tb),
        grid=(batch, nt),
        in_specs=[main(0), main(1), main(2), main(3), prev(0), prev(1), prev(2), prev(3),
                  nxt(0), nxt(1), nxt(2), nxt(3),
                  full(mu.shape), full(w0.shape), full(w1.shape), full(w2.shape),
                  full(a0.shape), full(a1.shape), full(a2.shape), full(g1.shape), full(g2.shape),
                  full((1, d)), full((1, d)), full((1, d))],
        out_specs=[out_spec] * 11,
        out_shape=[out_sd(F32)] * 2 + [out_sd(BF16)] * 9,
        compiler_params=_cparams(("parallel", "parallel")),
        name="rwkv_features",
    )(*([proj] * 12), mu, w0, w1, w2, a0, a1, a2, g1, g2, row(k_k), row(k_a), row(r_k))


def _split3(x):
    hi = x.astype(BF16)
    rest = x - hi.astype(F32)
    mid = rest.astype(BF16)
    lo = (rest - mid.astype(F32)).astype(BF16)
    return hi, mid, lo


def _rwkv_scan_kernel(lw0_ref, kd0_ref, bb0_ref, v0_ref, r0_ref, kk0_ref,
                      lw1_ref, kd1_ref, bb1_ref, v1_ref, r1_ref, kk1_ref,
                      y0_ref, y1_ref, s_ref, gr_ref, u0_ref, mr_ref, vbd_ref, ev_ref, dec_ref,
                      *, chunk, group, nb):
    npair = RWKV_HEADS // 2
    assert 2 * chunk == LANES

    @pl.when(pl.program_id(1) == 0)
    def _():
        s_ref[...] = jnp.zeros_like(s_ref)

    lane = lax.broadcasted_iota(jnp.int32, (1, LANES), 1)
    halves = (lane < chunk, lane >= chunk)
    ri = lax.broadcasted_iota(jnp.int32, (LANES, LANES), 0)
    ci = lax.broadcasted_iota(jnp.int32, (LANES, LANES), 1)
    same_head = (ri // HEAD_DIM) == (ci // HEAD_DIM)
    ti = lax.broadcasted_iota(jnp.int32, (chunk, LANES), 0)
    tj = lax.broadcasted_iota(jnp.int32, (chunk, LANES), 1) % chunk
    eye = (ti == tj).astype(F32)
    strict = (tj < ti, tj > ti)
    incl = (tj <= ti, tj >= ti)
    si = lax.broadcasted_iota(jnp.int32, (chunk, chunk), 0)
    sj = lax.broadcasted_iota(jnp.int32, (chunk, chunk), 1)
    cum_mat = ((sj <= si).astype(BF16), (sj >= si).astype(BF16))
    ins = ((lw0_ref, kd0_ref, bb0_ref, v0_ref, r0_ref, kk0_ref),
           (lw1_ref, kd1_ref, bb1_ref, v1_ref, r1_ref, kk1_ref))
    outs = (y0_ref, y1_ref)
    items = [(c, bb, d, p) for c in range(group) for bb in range(nb) for p in range(npair) for d in range(2)]
    rows = lambda c, d: slice((c if d == 0 else group - 1 - c) * chunk,
                              ((c if d == 0 else group - 1 - c) + 1) * chunk)

    def blockdiag(x):
        return jnp.concatenate([jnp.where(halves[0], x, 0.0), jnp.where(halves[1], x, 0.0)],
                               axis=0).astype(BF16)

    rounds = int(math.log2(chunk)) - 1

    def phase1(sel):
        feats = {}
        for c, bb, d in sorted({items[n][:3] for n in sel}):
            lw_ref, kd_ref, bb_ref, v_ref, r_ref, kk_ref = ins[d]
            rs = rows(c, d)
            lw = lw_ref[bb, rs, :]
            cum_in = sum(_dot(cum_mat[d], part) for part in _split3(lw))
            total = jnp.sum(lw, axis=0, keepdims=True)
            kd = kd_ref[bb, rs, :].astype(F32)
            bv = bb_ref[bb, rs, :].astype(F32)
            inv_p = jnp.exp(-cum_in)
            to_end = jnp.exp(total - cum_in)
            feats[c, bb, d] = dict(a=-kk_ref[bb, rs, :].astype(F32) * jnp.exp(cum_in - lw),
                                   r=r_ref[bb, rs, :].astype(F32) * jnp.exp(cum_in),
                                   b=bv * inv_p, k=kd * inv_p, v=v_ref[bb, rs, :].astype(F32),
                                   bend=bv * to_end, kend=kd * to_end, dec=jnp.exp(total))
        yield

        def per_item(name):
            return [feats[items[n][:3]][name][:, items[n][3] * LANES:(items[n][3] + 1) * LANES] for n in sel]

        dirs = [items[n][2] for n in sel]
        a = per_item("a")
        v = per_item("v")
        v_bd = [blockdiag(x) for x in v]
        ar = [jnp.concatenate(pair, axis=0).astype(BF16) for pair in zip(a, per_item("r"))]
        bk = [jnp.concatenate([blockdiag(x), blockdiag(y)], axis=0)
              for x, y in zip(per_item("b"), per_item("k"))]
        gram = [_dot_nt(x, y) for x, y in zip(ar, bk)]
        l_ab = [jnp.where(strict[d], g[:chunk, :LANES], 0.0) for d, g in zip(dirs, gram)]
        l_ak = [jnp.where(strict[d], g[:chunk, LANES:], 0.0).astype(BF16) for d, g in zip(dirs, gram)]
        for n, d, g in zip(sel, dirs, gram):
            mr_ref[n] = jnp.where(jnp.concatenate([incl[d], incl[d]], axis=1), g[chunk:, :], 0.0).astype(BF16)
        lv = [_dot(x, y) for x, y in zip(l_ak, v_bd)]
        yield

        power = [_dot(x.astype(BF16), blockdiag(x)) for x in l_ab]
        inv = [eye + x for x in l_ab]
        yield
        for j in range(rounds):
            pd = [blockdiag(x) for x in power]
            if j == rounds - 1:
                inv = [t + _dot(t.astype(BF16), x) for t, x in zip(inv, pd)]
            else:
                both = [_dot(jnp.concatenate([x, t], axis=0).astype(BF16), y)
                        for x, t, y in zip(power, inv, pd)]
                power = [x[:chunk] for x in both]
                inv = [t + x[chunk:] for t, x in zip(inv, both)]
            yield
        ta = [_dot(t.astype(BF16), jnp.concatenate([blockdiag(x), blockdiag(y)], axis=1))
              for t, x, y in zip(inv, a, lv)]
        for i, n in enumerate(sel):
            c, bb, d, p = items[n]
            cols = slice(p * LANES, (p + 1) * LANES)
            f = feats[c, bb, d]
            gr_ref[n] = jnp.concatenate([ta[i][:, :LANES].astype(BF16), ar[i][chunk:]], axis=0)
            u0_ref[n] = ta[i][:, LANES:]
            vbd_ref[n] = v_bd[i]
            ev_ref[n] = jnp.concatenate([f["bend"][:, cols], f["kend"][:, cols], v[i]], axis=0).astype(BF16)
            dec_ref[n] = jnp.broadcast_to(f["dec"][:, cols], (8, LANES))
        yield

    n_stages = rounds + 4
    lag = 2
    item_halves = [[n for n, it in enumerate(items) if it[1] % 2 == h] for h in range(min(nb, 2))]
    gens = [phase1(sel) for sel in item_halves]
    for step in range(n_stages + lag * (len(gens) - 1)):
        for g, gen in enumerate(gens):
            if 0 <= step - lag * g < n_stages:
                next(gen)

    per_chunk = 2 * npair * nb
    for c in range(group):
        ns = range(c * per_chunk, (c + 1) * per_chunk)
        states = [s_ref[j] for j in range(per_chunk)]
        prod = [_dot_nt(gr_ref[n], s.astype(BF16)) for n, s in zip(ns, states)]
        u = [x[:chunk] + u0_ref[n] for n, x in zip(ns, prod)]
        y = [x[chunk:] + _dot(mr_ref[n], jnp.concatenate([blockdiag(w), vbd_ref[n]], axis=0))
             for n, x, w in zip(ns, prod, u)]
        upd = [_dot_tn(jnp.concatenate([w.astype(BF16), ev_ref[n, 2 * chunk:, :]], axis=0),
                       ev_ref[n, :2 * chunk, :]) for n, w in zip(ns, u)]
        for j, n in enumerate(ns):
            _, bb, d, p = items[n]
            outs[d][bb, rows(c, d), p * LANES:(p + 1) * LANES] = y[j]
            s_ref[j] = states[j] * dec_ref[n, 0:1, :] + jnp.where(same_head, upd[j], 0.0)


def _rwkv_scan(feats, batch, seq):
    chunk = min(RWKV_CHUNK, seq)
    group = min(RWKV_GROUP, seq // chunk)
    tb = chunk * group
    nt = seq // tb
    nb = 2 if batch % 2 == 0 else 1
    n_items = group * RWKV_HEADS * nb
    fwd = pl.BlockSpec((nb, tb, D_RWKV), lambda b, i: (b, i, 0))
    bwd = pl.BlockSpec((nb, tb, D_RWKV), lambda b, i: (b, nt - 1 - i, 0))
    out_sd = jax.ShapeDtypeStruct((batch, seq, D_RWKV), F32)
    lw0, lw1, kd0, kd1, bb0, bb1, xv, xr, kk = [x.reshape(batch, seq, D_RWKV) for x in feats]
    y0, y1 = pl.pallas_call(
        functools.partial(_rwkv_scan_kernel, chunk=chunk, group=group, nb=nb),
        grid=(batch // nb, nt),
        in_specs=[fwd] * 6 + [bwd] * 6,
        out_specs=[fwd, bwd],
        out_shape=[out_sd, out_sd],
        scratch_shapes=[pltpu.VMEM((RWKV_HEADS * nb, LANES, LANES), F32),
                        pltpu.VMEM((n_items, 2 * chunk, LANES), BF16),
                        pltpu.VMEM((n_items, chunk, LANES), F32),
                        pltpu.VMEM((n_items, chunk, 4 * chunk), BF16),
                        pltpu.VMEM((n_items, 2 * chunk, LANES), BF16),
                        pltpu.VMEM((n_items, 3 * chunk, LANES), BF16),
                        pltpu.VMEM((n_items, 8, LANES), F32)],
        compiler_params=_cparams(("parallel", "arbitrary")),
        name="rwkv_scan",
    )(lw0, kd0, bb0, xv, xr, kk, lw1, kd1, bb1, xv, xr, kk)
    return y0.reshape(batch * seq, D_RWKV), y1.reshape(batch * seq, D_RWKV)


def _diff_attn_kernel(q_ref, k_ref, v_ref, band_ref, lam_ref, subw_ref, o_ref, kb_ref, vt_ref,
                      acc_ref, s_ref, *, seq, tq, tk, nblk, lambda_init):
    i = pl.program_id(2)

    @pl.when(i == 0)
    def _():
        kb_ref[...] = k_ref[...].astype(BF16)
        vt_ref[...] = jnp.transpose(v_ref[...]).astype(BF16)

    lane = lax.broadcasted_iota(jnp.int32, (1, LANES), 1)
    lv = lam_ref[...]
    lam = (jnp.exp(jnp.sum(lv[0:1] * lv[1:2], axis=1, keepdims=True))
           - jnp.exp(jnp.sum(lv[2:3] * lv[3:4], axis=1, keepdims=True)) + lambda_init)
    def one_block(blk):
        qi = i * nblk + blk
        q = q_ref[blk * tq:(blk + 1) * tq, :] * (DIFF_QK_DIM ** -0.5 * LOG2E)
        ones_rows = jnp.ones((BF16_ROWS, tk), BF16)
        n_maps = 4
        qms = [jnp.where((lane >= m * DIFF_QK_DIM) & (lane < (m + 1) * DIFF_QK_DIM), q, 0.0).astype(BF16)
               for m in range(n_maps)]
        n_tiles = seq // tk
        acc_ref[blk] = jnp.zeros(acc_ref.shape[1:], F32)

        def tile_index(j):
            t = qi + j
            return jnp.where(t >= n_tiles, t - n_tiles, t)

        def stage_scores(j, m):
            start = pl.multiple_of(tile_index(j) * tk, tk)
            s_ref[blk, j % 2, m] = _dot_nt(kb_ref[pl.ds(start, tk), :], qms[m])

        m_run = [jnp.full((1, tq), -1e30, F32)] * n_maps
        for m in range(n_maps):
            stage_scores(0, m)
        for j in range(n_tiles):
            slot = j % 2
            t = tile_index(j)
            start = pl.multiple_of(t * tk, tk)
            far = 2 <= j <= n_tiles - 2
            rows = BF16_ROWS // 2 if far else tk
            band_start = pl.multiple_of((jnp.clip(t - qi, -2, 2) + 2) * tk, tk)
            bias = [band_ref[hh, pl.ds(band_start, rows), :] for hh in range(2)]
            vt = [jnp.concatenate([vt_ref[hh * HEAD_DIM:(hh + 1) * HEAD_DIM, pl.ds(start, tk)],
                                   ones_rows], axis=0) for hh in range(2)]
            for m in range(n_maps):
                if j + 1 < n_tiles:
                    stage_scores(j + 1, m)
                if far:
                    s = s_ref[blk, slot, m]
                    const = bias[m // 2][0:1, :]
                    m_new = jnp.maximum(m_run[m], jnp.max(s, axis=0, keepdims=True) + const)
                    e = jnp.exp2(s - (m_new - const)).astype(BF16)
                else:
                    s = s_ref[blk, slot, m] + bias[m // 2]
                    m_new = jnp.maximum(m_run[m], jnp.max(s, axis=0, keepdims=True))
                    e = jnp.exp2(s - m_new).astype(BF16)
                pv = _dot(vt[m // 2], e)
                acc_ref[blk, m] = acc_ref[blk, m] * jnp.exp2(m_run[m] - m_new) + pv
                m_run[m] = m_new
        parts = [acc_ref[blk, m, :HEAD_DIM, :] / acc_ref[blk, m, HEAD_DIM:HEAD_DIM + 1, :] for m in range(n_maps)]
        heads = []
        for hh in range(2):
            head = parts[2 * hh] - lam * parts[2 * hh + 1]
            ms = jnp.mean(head * head, axis=0, keepdims=True)
            heads.append(head * lax.rsqrt(ms + NORM_EPS))
        out = jnp.transpose(jnp.concatenate(heads, axis=0))
        o_ref[blk * tq:(blk + 1) * tq, :] = (out * subw_ref[...] * (1.0 - lambda_init)).astype(BF16)

    for blk in range(nblk):
        one_block(blk)


def _diff_attention(proj, band, lam_vecs, subln_w, lambda_init, batch, seq):
    tq = min(ATTN_TQ, seq)
    nq = seq // tq
    nblk = ATTN_BLOCKS if nq % ATTN_BLOCKS == 0 else 1
    nsteps = nq // nblk
    npair = DIFF_HEADS // 2
    col0 = (4 * D_RET + 4 * D_RWKV) // LANES
    width = band.shape[1]
    tk = tq
    assert 2 * _bucket_saturation_distance() < tk
    return pl.pallas_call(
        functools.partial(_diff_attn_kernel, seq=seq, tq=tq, tk=tk, nblk=nblk, lambda_init=lambda_init),
        grid=(batch, npair, nsteps),
        in_specs=[pl.BlockSpec((nblk * tq, LANES), lambda b, p, i: (b * nsteps + i, col0 + p)),
                  pl.BlockSpec((seq, LANES), lambda b, p, i: (b, col0 + npair + p)),
                  pl.BlockSpec((seq, LANES), lambda b, p, i: (b, col0 + 2 * npair + p)),
                  pl.BlockSpec((2, width, tq), lambda b, p, i: (p, 0, 0)),
                  pl.BlockSpec((4, DIFF_QK_DIM), lambda b, p, i: (0, 0)),
                  pl.BlockSpec((1, LANES), lambda b, p, i: (0, 0))],
        out_specs=pl.BlockSpec((nblk * tq, LANES), lambda b, p, i: (b * nsteps + i, p)),
        out_shape=jax.ShapeDtypeStruct((batch * seq, D_DIFF), BF16),
        scratch_shapes=[pltpu.VMEM((seq, LANES), BF16), pltpu.VMEM((LANES, seq), BF16),
                        pltpu.VMEM((nblk, 4, HEAD_DIM + BF16_ROWS, tq), F32),
                        pltpu.VMEM((nblk, 2, 4, tk, tq), F32)],
        compiler_params=_cparams(("parallel", "parallel", "arbitrary")),
        name="diff_attention",
    )(proj, proj, proj, band, lam_vecs, jnp.tile(subln_w, 2).reshape(1, LANES))


def _t5_bucket(rel):
    nb = REL_BUCKETS // 2
    max_exact = nb // 2
    n = jnp.abs(rel)
    nf = jnp.maximum(n, 1).astype(jnp.float32)
    large = max_exact + (jnp.log(nf / max_exact) / math.log(REL_MAX_DIST / max_exact)
                         * (nb - max_exact)).astype(jnp.int32)
    large = jnp.minimum(large, nb - 1)
    return jnp.where(rel > 0, nb, 0) + jnp.where(n < max_exact, n, large)


def _bucket_saturation_distance():
    nb = REL_BUCKETS // 2
    max_exact = nb // 2
    n = np.arange(1, 4 * REL_MAX_DIST)
    large = max_exact + (np.log(n / max_exact) / math.log(REL_MAX_DIST / max_exact)
                         * (nb - max_exact)).astype(np.int64)
    bucket = np.where(n < max_exact, n, np.minimum(large, nb - 1))
    return int(n[bucket < nb - 1].max())


def _bias_band(rel_bias, tq):
    width = 5 * tq
    period = 6 * tq - 1
    m = jnp.arange(period, dtype=jnp.int32)
    rel = jnp.where(m < width, m, m - period) - 2 * tq
    vec = rel_bias.astype(F32)[_t5_bucket(rel)].T * LOG2E
    rows = jnp.tile(vec, (1, tq))[:, :tq * (period - 1)].reshape(-1, tq, period - 1)
    return jnp.transpose(rows[:, :, :width], (0, 2, 1))


def _outproj_kernel(x_ref, yr_ref, y0_ref, y1_ref, bonus_ref, gate_ref, yd_ref, lnw_ref, lnb_ref,
                    w_ref, o_ref):
    avg = _head_avg_matrix()
    mixed = [yr_ref[...]]
    for p in range(RWKV_HEADS // 2):
        cols = slice(p * LANES, (p + 1) * LANES)
        y = _group_norm(y0_ref[:, cols] + y1_ref[:, cols], avg, lnw_ref[:, cols], lnb_ref[:, cols],
                        RWKV_GN_EPS)
        mixed.append(((y + bonus_ref[:, cols]) * gate_ref[:, cols]).astype(BF16))
    mixed.append(yd_ref[...])
    o_ref[...] = x_ref[...] + _dot(jnp.concatenate(mixed, axis=1), w_ref[...])


def _outproj(x2, y_ret, y0, y1, bonus, gate, y_diff, ln_w, ln_b, w_bf16, tm=512):
    m, d = x2.shape
    tm = min(tm, m)
    row = lambda n: pl.BlockSpec((tm, n), lambda i: (i, 0))
    vec = pl.BlockSpec((1, D_RWKV), lambda i: (0, 0))
    return pl.pallas_call(
        _outproj_kernel,
        grid=(m // tm,),
        in_specs=[row(d), row(D_RET), row(D_RWKV), row(D_RWKV), row(D_RWKV), row(D_RWKV), row(D_DIFF),
                  vec, vec, pl.BlockSpec(w_bf16.shape, lambda i: (0, 0))],
        out_specs=row(d),
        out_shape=jax.ShapeDtypeStruct((m, d), F32),
        compiler_params=_cparams(("parallel",)),
        name="outproj",
    )(x2, y_ret, y0, y1, bonus, gate, y_diff, ln_w.reshape(1, D_RWKV), ln_b.reshape(1, D_RWKV), w_bf16)


def _ffn_up_kernel(x_ref, g_ref, wg_ref, wu_ref, o_ref, *, tf):
    x = x_ref[...]
    ms = jnp.mean(x * x, axis=-1, keepdims=True)
    h = (x * lax.rsqrt(ms + NORM_EPS) * g_ref[...]).astype(BF16)
    for j in range(wg_ref.shape[1] // tf):
        cols = slice(j * tf, (j + 1) * tf)
        gate = _dot(h, wg_ref[:, cols])
        up = _dot(h, wu_ref[:, cols])
        o_ref[:, cols] = (gate * jax.nn.sigmoid(gate) * up).astype(BF16)


def _ffn_up(x2, g, wg_bf16, wu_bf16, tm=512, tf=256):
    m, d = x2.shape
    tm = min(tm, m)
    f = wg_bf16.shape[1]
    return pl.pallas_call(
        functools.partial(_ffn_up_kernel, tf=tf),
        grid=(m // tm,),
        in_specs=[pl.BlockSpec((tm, d), lambda i: (i, 0)),
                  pl.BlockSpec((1, d), lambda i: (0, 0)),
                  pl.BlockSpec((d, f), lambda i: (0, 0)),
                  pl.BlockSpec((d, f), lambda i: (0, 0))],
        out_specs=pl.BlockSpec((tm, f), lambda i: (i, 0)),
        out_shape=jax.ShapeDtypeStruct((m, f), BF16),
        compiler_params=_cparams(("parallel",)),
        name="ffn_up",
    )(x2, g.reshape(1, d), wg_bf16, wu_bf16)


def _ffn_down_kernel(x_ref, h_ref, w_ref, g_ref, o_ref, *, final_norm):
    y = x_ref[...] + _dot(h_ref[...], w_ref[...])
    if final_norm:
        ms = jnp.mean(y * y, axis=-1, keepdims=True)
        y = y * lax.rsqrt(ms + NORM_EPS) * g_ref[...]
    o_ref[...] = y


def _ffn_down(x2, hidden, w_bf16, g, final_norm, tm=512):
    m, d = x2.shape
    tm = min(tm, m)
    f = hidden.shape[1]
    return pl.pallas_call(
        functools.partial(_ffn_down_kernel, final_norm=final_norm),
        grid=(m // tm,),
        in_specs=[pl.BlockSpec((tm, d), lambda i: (i, 0)),
                  pl.BlockSpec((tm, f), lambda i: (i, 0)),
                  pl.BlockSpec((f, d), lambda i: (0, 0)),
                  pl.BlockSpec((1, d), lambda i: (0, 0))],
        out_specs=pl.BlockSpec((tm, d), lambda i: (i, 0)),
        out_shape=jax.ShapeDtypeStruct((m, d), F32),
        compiler_params=_cparams(("parallel",)),
        name="ffn_down",
    )(x2, hidden, w_bf16, g.reshape(1, d))


def _rope_tables(seq):
    half = HEAD_DIM // 2
    freqs = ROPE_BASE ** (-jnp.arange(half, dtype=F32) / half)
    ang = jnp.arange(seq, dtype=jnp.int32).astype(F32)[:, None] * freqs[None, :]
    cos = jnp.cos(ang)
    sin = jnp.sin(ang)
    cos_tab = jnp.tile(cos, (1, LANES // half))
    sin_tab = jnp.concatenate([-sin, -sin, sin, sin], axis=1)
    return cos_tab, sin_tab


def _retention_qk_layout(w):
    d = w.shape[0]
    half = HEAD_DIM // 2
    qk = w[:, :2 * D_RET].reshape(d, 2, RET_HEADS // 2, 2, 2, half)
    qk = jnp.swapaxes(qk, 3, 4).reshape(d, 2 * D_RET)
    return jnp.concatenate([qk, w[:, 2 * D_RET:]], axis=1)


def kernel(x, mix_norm_g, w_in, w_out, ret_gn_w, ret_gn_b, rwkv_mu, rwkv_w0, rwkv_w1, rwkv_w2, rwkv_a0, rwkv_a1, rwkv_a2, rwkv_g1, rwkv_g2, rwkv_k_k, rwkv_k_a, rwkv_r_k, rwkv_ln_w, rwkv_ln_b, diff_lambda, diff_subln_w, rel_bias, ffn_norm_g, w_gate, w_up, w_down, final_norm_g):
    batch, seq, d = x.shape
    depth = w_in.shape[0]
    cos_tab, sin_tab = _rope_tables(seq)
    band = _bias_band(rel_bias, min(ATTN_TQ, seq))
    x2 = x.reshape(batch * seq, d)
    for l in range(depth):
        proj = _norm_matmul(x2, mix_norm_g[l], _retention_qk_layout(w_in[l]).astype(BF16))
        y_ret = _retention(proj, cos_tab, sin_tab, ret_gn_w[l], ret_gn_b[l], batch, seq)
        feats = _rwkv_prep(proj, rwkv_mu[l], rwkv_w0[l], rwkv_w1[l], rwkv_w2[l], rwkv_a0[l],
                           rwkv_a1[l], rwkv_a2[l], rwkv_g1[l], rwkv_g2[l], rwkv_k_k[l],
                           rwkv_k_a[l], rwkv_r_k[l].reshape(-1), batch, seq)
        y0, y1 = _rwkv_scan(feats[:9], batch, seq)
        lambda_init = 0.8 - 0.6 * math.exp(-0.3 * l)
        y_diff = _diff_attention(proj, band, diff_lambda[l], diff_subln_w[l], lambda_init,
                                 batch, seq)
        x2 = _outproj(x2, y_ret, y0, y1, feats[10], feats[9], y_diff, rwkv_ln_w[l], rwkv_ln_b[l],
                      w_out[l].astype(BF16))
        hidden = _ffn_up(x2, ffn_norm_g[l], w_gate[l].astype(BF16), w_up[l].astype(BF16))
        x2 = _ffn_down(x2, hidden, w_down[l].astype(BF16), final_norm_g, l == depth - 1)
    return x2.reshape(batch, seq, d)
```

```python
import functools
import math

import numpy as np
import jax
import jax.numpy as jnp
from jax import lax
from jax.experimental import pallas as pl
from jax.experimental.pallas import tpu as pltpu

F32 = jnp.float32
BF16 = jnp.bfloat16

D_MODEL = 1024
HEAD_DIM = 64
LANES = 128
BF16_ROWS = 16
LOG2E = math.log2(math.e)
N_HEADS = D_MODEL // HEAD_DIM
RET_HEADS = (3 * N_HEADS) // 8
RWKV_HEADS = (3 * N_HEADS) // 8
DIFF_HEADS = N_HEADS - RET_HEADS - RWKV_HEADS
D_RET = RET_HEADS * HEAD_DIM
D_RWKV = RWKV_HEADS * HEAD_DIM
D_DIFF = DIFF_HEADS * HEAD_DIM
DIFF_QK_DIM = HEAD_DIM // 2
D_IN = 4 * D_RET + 4 * D_RWKV + 3 * D_DIFF
ROPE_BASE = 10000.0
D_FF = -(-8 * D_MODEL // (3 * 256)) * 256
REL_BUCKETS = 32
REL_MAX_DIST = 128
NORM_EPS = 1e-6
RET_GN_EPS = 1e-5
RWKV_GN_EPS = 64e-5

RET_CHUNK = 128
RWKV_CHUNK = 64
RWKV_GROUP = 4
ATTN_TQ = 256
ATTN_BLOCKS = 8
VMEM_LIMIT = 56 * 1024 * 1024


def _cparams(sem):
    return pltpu.CompilerParams(dimension_semantics=sem, vmem_limit_bytes=VMEM_LIMIT)


def _dot(a, b):
    return jnp.dot(a, b, preferred_element_type=F32)


def _bdot(a, b):
    return _dot(a.astype(BF16), b.astype(BF16))


def _dot_nt(a, b):
    return lax.dot_general(a, b, (((1,), (1,)), ((), ())), preferred_element_type=F32)


def _dot_tn(a, b):
    return lax.dot_general(a, b, (((0,), (0,)), ((), ())), preferred_element_type=F32)


def _head_avg_matrix():
    r = lax.broadcasted_iota(jnp.int32, (LANES, LANES), 0) // HEAD_DIM
    c = lax.broadcasted_iota(jnp.int32, (LANES, LANES), 1) // HEAD_DIM
    return jnp.where(r == c, 1.0 / HEAD_DIM, 0.0).astype(BF16)


def _head_mean(x, avg):
    hi = x.astype(BF16)
    lo = (x - hi.astype(F32)).astype(BF16)
    return _dot(hi, avg) + _dot(lo, avg)


def _group_norm(y, avg, w, b, eps):
    yc = y - _head_mean(y, avg)
    var = _head_mean(yc * yc, avg)
    return yc * lax.rsqrt(var + eps) * w + b


def _norm_matmul_kernel(x_ref, g_ref, w_ref, o_ref):
    x = x_ref[...]
    ms = jnp.mean(x * x, axis=-1, keepdims=True)
    h = x * lax.rsqrt(ms + NORM_EPS) * g_ref[...]
    o_ref[...] = _dot(h.astype(BF16), w_ref[...])


def _norm_matmul(x2, g, w_bf16, tm=512):
    m, d = x2.shape
    n = w_bf16.shape[1]
    return pl.pallas_call(
        _norm_matmul_kernel,
        grid=(m // tm,),
        in_specs=[pl.BlockSpec((tm, d), lambda i: (i, 0)),
                  pl.BlockSpec((1, d), lambda i: (0, 0)),
                  pl.BlockSpec((d, n), lambda i: (0, 0))],
        out_specs=pl.BlockSpec((tm, n), lambda i: (i, 0)),
        out_shape=jax.ShapeDtypeStruct((m, n), F32),
        compiler_params=_cparams(("parallel",)),
        name="norm_inproj",
    )(x2, g.reshape(1, d), w_bf16)


def _retention_kernel(q_ref, k_ref, v_ref, g_ref, cos_ref, sin_ref, gnw_ref, gnb_ref, o_ref,
                      qs_ref, ks_ref, sf_ref, sb_ref, *, seq, chunk, log_gamma):
    nc = seq // chunk
    npair = RET_HEADS // 2
    lane = lax.broadcasted_iota(jnp.int32, (1, LANES), 1)
    half = HEAD_DIM // 2
    head_masks = (lane < HEAD_DIM, lane >= HEAD_DIM)
    qk_masks = ((lane // half) % 2 == 0, (lane // half) % 2 == 1)
    ri = lax.broadcasted_iota(jnp.int32, (LANES, LANES), 0)
    ci = lax.broadcasted_iota(jnp.int32, (LANES, LANES), 1)
    block_diag = ((ri // half) % 2) == (ci // HEAD_DIM)
    avg = _head_avg_matrix()
    ti = lax.broadcasted_iota(jnp.int32, (chunk, 2 * chunk), 0)
    tj = lax.broadcasted_iota(jnp.int32, (chunk, 2 * chunk), 1)
    dist = jnp.abs(ti - jnp.where(tj >= chunk, tj - chunk, tj)).astype(F32)
    pos = lax.broadcasted_iota(jnp.int32, (chunk, 1), 0).astype(F32)
    pairs = range(npair)
    cols = [slice(p * LANES, (p + 1) * LANES) for p in pairs]
    lg = [jnp.where(qk_masks[0], log_gamma[2 * p], log_gamma[2 * p + 1]).astype(F32) for p in pairs]
    lg_v = [jnp.where(head_masks[0], log_gamma[2 * p], log_gamma[2 * p + 1]).astype(F32) for p in pairs]
    decay = [jnp.exp(dist * jnp.where(tj >= chunk, log_gamma[2 * p + 1], log_gamma[2 * p])) for p in pairs]
    xi_f = [jnp.exp((pos + 1.0) * x) for x in lg]
    xi_b = [jnp.exp((chunk - pos) * x) for x in lg]
    zeta_f = [jnp.exp((chunk - 1.0 - pos) * x) for x in lg]
    zeta_b = [jnp.exp(pos * x) for x in lg]
    dec_c = [jnp.exp(chunk * x) for x in lg_v]

    def rope(x, cos, sin):
        return x * cos + pltpu.roll(x, HEAD_DIM, 1) * sin

    def blockdiag(x, masks):
        return jnp.concatenate([jnp.where(masks[0], x, 0.0), jnp.where(masks[1], x, 0.0)],
                               axis=0).astype(BF16)

    width = 2 if nc % 2 == 0 else 1

    def pass1(i, carry):
        work = [(i * width + w, p) for w in range(width) for p in pairs]
        rows = [pl.ds(pl.multiple_of(n * chunk, chunk), chunk) for n, _ in work]
        cos = [cos_ref[r, :] for r in rows]
        sin = [sin_ref[r, :] for r in rows]
        q = [rope(q_ref[r, cols[p]], c, s) * (HEAD_DIM ** -0.5) for r, c, s, (_, p) in zip(rows, cos, sin, work)]
        k = [rope(k_ref[r, cols[p]], c, s) for r, c, s, (_, p) in zip(rows, cos, sin, work)]
        kz = [jnp.concatenate([x * zeta_f[p], x * zeta_b[p]], axis=1).astype(BF16) for x, (_, p) in zip(k, work)]
        kv = [_dot_tn(x, v_ref[r, cols[p]].astype(BF16)) for x, r, (_, p) in zip(kz, rows, work)]
        for x, y, z, r, (n, p) in zip(q, k, kv, rows, work):
            qs_ref[r, cols[p]] = x
            ks_ref[r, cols[p]] = y
            sf_ref[n, p] = jnp.where(block_diag, z[:LANES], 0.0)
            sb_ref[n, p] = jnp.where(block_diag, z[LANES:], 0.0)
        return carry

    lax.fori_loop(0, nc // width, pass1, 0)

    def scan_states(i, states):
        left, right = states
        new_left, new_right = [], []
        for p in pairs:
            kv = sf_ref[i, p]
            sf_ref[i, p] = left[p]
            new_left.append(left[p] * dec_c[p] + kv)
            kv = sb_ref[nc - 1 - i, p]
            sb_ref[nc - 1 - i, p] = right[p]
            new_right.append(right[p] * dec_c[p] + kv)
        return tuple(new_left), tuple(new_right)

    zeros = (jnp.zeros((LANES, LANES), F32),) * npair
    lax.fori_loop(0, nc, scan_states, (zeros, zeros))

    def pass2(i, carry):
        work = [(i * width + w, p) for w in range(width) for p in pairs]
        rows = [pl.ds(pl.multiple_of(n * chunk, chunk), chunk) for n, _ in work]
        q = [qs_ref[r, cols[p]] for r, (_, p) in zip(rows, work)]
        k_bd = [blockdiag(ks_ref[r, cols[p]], qk_masks) for r, (_, p) in zip(rows, work)]
        v_bd = [blockdiag(v_ref[r, cols[p]], head_masks) for r, (_, p) in zip(rows, work)]
        s = [(_dot_nt(x.astype(BF16), y) * decay[p]).astype(BF16) for x, y, (_, p) in zip(q, k_bd, work)]
        qx = [jnp.concatenate([x * xi_f[p], x * xi_b[p]], axis=1).astype(BF16) for x, (_, p) in zip(q, work)]
        st = [jnp.concatenate([sf_ref[n, p], sb_ref[n, p]], axis=0).astype(BF16) for n, p in work]
        o = [_dot(x, y) + _dot(z, w) for x, y, z, w in zip(s, v_bd, qx, st)]
        mean = [_head_mean(x, avg) for x in o]
        centred = [x - m for x, m in zip(o, mean)]
        var = [_head_mean(x * x, avg) for x in centred]
        for x, vr, r, (_, p) in zip(centred, var, rows, work):
            y = x * lax.rsqrt(vr + RET_GN_EPS) * gnw_ref[:, cols[p]] + gnb_ref[:, cols[p]]
            g = g_ref[r, cols[p]]
            o_ref[r, cols[p]] = (g * jax.nn.sigmoid(g) * y).astype(BF16)
        return carry

    lax.fori_loop(0, nc // width, pass2, 0)


def _retention(proj, cos_tab, sin_tab, gn_w, gn_b, batch, seq):
    chunk = min(RET_CHUNK, seq)
    nc = seq // chunk
    log_gamma = tuple(float(np.log1p(-np.exp2(-5.0 - h))) for h in range(RET_HEADS))
    kern = functools.partial(_retention_kernel, seq=seq, chunk=chunk, log_gamma=log_gamma)
    col = lambda j: pl.BlockSpec((seq, D_RET), lambda b, j=j: (b, j))
    full = lambda shape: pl.BlockSpec(shape, lambda b: (0,) * len(shape))
    state = pltpu.VMEM((nc, RET_HEADS // 2, LANES, LANES), F32)
    return pl.pallas_call(
        kern,
        grid=(batch,),
        in_specs=[col(0), col(1), col(2), col(3), full((seq, LANES)), full((seq, LANES)),
                  full((1, D_RET)), full((1, D_RET))],
        out_specs=pl.BlockSpec((seq, D_RET), lambda b: (b, 0)),
        out_shape=jax.ShapeDtypeStruct((batch * seq, D_RET), BF16),
        scratch_shapes=[pltpu.VMEM((seq, D_RET), F32), pltpu.VMEM((seq, D_RET), F32), state, state],
        compiler_params=_cparams(("parallel",)),
        name="retention",
    )(proj, proj, proj, proj, cos_tab, sin_tab, gn_w.reshape(1, D_RET), gn_b.reshape(1, D_RET))


def _rwkv_prep_kernel(r_ref, k_ref, v_ref, z_ref, rp_ref, kp_ref, vp_ref, zp_ref,
                      rn_ref, kn_ref, vn_ref, zn_ref, mu_ref, w0_ref, w1_ref, w2_ref,
                      a0_ref, a1_ref, a2_ref, g1_ref, g2_ref, kk_ref, ka_ref, rk_ref,
                      lw0_o, lw1_o, kd0_o, kd1_o, bb0_o, bb1_o, v_o, r_o, kk_o, gate_o, bonus_o,
                      *, tb):
    i = pl.program_id(1)
    has_prev = (i > 0).astype(F32)
    has_next = (i < pl.num_programs(1) - 1).astype(F32)
    row = lax.broadcasted_iota(jnp.int32, (tb, 1), 0)
    avg = _head_avg_matrix()

    def head_sum(x):
        return jnp.concatenate([_head_mean(x[:, c * LANES:(c + 1) * LANES], avg)
                                for c in range(D_RWKV // LANES)], axis=1) * HEAD_DIM

    def shifted(f_ref, p_ref, n_ref):
        f = f_ref[...]
        prev = jnp.where(row == 0, p_ref[7:8, :] * has_prev, pltpu.roll(f, 1, 0))
        nxt = jnp.where(row == tb - 1, n_ref[0:1, :] * has_next, pltpu.roll(f, tb - 1, 0))
        return f, prev - f, nxt - f

    def mix(parts, idx):
        f, dp, dn = parts
        return f + mu_ref[idx, 0:1, :] * dp + mu_ref[idx, 1:2, :] * dn

    xr = mix(shifted(r_ref, rp_ref, rn_ref), 0)
    xk = mix(shifted(k_ref, kp_ref, kn_ref), 1)
    xv = mix(shifted(v_ref, vp_ref, vn_ref), 2)
    zparts = shifted(z_ref, zp_ref, zn_ref)
    xw = mix(zparts, 3)
    xa = mix(zparts, 4)
    xg = mix(zparts, 5)

    gate = _bdot(jax.nn.sigmoid(_bdot(xg, g1_ref[...])), g2_ref[...])
    w_lo = _bdot(jnp.tanh(_bdot(xw, w1_ref[...])), w2_ref[...])
    a_lo = _bdot(_bdot(xa, a1_ref[...]), a2_ref[...])
    kk = xk * kk_ref[...]
    kk = kk / jnp.maximum(jnp.sqrt(head_sum(kk * kk)), 1e-12)

    ksum = jnp.zeros_like(xk)
    outs = ((lw0_o, kd0_o, bb0_o), (lw1_o, kd1_o, bb1_o))
    for d in range(2):
        cols = slice(d * D_RWKV, (d + 1) * D_RWKV)
        w_raw = w0_ref[d:d + 1, :] + w_lo[:, cols]
        log_w = -math.exp(-0.5) * jax.nn.sigmoid(w_raw)
        a = jax.nn.sigmoid(a0_ref[d:d + 1, :] + a_lo[:, cols])
        k_dir = xk * (1.0 + (a - 1.0) * ka_ref[...])
        ksum = ksum + k_dir
        lw_o, kd_o, bb_o = outs[d]
        lw_o[...] = log_w
        kd_o[...] = k_dir.astype(BF16)
        bb_o[...] = (kk * a).astype(BF16)
    bonus = head_sum(xr * ksum * rk_ref[...])
    v_o[...] = xv.astype(BF16)
    r_o[...] = xr.astype(BF16)
    kk_o[...] = kk.astype(BF16)
    gate_o[...] = gate.astype(BF16)
    bonus_o[...] = (bonus * xv).astype(BF16)


def _rwkv_prep(proj, mu, w0, w1, w2, a0, a1, a2, g1, g2, k_k, k_a, r_k, batch, seq):
    tb = min(512, seq)
    nt = seq // tb
    d = D_RWKV
    col0 = (4 * D_RET) // d
    main = lambda j: pl.BlockSpec((tb, d), lambda b, i, j=j: (b * nt + i, col0 + j))
    prev = lambda j: pl.BlockSpec(
        (8, d), lambda b, i, j=j: (b * (seq // 8) + jnp.maximum(i * (tb // 8) - 1, 0), col0 + j))
    nxt = lambda j: pl.BlockSpec(
        (8, d), lambda b, i, j=j: (b * (seq // 8) + jnp.minimum((i + 1) * (tb // 8), seq // 8 - 1),
                                   col0 + j))
    full = lambda shape: pl.BlockSpec(shape, lambda b, i: (0,) * len(shape))
    out_spec = pl.BlockSpec((tb, d), lambda b, i: (b * nt + i, 0))
    out_sd = lambda dtype: jax.ShapeDtypeStruct((batch * seq, d), dtype)
    row = lambda a: a.reshape(1, d)

    def both_dirs(first, second):
        r = first.shape[-1]
        blk = jnp.zeros((2 * r, 2 * d), F32)
        blk = blk.at[:r, :d].set(second[0]).at[r:, d:].set(second[1])
        return jnp.concatenate([first[0], first[1]], axis=1).astype(BF16), blk.astype(BF16)

    w1, w2 = both_dirs(w1, w2)
    a1, a2 = both_dirs(a1, a2)
    g1, g2 = g1.astype(BF16), g2.astype(BF16)
    return pl.pallas_call(
        functools.partial(_rwkv_prep_kernel, tb=tb),
        grid=(batch, nt),
        in_specs=[main(0), main(1), main(2), main(3), prev(0), prev(1), prev(2), prev(3),
                  nxt(0), nxt(1), nxt(2), nxt(3),
                  full(mu.shape), full(w0.shape), full(w1.shape), full(w2.shape),
                  full(a0.shape), full(a1.shape), full(a2.shape), full(g1.shape), full(g2.shape),
                  full((1, d)), full((1, d)), full((1, d))],
        out_specs=[out_spec] * 11,
        out_shape=[out_sd(F32)] * 2 + [out_sd(BF16)] * 9,
        compiler_params=_cparams(("parallel", "parallel")),
        name="rwkv_features",
    )(*([proj] * 12), mu, w0, w1, w2, a0, a1, a2, g1, g2, row(k_k), row(k_a), row(r_k))


def _split3(x):
    hi = x.astype(BF16)
    rest = x - hi.astype(F32)
    mid = rest.astype(BF16)
    lo = (rest - mid.astype(F32)).astype(BF16)
    return hi, mid, lo


def _rwkv_scan_kernel(lw0_ref, kd0_ref, bb0_ref, v0_ref, r0_ref, kk0_ref,
                      lw1_ref, kd1_ref, bb1_ref, v1_ref, r1_ref, kk1_ref,
                      y0_ref, y1_ref, s_ref, gr_ref, u0_ref, mr_ref, vbd_ref, ev_ref, dec_ref,
                      *, chunk, group, nb):
    npair = RWKV_HEADS // 2
    assert 2 * chunk == LANES

    @pl.when(pl.program_id(1) == 0)
    def _():
        s_ref[...] = jnp.zeros_like(s_ref)

    lane = lax.broadcasted_iota(jnp.int32, (1, LANES), 1)
    halves = (lane < chunk, lane >= chunk)
    ri = lax.broadcasted_iota(jnp.int32, (LANES, LANES), 0)
    ci = lax.broadcasted_iota(jnp.int32, (LANES, LANES), 1)
    same_head = (ri // HEAD_DIM) == (ci // HEAD_DIM)
    ti = lax.broadcasted_iota(jnp.int32, (chunk, LANES), 0)
    tj = lax.broadcasted_iota(jnp.int32, (chunk, LANES), 1) % chunk
    eye = (ti == tj).astype(F32)
    strict = (tj < ti, tj > ti)
    incl = (tj <= ti, tj >= ti)
    si = lax.broadcasted_iota(jnp.int32, (chunk, chunk), 0)
    sj = lax.broadcasted_iota(jnp.int32, (chunk, chunk), 1)
    cum_mat = ((sj <= si).astype(BF16), (sj >= si).astype(BF16))
    ins = ((lw0_ref, kd0_ref, bb0_ref, v0_ref, r0_ref, kk0_ref),
           (lw1_ref, kd1_ref, bb1_ref, v1_ref, r1_ref, kk1_ref))
    outs = (y0_ref, y1_ref)
    items = [(c, bb, d, p) for c in range(group) for bb in range(nb) for p in range(npair) for d in range(2)]
    rows = lambda c, d: slice((c if d == 0 else group - 1 - c) * chunk,
                              ((c if d == 0 else group - 1 - c) + 1) * chunk)

    def blockdiag(x):
        return jnp.concatenate([jnp.where(halves[0], x, 0.0), jnp.where(halves[1], x, 0.0)],
                               axis=0).astype(BF16)

    rounds = int(math.log2(chunk)) - 1

    def phase1(sel):
        feats = {}
        for c, bb, d in sorted({items[n][:3] for n in sel}):
            lw_ref, kd_ref, bb_ref, v_ref, r_ref, kk_ref = ins[d]
            rs = rows(c, d)
            lw = lw_ref[bb, rs, :]
            cum_in = sum(_dot(cum_mat[d], part) for part in _split3(lw))
            total = jnp.sum(lw, axis=0, keepdims=True)
            kd = kd_ref[bb, rs, :].astype(F32)
            bv = bb_ref[bb, rs, :].astype(F32)
            inv_p = jnp.exp(-cum_in)
            to_end = jnp.exp(total - cum_in)
            feats[c, bb, d] = dict(a=-kk_ref[bb, rs, :].astype(F32) * jnp.exp(cum_in - lw),
                                   r=r_ref[bb, rs, :].astype(F32) * jnp.exp(cum_in),
                                   b=bv * inv_p, k=kd * inv_p, v=v_ref[bb, rs, :].astype(F32),
                                   bend=bv * to_end, kend=kd * to_end, dec=jnp.exp(total))
        yield

        def per_item(name):
            return [feats[items[n][:3]][name][:, items[n][3] * LANES:(items[n][3] + 1) * LANES] for n in sel]

        dirs = [items[n][2] for n in sel]
        a = per_item("a")
        v = per_item("v")
        v_bd = [blockdiag(x) for x in v]
        ar = [jnp.concatenate(pair, axis=0).astype(BF16) for pair in zip(a, per_item("r"))]
        bk = [jnp.concatenate([blockdiag(x), blockdiag(y)], axis=0)
              for x, y in zip(per_item("b"), per_item("k"))]
        gram = [_dot_nt(x, y) for x, y in zip(ar, bk)]
        l_ab = [jnp.where(strict[d], g[:chunk, :LANES], 0.0) for d, g in zip(dirs, gram)]
        l_ak = [jnp.where(strict[d], g[:chunk, LANES:], 0.0).astype(BF16) for d, g in zip(dirs, gram)]
        for n, d, g in zip(sel, dirs, gram):
            mr_ref[n] = jnp.where(jnp.concatenate([incl[d], incl[d]], axis=1), g[chunk:, :], 0.0).astype(BF16)
        lv = [_dot(x, y) for x, y in zip(l_ak, v_bd)]
        yield

        power = [_dot(x.astype(BF16), blockdiag(x)) for x in l_ab]
        inv = [eye + x for x in l_ab]
        yield
        for j in range(rounds):
            pd = [blockdiag(x) for x in power]
            if j == rounds - 1:
                inv = [t + _dot(t.astype(BF16), x) for t, x in zip(inv, pd)]
            else:
                both = [_dot(jnp.concatenate([x, t], axis=0).astype(BF16), y)
                        for x, t, y in zip(power, inv, pd)]
                power = [x[:chunk] for x in both]
                inv = [t + x[chunk:] for t, x in zip(inv, both)]
            yield
        ta = [_dot(t.astype(BF16), jnp.concatenate([blockdiag(x), blockdiag(y)], axis=1))
              for t, x, y in zip(inv, a, lv)]
        for i, n in enumerate(sel):
            c, bb, d, p = items[n]
            cols = slice(p * LANES, (p + 1) * LANES)
            f = feats[c, bb, d]
            gr_ref[n] = jnp.concatenate([ta[i][:, :LANES].astype(BF16), ar[i][chunk:]], axis=0)
            u0_ref[n] = ta[i][:, LANES:]
            vbd_ref[n] = v_bd[i]
            ev_ref[n] = jnp.concatenate([f["bend"][:, cols], f["kend"][:, cols], v[i]], axis=0).astype(BF16)
            dec_ref[n] = jnp.broadcast_to(f["dec"][:, cols], (8, LANES))
        yield

    n_stages = rounds + 4
    lag = 2
    item_halves = [[n for n, it in enumerate(items) if it[1] % 2 == h] for h in range(min(nb, 2))]
    gens = [phase1(sel) for sel in item_halves]
    for step in range(n_stages + lag * (len(gens) - 1)):
        for g, gen in enumerate(gens):
            if 0 <= step - lag * g < n_stages:
                next(gen)

    per_chunk = 2 * npair * nb
    for c in range(group):
        ns = range(c * per_chunk, (c + 1) * per_chunk)
        states = [s_ref[j] for j in range(per_chunk)]
        prod = [_dot_nt(gr_ref[n], s.astype(BF16)) for n, s in zip(ns, states)]
        u = [x[:chunk] + u0_ref[n] for n, x in zip(ns, prod)]
        y = [x[chunk:] + _dot(mr_ref[n], jnp.concatenate([blockdiag(w), vbd_ref[n]], axis=0))
             for n, x, w in zip(ns, prod, u)]
        upd = [_dot_tn(jnp.concatenate([w.astype(BF16), ev_ref[n, 2 * chunk:, :]], axis=0),
                       ev_ref[n, :2 * chunk, :]) for n, w in zip(ns, u)]
        for j, n in enumerate(ns):
            _, bb, d, p = items[n]
            outs[d][bb, rows(c, d), p * LANES:(p + 1) * LANES] = y[j]
            s_ref[j] = states[j] * dec_ref[n, 0:1, :] + jnp.where(same_head, upd[j], 0.0)


def _rwkv_scan(feats, batch, seq):
    chunk = min(RWKV_CHUNK, seq)
    group = min(RWKV_GROUP, seq // chunk)
    tb = chunk * group
    nt = seq // tb
    nb = 2 if batch % 2 == 0 else 1
    n_items = group * RWKV_HEADS * nb
    fwd = pl.BlockSpec((nb, tb, D_RWKV), lambda b, i: (b, i, 0))
    bwd = pl.BlockSpec((nb, tb, D_RWKV), lambda b, i: (b, nt - 1 - i, 0))
    out_sd = jax.ShapeDtypeStruct((batch, seq, D_RWKV), F32)
    lw0, lw1, kd0, kd1, bb0, bb1, xv, xr, kk = [x.reshape(batch, seq, D_RWKV) for x in feats]
    y0, y1 = pl.pallas_call(
        functools.partial(_rwkv_scan_kernel, chunk=chunk, group=group, nb=nb),
        grid=(batch // nb, nt),
        in_specs=[fwd] * 6 + [bwd] * 6,
        out_specs=[fwd, bwd],
        out_shape=[out_sd, out_sd],
        scratch_shapes=[pltpu.VMEM((RWKV_HEADS * nb, LANES, LANES), F32),
                        pltpu.VMEM((n_items, 2 * chunk, LANES), BF16),
                        pltpu.VMEM((n_items, chunk, LANES), F32),
                        pltpu.VMEM((n_items, chunk, 4 * chunk), BF16),
                        pltpu.VMEM((n_items, 2 * chunk, LANES), BF16),
                        pltpu.VMEM((n_items, 3 * chunk, LANES), BF16),
                        pltpu.VMEM((n_items, 8, LANES), F32)],
        compiler_params=_cparams(("parallel", "arbitrary")),
        name="rwkv_scan",
    )(lw0, kd0, bb0, xv, xr, kk, lw1, kd1, bb1, xv, xr, kk)
    return y0.reshape(batch * seq, D_RWKV), y1.reshape(batch * seq, D_RWKV)


def _diff_attn_kernel(q_ref, k_ref, v_ref, band_ref, lam_ref, subw_ref, o_ref, kb_ref, vt_ref,
                      acc_ref, s_ref, *, seq, tq, tk, nblk, lambda_init):
    i = pl.program_id(2)

    @pl.when(i == 0)
    def _():
        kb_ref[...] = k_ref[...].astype(BF16)
        vt_ref[...] = jnp.transpose(v_ref[...]).astype(BF16)

    lane = lax.broadcasted_iota(jnp.int32, (1, LANES), 1)
    lv = lam_ref[...]
    lam = (jnp.exp(jnp.sum(lv[0:1] * lv[1:2], axis=1, keepdims=True))
           - jnp.exp(jnp.sum(lv[2:3] * lv[3:4], axis=1, keepdims=True)) + lambda_init)
    def one_block(blk):
        qi = i * nblk + blk
        q = q_ref[blk * tq:(blk + 1) * tq, :] * (DIFF_QK_DIM ** -0.5 * LOG2E)
        ones_rows = jnp.ones((BF16_ROWS, tk), BF16)
        n_maps = 4
        qms = [jnp.where((lane >= m * DIFF_QK_DIM) & (lane < (m + 1) * DIFF_QK_DIM), q, 0.0).astype(BF16)
               for m in range(n_maps)]
        n_tiles = seq // tk
        acc_ref[blk] = jnp.zeros(acc_ref.shape[1:], F32)

        def tile_index(j):
            t = qi + j
            return jnp.where(t >= n_tiles, t - n_tiles, t)

        def stage_scores(j, m):
            start = pl.multiple_of(tile_index(j) * tk, tk)
            s_ref[blk, j % 2, m] = _dot_nt(kb_ref[pl.ds(start, tk), :], qms[m])

        m_run = [jnp.full((1, tq), -1e30, F32)] * n_maps
        for m in range(n_maps):
            stage_scores(0, m)
        for j in range(n_tiles):
            slot = j % 2
            t = tile_index(j)
            start = pl.multiple_of(t * tk, tk)
            far = 2 <= j <= n_tiles - 2
            rows = BF16_ROWS // 2 if far else tk
            band_start = pl.multiple_of((jnp.clip(t - qi, -2, 2) + 2) * tk, tk)
            bias = [band_ref[hh, pl.ds(band_start, rows), :] for hh in range(2)]
            vt = [jnp.concatenate([vt_ref[hh * HEAD_DIM:(hh + 1) * HEAD_DIM, pl.ds(start, tk)],
                                   ones_rows], axis=0) for hh in range(2)]
            for m in range(n_maps):
                if j + 1 < n_tiles:
                    stage_scores(j + 1, m)
                if far:
                    s = s_ref[blk, slot, m]
                    const = bias[m // 2][0:1, :]
                    m_new = jnp.maximum(m_run[m], jnp.max(s, axis=0, keepdims=True) + const)
                    e = jnp.exp2(s - (m_new - const)).astype(BF16)
                else:
                    s = s_ref[blk, slot, m] + bias[m // 2]
                    m_new = jnp.maximum(m_run[m], jnp.max(s, axis=0, keepdims=True))
                    e = jnp.exp2(s - m_new).astype(BF16)
                pv = _dot(vt[m // 2], e)
                acc_ref[blk, m] = acc_ref[blk, m] * jnp.exp2(m_run[m] - m_new) + pv
                m_run[m] = m_new
        parts = [acc_ref[blk, m, :HEAD_DIM, :] / acc_ref[blk, m, HEAD_DIM:HEAD_DIM + 1, :] for m in range(n_maps)]
        heads = []
        for hh in range(2):
            head = parts[2 * hh] - lam * parts[2 * hh + 1]
            ms = jnp.mean(head * head, axis=0, keepdims=True)
            heads.append(head * lax.rsqrt(ms + NORM_EPS))
        out = jnp.transpose(jnp.concatenate(heads, axis=0))
        o_ref[blk * tq:(blk + 1) * tq, :] = (out * subw_ref[...] * (1.0 - lambda_init)).astype(BF16)

    for blk in range(nblk):
        one_block(blk)


def _diff_attention(proj, band, lam_vecs, subln_w, lambda_init, batch, seq):
    tq = min(ATTN_TQ, seq)
    nq = seq // tq
    nblk = ATTN_BLOCKS if nq % ATTN_BLOCKS == 0 else 1
    nsteps = nq // nblk
    npair = DIFF_HEADS // 2
    col0 = (4 * D_RET + 4 * D_RWKV) // LANES
    width = band.shape[1]
    tk = tq
    assert 2 * _bucket_saturation_distance() < tk
    return pl.pallas_call(
        functools.partial(_diff_attn_kernel, seq=seq, tq=tq, tk=tk, nblk=nblk, lambda_init=lambda_init),
        grid=(batch, npair, nsteps),
        in_specs=[pl.BlockSpec((nblk * tq, LANES), lambda b, p, i: (b * nsteps + i, col0 + p)),
                  pl.BlockSpec((seq, LANES), lambda b, p, i: (b, col0 + npair + p)),
                  pl.BlockSpec((seq, LANES), lambda b, p, i: (b, col0 + 2 * npair + p)),
                  pl.BlockSpec((2, width, tq), lambda b, p, i: (p, 0, 0)),
                  pl.BlockSpec((4, DIFF_QK_DIM), lambda b, p, i: (0, 0)),
                  pl.BlockSpec((1, LANES), lambda b, p, i: (0, 0))],
        out_specs=pl.BlockSpec((nblk * tq, LANES), lambda b, p, i: (b * nsteps + i, p)),
        out_shape=jax.ShapeDtypeStruct((batch * seq, D_DIFF), BF16),
        scratch_shapes=[pltpu.VMEM((seq, LANES), BF16), pltpu.VMEM((LANES, seq), BF16),
                        pltpu.VMEM((nblk, 4, HEAD_DIM + BF16_ROWS, tq), F32),
                        pltpu.VMEM((nblk, 2, 4, tk, tq), F32)],
        compiler_params=_cparams(("parallel", "parallel", "arbitrary")),
        name="diff_attention",
    )(proj, proj, proj, band, lam_vecs, jnp.tile(subln_w, 2).reshape(1, LANES))


def _t5_bucket(rel):
    nb = REL_BUCKETS // 2
    max_exact = nb // 2
    n = jnp.abs(rel)
    nf = jnp.maximum(n, 1).astype(jnp.float32)
    large = max_exact + (jnp.log(nf / max_exact) / math.log(REL_MAX_DIST / max_exact)
                         * (nb - max_exact)).astype(jnp.int32)
    large = jnp.minimum(large, nb - 1)
    return jnp.where(rel > 0, nb, 0) + jnp.where(n < max_exact, n, large)


def _bucket_saturation_distance():
    nb = REL_BUCKETS // 2
    max_exact = nb // 2
    n = np.arange(1, 4 * REL_MAX_DIST)
    large = max_exact + (np.log(n / max_exact) / math.log(REL_MAX_DIST / max_exact)
                         * (nb - max_exact)).astype(np.int64)
    bucket = np.where(n < max_exact, n, np.minimum(large, nb - 1))
    return int(n[bucket < nb - 1].max())


def _bias_band(rel_bias, tq):
    width = 5 * tq
    period = 6 * tq - 1
    m = jnp.arange(period, dtype=jnp.int32)
    rel = jnp.where(m < width, m, m - period) - 2 * tq
    vec = rel_bias.astype(F32)[_t5_bucket(rel)].T * LOG2E
    rows = jnp.tile(vec, (1, tq))[:, :tq * (period - 1)].reshape(-1, tq, period - 1)
    return jnp.transpose(rows[:, :, :width], (0, 2, 1))


def _outproj_kernel(x_ref, yr_ref, y0_ref, y1_ref, bonus_ref, gate_ref, yd_ref, lnw_ref, lnb_ref,
                    w_ref, o_ref):
    avg = _head_avg_matrix()
    mixed = [yr_ref[...]]
    for p in range(RWKV_HEADS // 2):
        cols = slice(p * LANES, (p + 1) * LANES)
        y = _group_norm(y0_ref[:, cols] + y1_ref[:, cols], avg, lnw_ref[:, cols], lnb_ref[:, cols],
                        RWKV_GN_EPS)
        mixed.append(((y + bonus_ref[:, cols]) * gate_ref[:, cols]).astype(BF16))
    mixed.append(yd_ref[...])
    o_ref[...] = x_ref[...] + _dot(jnp.concatenate(mixed, axis=1), w_ref[...])


def _outproj(x2, y_ret, y0, y1, bonus, gate, y_diff, ln_w, ln_b, w_bf16, tm=1024):
    m, d = x2.shape
    tm = min(tm, m)
    row = lambda n: pl.BlockSpec((tm, n), lambda i: (i, 0))
    vec = pl.BlockSpec((1, D_RWKV), lambda i: (0, 0))
    return pl.pallas_call(
        _outproj_kernel,
        grid=(m // tm,),
        in_specs=[row(d), row(D_RET), row(D_RWKV), row(D_RWKV), row(D_RWKV), row(D_RWKV), row(D_DIFF),
                  vec, vec, pl.BlockSpec(w_bf16.shape, lambda i: (0, 0))],
        out_specs=row(d),
        out_shape=jax.ShapeDtypeStruct((m, d), F32),
        compiler_params=_cparams(("parallel",)),
        name="outproj",
    )(x2, y_ret, y0, y1, bonus, gate, y_diff, ln_w.reshape(1, D_RWKV), ln_b.reshape(1, D_RWKV), w_bf16)


def _ffn_up_kernel(x_ref, g_ref, wg_ref, wu_ref, o_ref, *, tf):
    x = x_ref[...]
    ms = jnp.mean(x * x, axis=-1, keepdims=True)
    h = (x * lax.rsqrt(ms + NORM_EPS) * g_ref[...]).astype(BF16)
    for j in range(wg_ref.shape[1] // tf):
        cols = slice(j * tf, (j + 1) * tf)
        gate = _dot(h, wg_ref[:, cols])
        up = _dot(h, wu_ref[:, cols])
        o_ref[:, cols] = (gate * jax.nn.sigmoid(gate) * up).astype(BF16)


def _ffn_up(x2, g, wg_bf16, wu_bf16, tm=1024, tf=256):
    m, d = x2.shape
    tm = min(tm, m)
    f = wg_bf16.shape[1]
    return pl.pallas_call(
        functools.partial(_ffn_up_kernel, tf=tf),
        grid=(m // tm,),
        in_specs=[pl.BlockSpec((tm, d), lambda i: (i, 0)),
                  pl.BlockSpec((1, d), lambda i: (0, 0)),
                  pl.BlockSpec((d, f), lambda i: (0, 0)),
                  pl.BlockSpec((d, f), lambda i: (0, 0))],
        out_specs=pl.BlockSpec((tm, f), lambda i: (i, 0)),
        out_shape=jax.ShapeDtypeStruct((m, f), BF16),
        compiler_params=_cparams(("parallel",)),
        name="ffn_up",
    )(x2, g.reshape(1, d), wg_bf16, wu_bf16)


def _ffn_down_kernel(x_ref, h_ref, w_ref, g_ref, o_ref, *, final_norm):
    y = x_ref[...] + _dot(h_ref[...], w_ref[...])
    if final_norm:
        ms = jnp.mean(y * y, axis=-1, keepdims=True)
        y = y * lax.rsqrt(ms + NORM_EPS) * g_ref[...]
    o_ref[...] = y


def _ffn_down(x2, hidden, w_bf16, g, final_norm, tm=1024):
    m, d = x2.shape
    tm = min(tm, m)
    f = hidden.shape[1]
    return pl.pallas_call(
        functools.partial(_ffn_down_kernel, final_norm=final_norm),
        grid=(m // tm,),
        in_specs=[pl.BlockSpec((tm, d), lambda i: (i, 0)),
                  pl.BlockSpec((tm, f), lambda i: (i, 0)),
                  pl.BlockSpec((f, d), lambda i: (0, 0)),
                  pl.BlockSpec((1, d), lambda i: (0, 0))],
        out_specs=pl.BlockSpec((tm, d), lambda i: (i, 0)),
        out_shape=jax.ShapeDtypeStruct((m, d), F32),
        compiler_params=_cparams(("parallel",)),
        name="ffn_down",
    )(x2, hidden, w_bf16, g.reshape(1, d))


def _rope_tables(seq):
    half = HEAD_DIM // 2
    freqs = ROPE_BASE ** (-jnp.arange(half, dtype=F32) / half)
    ang = jnp.arange(seq, dtype=jnp.int32).astype(F32)[:, None] * freqs[None, :]
    cos = jnp.cos(ang)
    sin = jnp.sin(ang)
    cos_tab = jnp.tile(cos, (1, LANES // half))
    sin_tab = jnp.concatenate([-sin, -sin, sin, sin], axis=1)
    return cos_tab, sin_tab


def _retention_qk_layout(w):
    d = w.shape[0]
    half = HEAD_DIM // 2
    qk = w[:, :2 * D_RET].reshape(d, 2, RET_HEADS // 2, 2, 2, half)
    qk = jnp.swapaxes(qk, 3, 4).reshape(d, 2 * D_RET)
    return jnp.concatenate([qk, w[:, 2 * D_RET:]], axis=1)


def kernel(x, mix_norm_g, w_in, w_out, ret_gn_w, ret_gn_b, rwkv_mu, rwkv_w0, rwkv_w1, rwkv_w2, rwkv_a0, rwkv_a1, rwkv_a2, rwkv_g1, rwkv_g2, rwkv_k_k, rwkv_k_a, rwkv_r_k, rwkv_ln_w, rwkv_ln_b, diff_lambda, diff_subln_w, rel_bias, ffn_norm_g, w_gate, w_up, w_down, final_norm_g):
    batch, seq, d = x.shape
    depth = w_in.shape[0]
    cos_tab, sin_tab = _rope_tables(seq)
    band = _bias_band(rel_bias, min(ATTN_TQ, seq))
    x2 = x.reshape(batch * seq, d)
    for l in range(depth):
        proj = _norm_matmul(x2, mix_norm_g[l], _retention_qk_layout(w_in[l]).astype(BF16))
        y_ret = _retention(proj, cos_tab, sin_tab, ret_gn_w[l], ret_gn_b[l], batch, seq)
        feats = _rwkv_prep(proj, rwkv_mu[l], rwkv_w0[l], rwkv_w1[l], rwkv_w2[l], rwkv_a0[l],
                           rwkv_a1[l], rwkv_a2[l], rwkv_g1[l], rwkv_g2[l], rwkv_k_k[l],
                           rwkv_k_a[l], rwkv_r_k[l].reshape(-1), batch, seq)
        y0, y1 = _rwkv_scan(feats[:9], batch, seq)
        lambda_init = 0.8 - 0.6 * math.exp(-0.3 * l)
        y_diff = _diff_attention(proj, band, diff_lambda[l], diff_subln_w[l], lambda_init,
                                 batch, seq)
        x2 = _outproj(x2, y_ret, y0, y1, feats[10], feats[9], y_diff, rwkv_ln_w[l], rwkv_ln_b[l],
                      w_out[l].astype(BF16))
        hidden = _ffn_up(x2, ffn_norm_g[l], w_gate[l].astype(BF16), w_up[l].astype(BF16))
        x2 = _ffn_down(x2, hidden, w_down[l].astype(BF16), final_norm_g, l == depth - 1)
    return x2.reshape(batch, seq, d)
```
